```python
import jax, jax.numpy as jnp
from jax import lax
import numpy as np

D_MODEL = 1024
BATCH = 32
SEQ = 256
DEPTH = 2
DEC_BATCH = 4
DEC_SEQ = 2048
PAST_LEN = 512

GRID_W = 64
MLA_HEADS = 4
MLA_NOPE = 64
MLA_ROPE = 32
MLA_V = 64
MLA_Q_LORA = 192
MLA_KV_LORA = 128
MLA_SCALE = (MLA_NOPE + MLA_ROPE) ** -0.5
NA_HEADS = 4
NA_HEAD_DIM = 64
NA_WIN_H = 8
NA_WIN_W = 16
NA_QCB = 16
NA_KCB = NA_QCB + NA_WIN_W
NA_NCB = GRID_W // NA_QCB
NA_SCALE = NA_HEAD_DIM ** -0.5
GLA_HEADS = 4
GLA_DK = 32
GLA_DV = 64
GLA_GATE_RANK = 16
GLA_GATE_NORM = 16.0
GLA_CHUNK = 64
LRU_WIDTH = 256
LRU_BLOCKS = 4
LRU_BLOCK = LRU_WIDTH // LRU_BLOCKS
LRU_CONV = 4
LRU_C = 8.0
D_MIX = MLA_HEADS * MLA_V + NA_HEADS * NA_HEAD_DIM + GLA_HEADS * GLA_DV + LRU_WIDTH
IN_SPLIT = (MLA_Q_LORA, MLA_KV_LORA, MLA_ROPE, 3 * NA_HEADS * NA_HEAD_DIM, GLA_HEADS * GLA_DK, GLA_HEADS * GLA_DK, GLA_HEADS * GLA_DV, GLA_HEADS * GLA_DV, 2 * GLA_GATE_RANK, LRU_WIDTH, LRU_WIDTH)
IN_COLS = sum(IN_SPLIT)
N_EXPERTS = 32
TOP_K = 4
D_EXPERT = 1024
SWIGLU_ALPHA = 1.702
SWIGLU_LIMIT = 7.0
Q_BLOCK = 128
ROPE_BASE = 10000.0
EPS = 1e-6

kernel_name = 'hybrid_diffusion_prefix_step'


def _rms(x32):
    return x32 * lax.rsqrt(jnp.mean(x32 * x32, axis=-1, keepdims=True) + EPS)


def rmsnorm(x, g):
    return (_rms(x.astype(jnp.float32)) * g.astype(jnp.float32)).astype(x.dtype)


def axial_rope(x):
    n = x.shape[1]
    t = jnp.arange(n)
    half = x.shape[-1] // 2
    nf = half // 2
    freqs = ROPE_BASE ** (-jnp.arange(nf, dtype=jnp.float32) / nf)
    shape = (1, n) + (1,) * (x.ndim - 3) + (nf,)
    outs = []
    for axis, pos in enumerate((t // GRID_W, t % GRID_W)):
        seg = x[..., axis * half:(axis + 1) * half].astype(jnp.float32)
        ang = pos.astype(jnp.float32)[:, None] * freqs[None, :]
        cos = jnp.cos(ang).reshape(shape)
        sin = jnp.sin(ang).reshape(shape)
        x1, x2 = seg[..., :nf], seg[..., nf:]
        outs += [x1 * cos - x2 * sin, x2 * cos + x1 * sin]
    return jnp.concatenate(outs, axis=-1).astype(x.dtype)


def block_attention(q, k, v, scale):
    b, sq, h, dk = q.shape
    nb = sq // Q_BLOCK
    qb = jnp.moveaxis(q.reshape(b, nb, Q_BLOCK, h, dk), 1, 0)

    def one(qi):
        s = jnp.einsum('bqhd,bkhd->bhqk', qi, k).astype(jnp.float32) * scale
        p = jax.nn.softmax(s, axis=-1).astype(v.dtype)
        return jnp.einsum('bhqk,bkhd->bqhd', p, v)

    o = lax.map(one, qb)
    return jnp.moveaxis(o, 0, 1).reshape(b, sq, h, v.shape[-1])


def na_latent(q, k, v, k_ctx, v_ctx, rpb):
    b, n, h, dh = q.shape
    rows = n // GRID_W
    wr = min(NA_WIN_H, rows)
    r = jnp.arange(rows)
    rs = jnp.clip(r - wr // 2, 0, rows - wr)
    key_rows = rs[:, None] + jnp.arange(wr)[None, :]
    j = jnp.arange(NA_NCB)
    band = jnp.clip(j * NA_QCB - NA_WIN_W // 2, 0, GRID_W - NA_KCB)
    key_cols = band[:, None] + jnp.arange(NA_KCB)[None, :]
    key_idx = (key_rows[:, None, :, None] * GRID_W + key_cols[None, :, None, :]).reshape(rows, NA_NCB, wr * NA_KCB)
    kg = k[:, key_idx]
    vg = v[:, key_idx]
    qb = q.reshape(b, rows, NA_NCB, NA_QCB, h, dh)
    qc = j[:, None] * NA_QCB + jnp.arange(NA_QCB)[None, :]
    cs = jnp.clip(qc - NA_WIN_W // 2, 0, GRID_W - NA_WIN_W)
    kc = key_cols[:, None, :]
    col_ok = (kc >= cs[:, :, None]) & (kc < cs[:, :, None] + NA_WIN_W)
    valid = jnp.broadcast_to(col_ok[:, :, None, :], (NA_NCB, NA_QCB, wr, NA_KCB)).reshape(NA_NCB, NA_QCB, wr * NA_KCB)
    dri = (key_rows - r[:, None]) + NA_WIN_H - 1
    dci = jnp.clip(kc - qc[:, :, None], -(NA_WIN_W - 1), NA_WIN_W - 1) + NA_WIN_W - 1
    bias = rpb[:, dri[:, None, None, :, None], dci[None, :, :, None, :]]
    bias = bias.reshape(h, rows, NA_NCB, NA_QCB, wr * NA_KCB).astype(jnp.float32)
    s_loc = jnp.einsum('brjqhd,brjkhd->bhrjqk', qb, kg).astype(jnp.float32) * NA_SCALE + bias[None]
    s_loc = jnp.where(valid[None, None, None], s_loc, -jnp.inf)
    s_ctx = jnp.einsum('brjqhd,blhd->bhrjql', qb, k_ctx).astype(jnp.float32) * NA_SCALE
    p = jax.nn.softmax(jnp.concatenate([s_loc, s_ctx], axis=-1), axis=-1).astype(v.dtype)
    nk = wr * NA_KCB
    o = jnp.einsum('bhrjqk,brjkhd->brjqhd', p[..., :nk], vg) + jnp.einsum('bhrjql,blhd->brjqhd', p[..., nk:], v_ctx)
    return o.reshape(b, n, h, dh)


def _gla_chunked(q, k, v, log_a, s0):
    b, t, h, dk = q.shape
    dv = v.shape[-1]
    nc = t // GLA_CHUNK

    def chunks(a):
        return a.reshape(b, nc, GLA_CHUNK, h, a.shape[-1]).transpose(1, 0, 3, 2, 4)

    tri = jnp.tril(jnp.ones((GLA_CHUNK, GLA_CHUNK), dtype=bool))[:, :, None]

    def step(s, inp):
        qc, kc, vc, lc = inp
        cb = jnp.cumsum(lc, axis=2)
        o_inter = jnp.einsum('bhik,bhkv->bhiv', qc * jnp.exp(cb), s)
        dec = jnp.exp(jnp.where(tri, cb[:, :, :, None, :] - cb[:, :, None, :, :], -jnp.inf))
        att = jnp.einsum('bhik,bhjk,bhijk->bhij', qc, kc, dec)
        o_intra = jnp.einsum('bhij,bhjv->bhiv', att, vc)
        c_last = cb[:, :, -1:, :]
        s_new = jnp.exp(c_last[:, :, 0, :])[..., None] * s + jnp.einsum('bhjk,bhjv->bhkv', kc * jnp.exp(c_last - cb), vc)
        return s_new, o_inter + o_intra

    s_fin, o = lax.scan(step, s0, (chunks(q), chunks(k), chunks(v), chunks(log_a)))
    o = o.transpose(1, 0, 3, 2, 4).reshape(b, t, h, dv)
    return o, s_fin


def gla_mixer(pr, lp, s0):
    gq = pr['gq']
    b, s = gq.shape[:2]
    q = gq.reshape(b, s, GLA_HEADS, GLA_DK).astype(jnp.float32) * GLA_DK ** -0.5
    k = pr['gk'].reshape(b, s, GLA_HEADS, GLA_DK).astype(jnp.float32)
    v = pr['gv'].reshape(b, s, GLA_HEADS, GLA_DV).astype(jnp.float32)
    ga = pr['ga'].reshape(b, s, 2, GLA_GATE_RANK)
    outs, finals = [], []
    for d in range(2):
        la = jax.nn.log_sigmoid((ga[:, :, d] @ lp['gla_w_gate'][d] + lp['gla_b_gate'][d]).astype(jnp.float32)) / GLA_GATE_NORM
        seqs = (q, k, v, la.reshape(b, s, GLA_HEADS, GLA_DK))
        if d == 1:
            seqs = tuple(jnp.flip(a, axis=1) for a in seqs)
        o, s_fin = _gla_chunked(*seqs, s0[:, d])
        if d == 1:
            o = jnp.flip(o, axis=1)
        outs.append(o)
        finals.append(s_fin)
    o = _rms(outs[0] + outs[1]).reshape(b, s, GLA_HEADS * GLA_DV)
    o = o * lp['gla_norm_g'].astype(jnp.float32) * jax.nn.silu(pr['gg'].astype(jnp.float32))
    return o.astype(gq.dtype), jnp.stack(finals, axis=1)


def _linear_scan(a, bt, h0):
    def comb(e1, e2):
        a1, b1 = e1
        a2, b2 = e2
        return a1 * a2, a2 * b1 + b2
    acum, bcum = lax.associative_scan(comb, (a, bt), axis=1)
    h = acum * h0[:, None, :] + bcum
    return h, h[:, -1]


def lru_mixer(lx, lg, lp, h0):
    b, s, _ = lx.shape
    xp = jnp.pad(lx, ((0, 0), (1, 2), (0, 0)))
    xc = lp['lru_conv_b'] + xp[:, 0:s] * lp['lru_conv_w'][0]
    for i in range(1, LRU_CONV):
        xc = xc + xp[:, i:i + s] * lp['lru_conv_w'][i]
    xb = xc.reshape(b, s, LRU_BLOCKS, LRU_BLOCK)
    x32 = xc.astype(jnp.float32)
    hs, finals = [], []
    for d in range(2):
        r = jax.nn.sigmoid((jnp.einsum('bsnk,nkj->bsnj', xb, lp['lru_w_a'][d]).reshape(b, s, LRU_WIDTH) + lp['lru_b_a'][d]).astype(jnp.float32))
        ig = jax.nn.sigmoid((jnp.einsum('bsnk,nkj->bsnj', xb, lp['lru_w_x'][d]).reshape(b, s, LRU_WIDTH) + lp['lru_b_x'][d]).astype(jnp.float32))
        log_a = -LRU_C * r * jax.nn.softplus(-lp['lru_lambda'][d].astype(jnp.float32))
        bt = jnp.sqrt(-jnp.expm1(2.0 * log_a)) * (ig * x32)
        a = jnp.exp(log_a)
        if d == 1:
            a, bt = jnp.flip(a, axis=1), jnp.flip(bt, axis=1)
        h, h_fin = _linear_scan(a, bt, h0[:, d])
        if d == 1:
            h = jnp.flip(h, axis=1)
        hs.append(h)
        finals.append(h_fin)
    y = (hs[0] + hs[1]) * jax.nn.gelu(lg.astype(jnp.float32))
    return y.astype(lx.dtype), jnp.stack(finals, axis=1)


def _project(h, lp):
    b, s, _ = h.shape
    p = h @ lp['w_in']
    parts, o = [], 0
    for size in IN_SPLIT:
        parts.append(p[..., o:o + size])
        o += size
    pq, pkv, pkr, pna, gq, gk, gv, gg, ga, lx, lg = parts
    mq = (rmsnorm(pq, lp['mla_q_norm_g']) @ lp['mla_w_uq']).reshape(b, s, MLA_HEADS, MLA_NOPE + MLA_ROPE)
    na = pna.reshape(b, s, 3, NA_HEADS, NA_HEAD_DIM)
    return dict(mq=mq, ckv=rmsnorm(pkv, lp['mla_kv_norm_g']), kr=pkr, nq=na[:, :, 0], nk=na[:, :, 1], nv=na[:, :, 2],
                gq=gq, gk=gk, gv=gv, gg=gg, ga=ga, lx=lx, lg=lg)


def _mla_kv(ckv, kr, lp):
    b, s, _ = ckv.shape
    kv = (ckv @ lp['mla_w_ukv']).reshape(b, s, MLA_HEADS, MLA_NOPE + MLA_V)
    k = jnp.concatenate([kv[..., :MLA_NOPE], jnp.broadcast_to(kr[:, :, None, :], (b, s, MLA_HEADS, MLA_ROPE)).astype(kv.dtype)], axis=-1)
    return k, kv[..., MLA_NOPE:]


def _merge(o_a, o_b, o_c, o_d, lp):
    b, s = o_c.shape[:2]
    cat = jnp.concatenate([o_a.reshape(b, s, -1), o_b.reshape(b, s, -1), o_c, o_d], axis=-1)
    return cat @ lp['w_out']


def _context_mixers(h, lp):
    b = h.shape[0]
    pr = _project(h, lp)
    k_a, v_a = _mla_kv(pr['ckv'], pr['kr'], lp)
    o_a = block_attention(pr['mq'], k_a, v_a, MLA_SCALE)
    o_b = block_attention(pr['nq'], pr['nk'], pr['nv'], NA_SCALE)
    o_c, s_gla = gla_mixer(pr, lp, jnp.zeros((b, 2, GLA_HEADS, GLA_DK, GLA_DV), jnp.float32))
    o_d, s_lru = lru_mixer(pr['lx'], pr['lg'], lp, jnp.zeros((b, 2, LRU_WIDTH), jnp.float32))
    return _merge(o_a, o_b, o_c, o_d, lp), (pr['ckv'], pr['kr'], pr['nk'], pr['nv'], s_gla, s_lru)


def _latent_mixers(h, lp, cache):
    ckv_c, kr_c, nk_c, nv_c, sg_c, sl_c = cache
    pr = _project(h, lp)
    mq = pr['mq']
    q_a = jnp.concatenate([mq[..., :MLA_NOPE], axial_rope(mq[..., MLA_NOPE:])], axis=-1)
    k_l, v_l = _mla_kv(pr['ckv'], axial_rope(pr['kr']), lp)
    k_c, v_c = _mla_kv(ckv_c, kr_c, lp)
    o_a = block_attention(q_a, jnp.concatenate([k_l, k_c.astype(k_l.dtype)], axis=1), jnp.concatenate([v_l, v_c.astype(v_l.dtype)], axis=1), MLA_SCALE)
    o_b = na_latent(pr['nq'], pr['nk'], pr['nv'], nk_c, nv_c, lp['na_rpb'])
    o_c, _ = gla_mixer(pr, lp, sg_c.astype(jnp.float32))
    o_d, _ = lru_mixer(pr['lx'], pr['lg'], lp, sl_c.astype(jnp.float32))
    return _merge(o_a, o_b, o_c, o_d, lp)


def _moe(h, lp):
    b, s, d = h.shape
    t = h.reshape(b * s, d)
    logits = (t @ lp['router_w'] + lp['router_b']).astype(jnp.float32)
    top_v, top_i = lax.top_k(logits, TOP_K)
    gates = jnp.sum(jax.nn.one_hot(top_i, N_EXPERTS, dtype=jnp.float32) * jax.nn.softmax(top_v, axis=-1)[..., None], axis=1)
    out = jnp.zeros((b * s, d), jnp.float32)
    for e in range(N_EXPERTS):
        gu = t @ lp['exp_w_gu'][e] + lp['exp_b_gu'][e]
        xg = jnp.minimum(gu[:, :D_EXPERT].astype(jnp.float32), SWIGLU_LIMIT)
        xl = jnp.clip(gu[:, D_EXPERT:].astype(jnp.float32), -SWIGLU_LIMIT, SWIGLU_LIMIT)
        act = (xg * jax.nn.sigmoid(SWIGLU_ALPHA * xg) * (xl + 1.0)).astype(h.dtype)
        out = out + gates[:, e:e + 1] * (act @ lp['exp_w_down'][e] + lp['exp_b_down'][e]).astype(jnp.float32)
    return out.astype(h.dtype).reshape(b, s, d)


def _modulation(cvec, lp):
    m = jax.nn.silu(cvec) @ lp['w_mod'] + lp['b_mod']
    return jnp.split(m, 6, axis=-1)


def _layer(x, mods, mixer_fn, lp):
    sh1, sc1, g1, sh2, sc2, g2 = mods
    h = rmsnorm(x, lp['norm1_g']) * (1 + sc1) + sh1
    mix, extra = mixer_fn(h)
    x = x + g1 * mix
    h = rmsnorm(x, lp['norm2_g']) * (1 + sc2) + sh2
    x = x + g2 * _moe(h, lp)
    return x, extra


def setup_inputs(seed: int = 0) -> dict:
    key = jax.random.key(seed)
    ks = iter(jax.random.split(key, 64))
    f32 = jnp.float32
    D = D_MODEL

    def nrm(shape, scale):
        return jax.random.normal(next(ks), shape, f32) * scale

    x_prompt = nrm((BATCH, SEQ, D), 1.0)
    x_sample = nrm((DEC_BATCH, DEC_SEQ, D), 1.0)
    cache_mla_ckv = nrm((DEC_BATCH, DEPTH, PAST_LEN, MLA_KV_LORA), 1.0)
    cache_mla_krope = nrm((DEC_BATCH, DEPTH, PAST_LEN, MLA_ROPE), 1.0)
    cache_na_k = nrm((DEC_BATCH, DEPTH, PAST_LEN, NA_HEADS, NA_HEAD_DIM), 1.0)
    cache_na_v = nrm((DEC_BATCH, DEPTH, PAST_LEN, NA_HEADS, NA_HEAD_DIM), 1.0)
    state_gla = nrm((DEC_BATCH, DEPTH, 2, GLA_HEADS, GLA_DK, GLA_DV), 0.5)
    state_lru = nrm((DEC_BATCH, DEPTH, 2, LRU_WIDTH), 0.5)
    c = nrm((DEC_BATCH, D), 1.0)
    c_ctx = nrm((D,), 1.0)
    norm1_g = 1.0 + nrm((DEPTH, D), 0.05)
    norm2_g = 1.0 + nrm((DEPTH, D), 0.05)
    w_mod = nrm((DEPTH, D, 6 * D), 0.5 * D ** -0.5)
    b_mod = nrm((DEPTH, 6 * D), 0.02)
    w_in = nrm((DEPTH, D, IN_COLS), D ** -0.5)
    w_out = nrm((DEPTH, D_MIX, D), D_MIX ** -0.5)
    mla_q_norm_g = 1.0 + nrm((DEPTH, MLA_Q_LORA), 0.05)
    mla_w_uq = nrm((DEPTH, MLA_Q_LORA, MLA_HEADS * (MLA_NOPE + MLA_ROPE)), MLA_Q_LORA ** -0.5)
    mla_kv_norm_g = 1.0 + nrm((DEPTH, MLA_KV_LORA), 0.05)
    mla_w_ukv = nrm((DEPTH, MLA_KV_LORA, MLA_HEADS * (MLA_NOPE + MLA_V)), MLA_KV_LORA ** -0.5)
    na_rpb = nrm((DEPTH, NA_HEADS, 2 * NA_WIN_H - 1, 2 * NA_WIN_W - 1), 0.1)
    gla_w_gate = nrm((DEPTH, 2, GLA_GATE_RANK, GLA_HEADS * GLA_DK), GLA_GATE_RANK ** -0.5)
    gla_b_gate = nrm((DEPTH, 2, GLA_HEADS * GLA_DK), 0.1)
    gla_norm_g = 1.0 + nrm((DEPTH, GLA_HEADS * GLA_DV), 0.05)
    lru_conv_w = nrm((DEPTH, LRU_CONV, LRU_WIDTH), LRU_CONV ** -0.5)
    lru_conv_b = nrm((DEPTH, LRU_WIDTH), 0.02)
    lru_w_a = nrm((DEPTH, 2, LRU_BLOCKS, LRU_BLOCK, LRU_BLOCK), LRU_BLOCK ** -0.5)
    lru_b_a = nrm((DEPTH, 2, LRU_WIDTH), 0.02)
    lru_w_x = nrm((DEPTH, 2, LRU_BLOCKS, LRU_BLOCK, LRU_BLOCK), LRU_BLOCK ** -0.5)
    lru_b_x = nrm((DEPTH, 2, LRU_WIDTH), 0.02)
    a0 = jax.random.uniform(next(ks), (DEPTH, 2, LRU_WIDTH), f32, 0.9, 0.999)
    lru_lambda = jnp.log(a0) - jnp.log1p(-a0)
    router_w = nrm((DEPTH, D, N_EXPERTS), D ** -0.5)
    router_b = nrm((DEPTH, N_EXPERTS), 0.01)
    exp_w_gu = nrm((DEPTH, N_EXPERTS, D, 2 * D_EXPERT), D ** -0.5)
    exp_b_gu = nrm((DEPTH, N_EXPERTS, 2 * D_EXPERT), 0.02)
    exp_w_down = nrm((DEPTH, N_EXPERTS, D_EXPERT, D), D_EXPERT ** -0.5)
    exp_b_down = nrm((DEPTH, N_EXPERTS, D), 0.02)
    final_norm_g = 1.0 + nrm((D,), 0.05)
    return {'x_prompt': x_prompt, 'x_sample': x_sample, 'cache_mla_ckv': cache_mla_ckv, 'cache_mla_krope': cache_mla_krope,
            'cache_na_k': cache_na_k, 'cache_na_v': cache_na_v, 'state_gla': state_gla, 'state_lru': state_lru, 'c': c,
            'c_ctx': c_ctx, 'norm1_g': norm1_g, 'norm2_g': norm2_g, 'w_mod': w_mod, 'b_mod': b_mod, 'w_in': w_in, 'w_out': w_out,
            'mla_q_norm_g': mla_q_norm_g, 'mla_w_uq': mla_w_uq, 'mla_kv_norm_g': mla_kv_norm_g, 'mla_w_ukv': mla_w_ukv,
            'na_rpb': na_rpb, 'gla_w_gate': gla_w_gate, 'gla_b_gate': gla_b_gate, 'gla_norm_g': gla_norm_g,
            'lru_conv_w': lru_conv_w, 'lru_conv_b': lru_conv_b, 'lru_w_a': lru_w_a, 'lru_b_a': lru_b_a, 'lru_w_x': lru_w_x,
            'lru_b_x': lru_b_x, 'lru_lambda': lru_lambda, 'router_w': router_w, 'router_b': router_b, 'exp_w_gu': exp_w_gu,
            'exp_b_gu': exp_b_gu, 'exp_w_down': exp_w_down, 'exp_b_down': exp_b_down, 'final_norm_g': final_norm_g}


def reference(x_prompt, x_sample, cache_mla_ckv, cache_mla_krope, cache_na_k, cache_na_v, state_gla, state_lru, c,
              c_ctx, norm1_g, norm2_g, w_mod, b_mod, w_in, w_out, mla_q_norm_g, mla_w_uq, mla_kv_norm_g, mla_w_ukv,
              na_rpb, gla_w_gate, gla_b_gate, gla_norm_g, lru_conv_w, lru_conv_b, lru_w_a, lru_b_a, lru_w_x, lru_b_x,
              lru_lambda, router_w, router_b, exp_w_gu, exp_b_gu, exp_w_down, exp_b_down, final_norm_g):
    xp, xs = x_prompt, x_sample
    collected = [[] for _ in range(6)]
    for l in range(DEPTH):
        lp = dict(norm1_g=norm1_g[l], norm2_g=norm2_g[l], w_mod=w_mod[l], b_mod=b_mod[l], w_in=w_in[l], w_out=w_out[l],
                  mla_q_norm_g=mla_q_norm_g[l], mla_w_uq=mla_w_uq[l], mla_kv_norm_g=mla_kv_norm_g[l], mla_w_ukv=mla_w_ukv[l],
                  na_rpb=na_rpb[l], gla_w_gate=gla_w_gate[l], gla_b_gate=gla_b_gate[l], gla_norm_g=gla_norm_g[l],
                  lru_conv_w=lru_conv_w[l], lru_conv_b=lru_conv_b[l], lru_w_a=lru_w_a[l], lru_b_a=lru_b_a[l],
                  lru_w_x=lru_w_x[l], lru_b_x=lru_b_x[l], lru_lambda=lru_lambda[l], router_w=router_w[l],
                  router_b=router_b[l], exp_w_gu=exp_w_gu[l], exp_b_gu=exp_b_gu[l], exp_w_down=exp_w_down[l],
                  exp_b_down=exp_b_down[l])
        xp, st = _layer(xp, _modulation(c_ctx, lp), lambda h: _context_mixers(h, lp), lp)
        for lst, arr in zip(collected, st):
            lst.append(arr.astype(x_prompt.dtype))
        cache = (cache_mla_ckv[:, l], cache_mla_krope[:, l], cache_na_k[:, l], cache_na_v[:, l], state_gla[:, l], state_lru[:, l])
        xs, _ = _layer(xs, _modulation(c[:, None, :], lp), lambda h: (_latent_mixers(h, lp, cache), None), lp)
    y_prompt = rmsnorm(xp, final_norm_g)
    y_sample = rmsnorm(xs, final_norm_g)
    new_mla_ckv = jnp.stack(collected[0], axis=1)
    new_mla_krope = jnp.stack(collected[1], axis=1)
    new_na_k = jnp.stack(collected[2], axis=1)
    new_na_v = jnp.stack(collected[3], axis=1)
    new_state_gla = jnp.stack(collected[4], axis=1)
    new_state_lru = jnp.stack(collected[5], axis=1)
    return (y_prompt, y_sample, new_mla_ckv, new_mla_krope, new_na_k, new_na_v, new_state_gla, new_state_lru)
```

```python
import functools

import numpy as np
import jax
import jax.numpy as jnp
from jax import lax
from jax.experimental import pallas as pl
from jax.experimental.pallas import tpu as pltpu

F32 = jnp.float32
BF16 = jnp.bfloat16
HIGHEST = lax.Precision.HIGHEST

D = 1024
DEPTH = 2
B_CTX, S_CTX = 32, 256
B_LAT, S_LAT = 4, 2048
PAST = 512
T_CTX = B_CTX * S_CTX
T_LAT = B_LAT * S_LAT
T_ALL = T_CTX + T_LAT
GRID_W = 64
ROWS = S_LAT // GRID_W

MLA_HEADS, MLA_NOPE, MLA_ROPE, MLA_V = 4, 64, 32, 64
MLA_Q_LORA, MLA_KV_LORA = 192, 128
MLA_SCALE = (MLA_NOPE + MLA_ROPE) ** -0.5
NA_HEADS, NA_DH, NA_WIN_H, NA_WIN_W = 4, 64, 8, 16
NA_SCALE = NA_DH ** -0.5
GLA_HEADS, GLA_DK, GLA_DV, GLA_RANK, GLA_NORM, GLA_CHUNK = 4, 32, 64, 16, 16.0, 64
GLA_SUB = 16
LRU_W, LRU_BLOCKS, LRU_C = 256, 4, 8.0
N_EXP, TOP_K, D_EXP = 32, 4, 1024
SWIGLU_ALPHA, SWIGLU_LIMIT = 1.702, 7.0
ROPE_BASE = 10000.0
EPS = 1e-6
NEG = -1e30

LANES = 128
VMEM_BYTES = 64 * 1024 * 1024

TM = 256
NT_CTX = T_CTX // TM
NT_ALL = T_ALL // TM
TILES_PER_LAT = S_LAT // TM
ID_ROWS = 2 * TM

TM_MOE = 512
N_ROWS = T_ALL * TOP_K
R_MAX = N_ROWS + N_EXP * TM_MOE
NT_MOE = R_MAX // TM_MOE
EXP_CHUNK = 256

C_NAQ, C_NAK, C_NAV, C_GQ, C_GK, C_GV, C_GG, C_LX, C_LG = 0, 256, 512, 768, 896, 1024, 1280, 1536, 1792
BIG = 2048
C_PKV, C_PQ, C_SEGA, C_SEGB = 2048, 2176, 2432, 2560
NC_IN = 2688
SEG_GA = 32


def _cp(vmem_mb, sem=None):
    kw = dict(vmem_limit_bytes=vmem_mb * 1024 * 1024)
    if sem is not None:
        kw["dimension_semantics"] = sem
    return pltpu.CompilerParams(**kw)


def _rms(x):
    return x * lax.rsqrt(jnp.mean(x * x, axis=-1, keepdims=True) + EPS)


def _dot(a, b, **kw):
    return jnp.dot(a, b, preferred_element_type=F32, **kw)


def _dot_nt(a, b):
    return lax.dot_general(a, b, (((1,), (1,)), ((), ())), preferred_element_type=F32)


def _dot_tn(a, b):
    return lax.dot_general(a, b, (((0,), (0,)), ((), ())), preferred_element_type=F32)


def _const_spec(shape):
    n = len(shape)
    return pl.BlockSpec(shape, lambda *_: (0,) * n)


def _mod_body(c_ref, w_ref, b_ref, o_ref):
    c = c_ref[...]
    s = c * jax.nn.sigmoid(c)
    o_ref[0] = _dot(s.astype(BF16), w_ref[0].astype(BF16)) + b_ref[0]


def _modulation(crow, w_mod, b_mod):
    tn = 1536
    return pl.pallas_call(
        _mod_body,
        grid=(DEPTH, 6 * D // tn),
        in_specs=[
            pl.BlockSpec((8, D), lambda l, j: (0, 0)),
            pl.BlockSpec((1, D, tn), lambda l, j: (l, 0, j)),
            pl.BlockSpec((1, 1, tn), lambda l, j: (l, 0, j)),
        ],
        out_specs=pl.BlockSpec((1, 8, tn), lambda l, j: (l, 0, j)),
        out_shape=jax.ShapeDtypeStruct((DEPTH, 8, 6 * D), F32),
        compiler_params=_cp(40),
        name="modulation",
    )(crow, w_mod, b_mod.reshape(DEPTH, 1, 6 * D))


def _mod_row(i):
    return jnp.where(i < NT_CTX, 0, 1 + (i - NT_CTX) // TILES_PER_LAT)


def _pos_block(i):
    return jnp.where(i < NT_CTX, 0, ID_ROWS // TM + (i - NT_CTX) % TILES_PER_LAT)


def _proj_body(first, *refs):
    if first:
        (x_ref, mod_ref, g1_ref, win_ref, qg_ref, wq_ref, kvg_ref, wkv_ref, cq_ref, sq_ref, cs_ref, ss_ref,
         big_ref, seg_ref, ckv_ref, q_ref, k_ref, v_ref) = refs
        x = x_ref[...]
    else:
        (x_ref, moe_ref, modp_ref, mod_ref, g1_ref, win_ref, qg_ref, wq_ref, kvg_ref, wkv_ref, cq_ref, sq_ref,
         cs_ref, ss_ref, xo_ref, big_ref, seg_ref, ckv_ref, q_ref, k_ref, v_ref) = refs
        g2 = modp_ref[0][:, 5 * D:6 * D]
        x = x_ref[...] + g2 * moe_ref[...]
        xo_ref[...] = x
    m = mod_ref[0]
    sh1, sc1 = m[:, 0:D], m[:, D:2 * D]
    h = _rms(x) * g1_ref[...] * (1.0 + sc1) + sh1
    p = _dot(h.astype(BF16), win_ref[...])
    big_ref[...] = p[:, :BIG]
    ckv = _rms(p[:, C_PKV:C_PKV + MLA_KV_LORA]) * kvg_ref[...]
    ckv_ref[...] = ckv
    pq = p[:, C_PQ:C_PQ + 256]
    qn = pq * lax.rsqrt(jnp.sum(pq * pq, axis=-1, keepdims=True) * (1.0 / MLA_Q_LORA) + EPS) * qg_ref[...]
    a = _dot(qn.astype(BF16), wq_ref[...])
    q_ref[...] = (a[:, :512] * cq_ref[...] + a[:, 512:] * sq_ref[...]).astype(BF16)
    seg = p[:, C_SEGA:C_SEGA + LANES] * cs_ref[...] + p[:, C_SEGB:C_SEGB + LANES] * ss_ref[...]
    seg_ref[...] = seg
    kv = _dot(jnp.concatenate([ckv, seg], axis=-1).astype(BF16), wkv_ref[...])
    k_ref[...] = kv[:, :512].astype(BF16)
    v_ref[...] = kv[:, 512:].astype(BF16)


def _proj(first, x, moe, mods_prev, mods_l, lw, tabs):
    tok = lambda w: pl.BlockSpec((TM, w), lambda i: (i, 0))
    modspec = pl.BlockSpec((1, 1, 6 * D), lambda i: (_mod_row(i), 0, 0))
    pos = lambda w: pl.BlockSpec((TM, w), lambda i: (_pos_block(i), 0))
    in_specs = [tok(D)]
    args = [x]
    if not first:
        in_specs += [tok(D), modspec]
        args += [moe, mods_prev]
    in_specs += [modspec, _const_spec((1, D)), _const_spec((D, NC_IN)), _const_spec((1, 256)),
                 _const_spec((256, 1024)), _const_spec((1, LANES)), _const_spec((256, 768)),
                 pos(512), pos(512), pos(LANES), pos(LANES)]
    args += [mods_l, lw["norm1_g"], lw["w_in"], lw["qg"], lw["wq"], lw["kvg"], lw["wkv"],
             tabs["cq"], tabs["sq"], tabs["cs"], tabs["ss"]]
    out_specs = [tok(BIG), tok(LANES), tok(LANES), tok(512), tok(512), tok(256)]
    out_shape = [jax.ShapeDtypeStruct((T_ALL, BIG), F32), jax.ShapeDtypeStruct((T_ALL, LANES), F32),
                 jax.ShapeDtypeStruct((T_ALL, LANES), F32), jax.ShapeDtypeStruct((T_ALL, 512), BF16),
                 jax.ShapeDtypeStruct((T_ALL, 512), BF16), jax.ShapeDtypeStruct((T_ALL, 256), BF16)]
    if not first:
        out_specs = [tok(D)] + out_specs
        out_shape = [jax.ShapeDtypeStruct((T_ALL, D), F32)] + out_shape
    outs = pl.pallas_call(
        functools.partial(_proj_body, first),
        grid=(NT_ALL,),
        in_specs=in_specs,
        out_specs=out_specs,
        out_shape=out_shape,
        compiler_params=_cp(48),
        name="proj",
    )(*args)
    if first:
        return (x,) + tuple(outs)
    return tuple(outs)


def _kvc_body(ckv_ref, kr_ref, wkv_ref, k_ref, v_ref):
    kv = _dot(jnp.concatenate([ckv_ref[...], kr_ref[...]], axis=-1).astype(BF16), wkv_ref[...])
    k_ref[...] = kv[:, :512].astype(BF16)
    v_ref[...] = kv[:, 512:].astype(BF16)


def _cache_kv(ckv_c, kr_c, wkv):
    n = B_LAT * PAST
    return pl.pallas_call(
        _kvc_body,
        grid=(B_LAT,),
        in_specs=[pl.BlockSpec((PAST, LANES), lambda b: (b, 0)), pl.BlockSpec((PAST, LANES), lambda b: (b, 0)),
                  _const_spec((256, 768))],
        out_specs=[pl.BlockSpec((PAST, 512), lambda b: (b, 0)), pl.BlockSpec((PAST, 256), lambda b: (b, 0))],
        out_shape=[jax.ShapeDtypeStruct((n, 512), BF16), jax.ShapeDtypeStruct((n, 256), BF16)],
        name="cache_kv",
    )(ckv_c, kr_c, wkv)


def _softmax_pv(scores, values):
    m = scores[0].max(axis=-1, keepdims=True)
    for s in scores[1:]:
        m = jnp.maximum(m, s.max(axis=-1, keepdims=True))
    den = 0.0
    acc = 0.0
    for s, v in zip(scores, values):
        e = jnp.exp(s - m)
        den = den + e.sum(axis=-1, keepdims=True)
        acc = acc + _dot(e.astype(BF16), v)
    return acc / den


def _lane_lo(shape):
    return lax.broadcasted_iota(jnp.int32, shape, len(shape) - 1) < 64


def _ctx_attn_body(q_ref, k_ref, v_ref, na_ref, oa_ref, ob_ref):
    lo = _lane_lo((S_CTX, LANES))
    for pp in range(2):
        vp = v_ref[:, pp * LANES:(pp + 1) * LANES]
        outs = []
        for h in (2 * pp, 2 * pp + 1):
            s = _dot_nt(q_ref[:, h * LANES:(h + 1) * LANES], k_ref[:, h * LANES:(h + 1) * LANES]) * MLA_SCALE
            outs.append(_softmax_pv([s], [vp]))
        oa_ref[:, pp * LANES:(pp + 1) * LANES] = jnp.where(lo, outs[0], outs[1])
    for pp in range(2):
        qp = na_ref[:, C_NAQ + pp * LANES:C_NAQ + (pp + 1) * LANES]
        kp = na_ref[:, C_NAK + pp * LANES:C_NAK + (pp + 1) * LANES].astype(BF16)
        vp = na_ref[:, C_NAV + pp * LANES:C_NAV + (pp + 1) * LANES].astype(BF16)
        outs = []
        for sel in (lo, ~lo):
            qh = jnp.where(sel, qp, 0.0).astype(BF16)
            outs.append(_softmax_pv([_dot_nt(qh, kp) * NA_SCALE], [vp]))
        ob_ref[:, pp * LANES:(pp + 1) * LANES] = jnp.where(lo, outs[0], outs[1])


def _ctx_attn(q, k, v, big):
    blk = lambda w: pl.BlockSpec((S_CTX, w), lambda b: (b, 0))
    return pl.pallas_call(
        _ctx_attn_body,
        grid=(B_CTX,),
        in_specs=[blk(512), blk(512), blk(256), blk(768)],
        out_specs=[blk(256), blk(256)],
        out_shape=[jax.ShapeDtypeStruct((T_ALL, 256), F32), jax.ShapeDtypeStruct((T_ALL, 256), F32)],
        name="ctx_attn",
    )(q, k, v, big)


TQ = 256


def _lat_mla_body(q_ref, kl_ref, vl_ref, kc_ref, vc_ref, oprev_ref, o_ref):
    del oprev_ref
    lo = _lane_lo((TQ, LANES))
    for pp in range(2):
        vlp = vl_ref[:, pp * LANES:(pp + 1) * LANES]
        vcp = vc_ref[:, pp * LANES:(pp + 1) * LANES]
        outs = []
        for h in (2 * pp, 2 * pp + 1):
            qh = q_ref[:, h * LANES:(h + 1) * LANES]
            s_l = _dot_nt(qh, kl_ref[:, h * LANES:(h + 1) * LANES]) * MLA_SCALE
            s_c = _dot_nt(qh, kc_ref[:, h * LANES:(h + 1) * LANES]) * MLA_SCALE
            outs.append(_softmax_pv([s_l, s_c], [vlp, vcp]))
        o_ref[:, pp * LANES:(pp + 1) * LANES] = jnp.where(lo, outs[0], outs[1])


def _lat_mla(q, k, v, kc, vc, o_a):
    nq = S_LAT // TQ
    qrow = lambda b, t: (T_CTX // TQ + b * nq + t, 0)
    lat = lambda b, t: (T_CTX // S_LAT + b, 0)
    return pl.pallas_call(
        _lat_mla_body,
        grid=(B_LAT, nq),
        in_specs=[pl.BlockSpec((TQ, 512), qrow), pl.BlockSpec((S_LAT, 512), lat), pl.BlockSpec((S_LAT, 256), lat),
                  pl.BlockSpec((PAST, 512), lambda b, t: (b, 0)), pl.BlockSpec((PAST, 256), lambda b, t: (b, 0)),
                  pl.BlockSpec(memory_space=pl.ANY)],
        out_specs=pl.BlockSpec((TQ, 256), qrow),
        out_shape=jax.ShapeDtypeStruct((T_ALL, 256), F32),
        input_output_aliases={5: 0},
        compiler_params=_cp(48),
        name="lat_mla",
    )(q, k, v, kc, vc, o_a)


NK_LOC = NA_WIN_H * GRID_W


def _na_row_start(r):
    return jnp.clip(r - NA_WIN_H // 2, 0, ROWS - NA_WIN_H)


def _lat_na_body(q_ref, k_ref, v_ref, kc_ref, vc_ref, bias_ref, oprev_ref, o_ref):
    del oprev_ref
    r = pl.program_id(1)
    k0 = pl.multiple_of(_na_row_start(r) * GRID_W, GRID_W)
    lo = _lane_lo((GRID_W, LANES))
    for pp in range(2):
        cols = slice(pp * LANES, (pp + 1) * LANES)
        qp = q_ref[:, cols]
        kl = k_ref[pl.ds(k0, NK_LOC), cols].astype(BF16)
        vl = v_ref[pl.ds(k0, NK_LOC), cols].astype(BF16)
        kc = kc_ref[:, cols].astype(BF16)
        vc = vc_ref[:, cols].astype(BF16)
        outs = []
        for j, sel in enumerate((lo, ~lo)):
            qh = jnp.where(sel, qp, 0.0).astype(BF16)
            s_loc = _dot_nt(qh, kl) * NA_SCALE + bias_ref[0, 2 * pp + j]
            s_ctx = _dot_nt(qh, kc) * NA_SCALE
            outs.append(_softmax_pv([s_loc, s_ctx], [vl, vc]))
        o_ref[:, cols] = jnp.where(lo, outs[0], outs[1])


def _lat_na(big, kc, vc, bias, o_b):
    qrow = lambda b, r: (T_CTX // GRID_W + b * ROWS + r, 0)
    boff = lambda b, r: (_na_row_start(r) - r + NA_WIN_H - 1, 0, 0, 0)
    lat = lambda c: (lambda b, r: (T_CTX // S_LAT + b, c))
    return pl.pallas_call(
        _lat_na_body,
        grid=(B_LAT, ROWS),
        in_specs=[pl.BlockSpec((GRID_W, 256), qrow), pl.BlockSpec((S_LAT, 256), lat(1)),
                  pl.BlockSpec((S_LAT, 256), lat(2)),
                  pl.BlockSpec((PAST, 256), lambda b, r: (b, 0)), pl.BlockSpec((PAST, 256), lambda b, r: (b, 0)),
                  pl.BlockSpec((1, NA_HEADS, GRID_W, NK_LOC), boff),
                  pl.BlockSpec(memory_space=pl.ANY)],
        out_specs=pl.BlockSpec((GRID_W, 256), qrow),
        out_shape=jax.ShapeDtypeStruct((T_ALL, 256), F32),
        input_output_aliases={6: 0},
        compiler_params=_cp(48),
        name="lat_na",
    )(big, big, big, kc, vc, bias, o_b)


def _log_sigmoid(x):
    return jnp.minimum(x, 0.0) - jnp.log1p(jnp.exp(-jnp.abs(x)))


def _gla_chunk(rev, q, k, v, la, st):
    C, SB = GLA_CHUNK, GLA_SUB
    ri = lax.broadcasted_iota(jnp.int32, (C, C), 0)
    ci = lax.broadcasted_iota(jnp.int32, (C, C), 1)
    tri = ((ci >= ri) if rev else (ci <= ri)).astype(F32)
    cb = _dot(tri, la, precision=HIGHEST)
    cl = cb[0:1] if rev else cb[C - 1:C]
    o = _dot_nt((q * jnp.exp(cb)).astype(BF16), st.astype(BF16))
    kd = (k * jnp.exp(cl - cb)).astype(BF16)
    bd = (lax.broadcasted_iota(jnp.int32, (256, LANES), 0) // GLA_DV
          == lax.broadcasted_iota(jnp.int32, (256, LANES), 1) // GLA_DK)
    st_new = st * jnp.exp(cl) + jnp.where(bd, _dot_tn(v.astype(BF16), kd), 0.0)

    vb = v.astype(BF16)
    lane_h = lax.broadcasted_iota(jnp.int32, (SB, LANES), 1) // GLA_DK
    col_h = lax.broadcasted_iota(jnp.int32, (SB, 256), 1) // GLA_DV
    expand = (lax.broadcasted_iota(jnp.int32, (LANES, 256), 0) // GLA_DK
              == lax.broadcasted_iota(jnp.int32, (LANES, 256), 1) // GLA_DV).astype(BF16)
    row = lax.broadcasted_iota(jnp.int32, (C, LANES), 0)
    pj = lax.broadcasted_iota(jnp.int32, (SB * SB, LANES), 0) // SB
    pi = lax.broadcasted_iota(jnp.int32, (SB * SB, LANES), 0) % SB
    pmask = (pi <= pj) if rev else (pi >= pj)
    nsub = C // SB
    parts = []
    for i in range(nsub):
        sl = slice(i * SB, (i + 1) * SB)
        qi, ki, cbi, vi = q[sl], k[sl], cb[sl], v[sl]
        acc = o[sl]
        has_off = (i < nsub - 1) if rev else (i > 0)
        if has_off:
            if rev:
                ref = cb[(i + 1) * SB:(i + 1) * SB + 1]
                outside = row >= (i + 1) * SB
            else:
                ref = cb[i * SB - 1:i * SB]
                outside = row < i * SB
            a = qi * jnp.exp(cbi - ref)
            bm = jnp.where(outside, k * jnp.exp(jnp.minimum(ref - cb, 0.0)), 0.0).astype(BF16)
            a4 = jnp.concatenate([jnp.where(lane_h == h, a, 0.0) for h in range(GLA_HEADS)], axis=0)
            att = _dot_nt(a4.astype(BF16), bm)
            pv = _dot(att.astype(BF16), vb)
            for h in range(GLA_HEADS):
                acc = acc + jnp.where(col_h == h, pv[h * SB:(h + 1) * SB], 0.0)
        kj = jnp.broadcast_to(ki[:, None, :], (SB, SB, LANES)).reshape(SB * SB, LANES)
        cbj = jnp.broadcast_to(cbi[:, None, :], (SB, SB, LANES)).reshape(SB * SB, LANES)
        qq = jnp.broadcast_to(qi[None, :, :], (SB, SB, LANES)).reshape(SB * SB, LANES)
        cbq = jnp.broadcast_to(cbi[None, :, :], (SB, SB, LANES)).reshape(SB * SB, LANES)
        x = qq * kj * jnp.where(pmask, jnp.exp(jnp.where(pmask, cbq - cbj, 0.0)), 0.0)
        y = _dot(x.astype(BF16), expand).reshape(SB, SB, 256)
        acc = acc + jnp.sum(y * vi[:, None, :], axis=0)
        parts.append(acc)
    return jnp.concatenate(parts, axis=0), st_new


def _gla_body(S, has_s0, *refs):
    if has_s0:
        (gq_ref, gk_ref, gv_ref, gg_ref, seg_ref, wg_ref, bg_ref, gn_ref, s0_ref, oprev_ref,
         o_ref, sfin_ref, la_scr, of_scr, ob_scr, st_scr) = refs
        del oprev_ref
    else:
        (gq_ref, gk_ref, gv_ref, gg_ref, seg_ref, wg_ref, bg_ref, gn_ref,
         o_ref, sfin_ref, la_scr, of_scr, ob_scr, st_scr) = refs
    nc = S // GLA_CHUNK
    seg = seg_ref[...].astype(BF16)
    for d in range(2):
        la_scr[d] = _log_sigmoid(_dot(seg, wg_ref[d]) + bg_ref[d]) * (1.0 / GLA_NORM)
        if has_s0:
            st_scr[d] = s0_ref[0, d]
        else:
            st_scr[d] = jnp.zeros((256, LANES), F32)

    def body(c, carry):
        for d, out_scr in ((0, of_scr), (1, ob_scr)):
            r0 = pl.multiple_of((c if d == 0 else nc - 1 - c) * GLA_CHUNK, GLA_CHUNK)
            rows = pl.ds(r0, GLA_CHUNK)
            q = gq_ref[rows, :] * (GLA_DK ** -0.5)
            o, st = _gla_chunk(d == 1, q, gk_ref[rows, :], gv_ref[rows, :], la_scr[d, rows, :], st_scr[d])
            out_scr[rows, :] = o
            st_scr[d] = st
        return carry

    lax.fori_loop(0, nc, body, 0)
    sfin_ref[0] = st_scr[...]
    o = of_scr[...] + ob_scr[...]
    ones = (lax.broadcasted_iota(jnp.int32, (256, 256), 0) // GLA_DV
            == lax.broadcasted_iota(jnp.int32, (256, 256), 1) // GLA_DV).astype(F32)
    ms = _dot(o * o, ones, precision=HIGHEST) * (1.0 / GLA_DV)
    g = gg_ref[...]
    o_ref[...] = o * lax.rsqrt(ms + EPS) * gn_ref[...] * (g * jax.nn.sigmoid(g))


def _gla(big, seg, lw, s0, o_prev):
    has_s0 = s0 is not None
    S, nb, base = (S_LAT, B_LAT, T_CTX // S_LAT) if has_s0 else (S_CTX, B_CTX, 0)
    blk = lambda w, c: pl.BlockSpec((S, w), lambda b: (base + b, c))
    in_specs = [blk(LANES, C_GQ // LANES), blk(LANES, C_GK // LANES), blk(256, C_GV // 256), blk(256, C_GG // 256),
                blk(LANES, 0), _const_spec((2, LANES, LANES)), _const_spec((2, 1, LANES)), _const_spec((1, 256))]
    args = [big, big, big, big, seg, lw["wg"], lw["bg"], lw["gla_norm_g"]]
    aliases = {}
    if has_s0:
        in_specs += [pl.BlockSpec((1, 2, 256, LANES), lambda b: (b, 0, 0, 0)), pl.BlockSpec(memory_space=pl.ANY)]
        args += [s0, o_prev]
        aliases = {9: 0}
    return pl.pallas_call(
        functools.partial(_gla_body, S, has_s0),
        grid=(nb,),
        in_specs=in_specs,
        out_specs=[blk(256, 0), pl.BlockSpec((1, 2, 256, LANES), lambda b: (b, 0, 0, 0))],
        out_shape=[jax.ShapeDtypeStruct((T_ALL, 256), F32), jax.ShapeDtypeStruct((nb, 2, 256, LANES), F32)],
        scratch_shapes=[pltpu.VMEM((2, S, LANES), F32), pltpu.VMEM((S, 256), F32), pltpu.VMEM((S, 256), F32),
                        pltpu.VMEM((2, 256, LANES), F32)],
        input_output_aliases=aliases,
        compiler_params=_cp(48),
        name="gla_lat" if has_s0 else "gla_ctx",
    )(*args)


def _softplus(x):
    return jnp.maximum(x, 0.0) + jnp.log1p(jnp.exp(-jnp.abs(x)))


def _lru_body(S, has_h0, *refs):
    if has_h0:
        (lx_ref, lg_ref, cw_ref, cb_ref, wg_ref, bgate_ref, lam_ref, h0_ref, oprev_ref,
         o_ref, hfin_ref, a_scr, b_scr, hf_scr, hb_scr) = refs
        del oprev_ref
    else:
        (lx_ref, lg_ref, cw_ref, cb_ref, wg_ref, bgate_ref, lam_ref,
         o_ref, hfin_ref, a_scr, b_scr, hf_scr, hb_scr) = refs
    x = lx_ref[...]
    row = lax.broadcasted_iota(jnp.int32, (S, LRU_W), 0)
    xm1 = jnp.where(row >= 1, pltpu.roll(x, 1, 0), 0.0)
    xp1 = jnp.where(row < S - 1, pltpu.roll(x, S - 1, 0), 0.0)
    xp2 = jnp.where(row < S - 2, pltpu.roll(x, S - 2, 0), 0.0)
    xc = cb_ref[...] + xm1 * cw_ref[0:1] + x * cw_ref[1:2] + xp1 * cw_ref[2:3] + xp2 * cw_ref[3:4]
    gates = _dot(xc.astype(BF16), wg_ref[...]) + bgate_ref[...]
    for d in range(2):
        r = jax.nn.sigmoid(gates[:, d * LRU_W:(d + 1) * LRU_W])
        ig = jax.nn.sigmoid(gates[:, (2 + d) * LRU_W:(3 + d) * LRU_W])
        log_a = -LRU_C * r * _softplus(-lam_ref[d])
        a_scr[d] = jnp.exp(log_a)
        b_scr[d] = jnp.sqrt(1.0 - jnp.exp(2.0 * log_a)) * (ig * xc)
    if has_h0:
        h_init = (h0_ref[0, 0], h0_ref[0, 1])
    else:
        h_init = (jnp.zeros((1, LRU_W), F32), jnp.zeros((1, LRU_W), F32))

    def body(i, carry):
        hf, hb = carry
        r0 = pl.multiple_of(i * 8, 8)
        rb = pl.multiple_of(S - 8 - i * 8, 8)
        af, bf = a_scr[0, pl.ds(r0, 8), :], b_scr[0, pl.ds(r0, 8), :]
        ab, bb = a_scr[1, pl.ds(rb, 8), :], b_scr[1, pl.ds(rb, 8), :]
        fs, bs = [], [None] * 8
        for t in range(8):
            hf = af[t:t + 1] * hf + bf[t:t + 1]
            fs.append(hf)
            u = 7 - t
            hb = ab[u:u + 1] * hb + bb[u:u + 1]
            bs[u] = hb
        hf_scr[pl.ds(r0, 8), :] = jnp.concatenate(fs, axis=0)
        hb_scr[pl.ds(rb, 8), :] = jnp.concatenate(bs, axis=0)
        return hf, hb

    hf, hb = lax.fori_loop(0, S // 8, body, h_init)
    hfin_ref[0, 0] = hf
    hfin_ref[0, 1] = hb
    g = lg_ref[...]
    gelu = g * (0.5 * (1.0 + jnp.tanh(np.sqrt(2.0 / np.pi).astype(np.float32) * (g + 0.044715 * (g * g * g)))))
    o_ref[...] = (hf_scr[...] + hb_scr[...]) * gelu


def _lru(big, lw, h0, o_prev):
    has_h0 = h0 is not None
    S, nb, base = (S_LAT, B_LAT, T_CTX // S_LAT) if has_h0 else (S_CTX, B_CTX, 0)
    blk = lambda c: pl.BlockSpec((S, LRU_W), lambda b: (base + b, c))
    in_specs = [blk(C_LX // LRU_W), blk(C_LG // LRU_W), _const_spec((4, LRU_W)), _const_spec((1, LRU_W)),
                _const_spec((LRU_W, 4 * LRU_W)), _const_spec((1, 4 * LRU_W)), _const_spec((2, 1, LRU_W))]
    args = [big, big, lw["lru_conv_w"], lw["lru_conv_b"], lw["lru_wg"], lw["lru_bg"], lw["lru_lambda"]]
    aliases = {}
    if has_h0:
        in_specs += [pl.BlockSpec((1, 2, 1, LRU_W), lambda b: (b, 0, 0, 0)), pl.BlockSpec(memory_space=pl.ANY)]
        args += [h0, o_prev]
        aliases = {8: 0}
    return pl.pallas_call(
        functools.partial(_lru_body, S, has_h0),
        grid=(nb,),
        in_specs=in_specs,
        out_specs=[pl.BlockSpec((S, LRU_W), lambda b: (base + b, 0)),
                   pl.BlockSpec((1, 2, 1, LRU_W), lambda b: (b, 0, 0, 0))],
        out_shape=[jax.ShapeDtypeStruct((T_ALL, LRU_W), F32), jax.ShapeDtypeStruct((nb, 2, 1, LRU_W), F32)],
        scratch_shapes=[pltpu.VMEM((2, S, LRU_W), F32), pltpu.VMEM((2, S, LRU_W), F32),
                        pltpu.VMEM((S, LRU_W), F32), pltpu.VMEM((S, LRU_W), F32)],
        input_output_aliases=aliases,
        compiler_params=_cp(48),
        name="lru_lat" if has_h0 else "lru_ctx",
    )(*args)


def _merge_body(x_ref, oa_ref, ob_ref, oc_ref, od_ref, mod_ref, wout_ref, g2_ref, rwh_ref, rwl_ref, rb_ref,
                x1_ref, h2_ref, idx_ref, gw_ref):
    mix = 0.0
    for j, o_ref in enumerate((oa_ref, ob_ref, oc_ref, od_ref)):
        mix = mix + _dot(o_ref[...].astype(BF16), wout_ref[j * 256:(j + 1) * 256, :])
    m = mod_ref[0]
    g1, sh2, sc2 = m[:, 2 * D:3 * D], m[:, 3 * D:4 * D], m[:, 4 * D:5 * D]
    x1 = x_ref[...] + g1 * mix
    x1_ref[...] = x1
    h2 = _rms(x1) * g2_ref[...] * (1.0 + sc2) + sh2
    hi = h2.astype(BF16)
    h2_ref[...] = hi
    lo = (h2 - hi.astype(F32)).astype(BF16)
    logits = _dot(hi, rwh_ref[...]) + _dot(hi, rwl_ref[...]) + _dot(lo, rwh_ref[...]) + rb_ref[...]
    lane = lax.broadcasted_iota(jnp.int32, (TM, LANES), 1)
    idx_out = jnp.zeros((TM, LANES), jnp.int32)
    val_out = jnp.zeros((TM, LANES), F32)
    top = None
    den = 0.0
    for kk in range(TOP_K):
        mval = logits.max(axis=-1, keepdims=True)
        midx = jnp.where(logits == mval, lane, LANES).min(axis=-1, keepdims=True)
        if kk == 0:
            top = mval
        e = jnp.exp(mval - top)
        den = den + e
        idx_out = jnp.where(lane == kk, midx, idx_out)
        val_out = jnp.where(lane == kk, e, val_out)
        logits = jnp.where(lane == midx, -jnp.inf, logits)
    idx_ref[...] = idx_out
    gw_ref[...] = val_out / den


def _merge(x, o_a, o_b, o_c, o_d, mods_l, lw):
    tok = lambda w: pl.BlockSpec((TM, w), lambda i: (i, 0))
    return pl.pallas_call(
        _merge_body,
        grid=(NT_ALL,),
        in_specs=[tok(D), tok(256), tok(256), tok(256), tok(256),
                  pl.BlockSpec((1, 1, 6 * D), lambda i: (_mod_row(i), 0, 0)),
                  _const_spec((D, D)), _const_spec((1, D)), _const_spec((D, LANES)), _const_spec((D, LANES)),
                  _const_spec((1, LANES))],
        out_specs=[tok(D), tok(D), tok(LANES), tok(LANES)],
        out_shape=[jax.ShapeDtypeStruct((T_ALL, D), F32), jax.ShapeDtypeStruct((T_ALL, D), BF16),
                   jax.ShapeDtypeStruct((T_ALL, LANES), jnp.int32), jax.ShapeDtypeStruct((T_ALL, LANES), F32)],
        compiler_params=_cp(40),
        name="merge",
    )(x, o_a, o_b, o_c, o_d, mods_l, lw["w_out"], lw["norm2_g"], lw["rw_hi"], lw["rw_lo"], lw["rb"])


def _moe_body(te_ref, tv_ref, x_ref, g_ref, wgu_ref, bgu_ref, wd_ref, bd_ref, o_ref, wgu_bf, wd_bf):
    i = pl.program_id(0)
    e = te_ref[i]
    prev = te_ref[jnp.maximum(i - 1, 0)]

    @pl.when((i == 0) | (e != prev))
    def _():
        wgu_bf[...] = wgu_ref[0].astype(BF16)
        wd_bf[...] = wd_ref[0].astype(BF16)

    @pl.when(tv_ref[i] > 0)
    def _():
        x = x_ref[...]
        acc = jnp.zeros((TM_MOE, D), F32)
        for j in range(D_EXP // EXP_CHUNK):
            c0 = j * EXP_CHUNK
            xg = _dot(x, wgu_bf[:, c0:c0 + EXP_CHUNK]) + bgu_ref[0][:, c0:c0 + EXP_CHUNK]
            xl = _dot(x, wgu_bf[:, D_EXP + c0:D_EXP + c0 + EXP_CHUNK]) + bgu_ref[0][:, D_EXP + c0:D_EXP + c0 + EXP_CHUNK]
            xg = jnp.minimum(xg, SWIGLU_LIMIT)
            xl = jnp.clip(xl, -SWIGLU_LIMIT, SWIGLU_LIMIT)
            act = xg * jax.nn.sigmoid(SWIGLU_ALPHA * xg) * (xl + 1.0)
            acc = acc + _dot(act.astype(BF16), wd_bf[c0:c0 + EXP_CHUNK, :])
        o_ref[...] = g_ref[...] * (acc + bd_ref[0])

    @pl.when(tv_ref[i] == 0)
    def _():
        o_ref[...] = jnp.zeros((TM_MOE, D), F32)


def _moe(tile_exp, tile_valid, x_sorted, row_gate, lw):
    grid_spec = pltpu.PrefetchScalarGridSpec(
        num_scalar_prefetch=2,
        grid=(NT_MOE,),
        in_specs=[
            pl.BlockSpec((TM_MOE, D), lambda i, te, tv: (i, 0)),
            pl.BlockSpec((TM_MOE, 1), lambda i, te, tv: (i, 0)),
            pl.BlockSpec((1, D, 2 * D_EXP), lambda i, te, tv: (te[i], 0, 0)),
            pl.BlockSpec((1, 1, 2 * D_EXP), lambda i, te, tv: (te[i], 0, 0)),
            pl.BlockSpec((1, D_EXP, D), lambda i, te, tv: (te[i], 0, 0)),
            pl.BlockSpec((1, 1, D), lambda i, te, tv: (te[i], 0, 0)),
        ],
        out_specs=pl.BlockSpec((TM_MOE, D), lambda i, te, tv: (i, 0)),
        scratch_shapes=[pltpu.VMEM((D, 2 * D_EXP), BF16), pltpu.VMEM((D_EXP, D), BF16)],
    )
    return pl.pallas_call(
        _moe_body,
        grid_spec=grid_spec,
        out_shape=jax.ShapeDtypeStruct((R_MAX, D), F32),
        compiler_params=_cp(56, ("arbitrary",)),
        name="moe",
    )(tile_exp, tile_valid, x_sorted, row_gate, lw["exp_w_gu"], lw["exp_b_gu"], lw["exp_w_down"], lw["exp_b_down"])


def _route(idx, gw):
    flat = idx.reshape(-1)
    onehot = (flat[:, None] == jnp.arange(N_EXP, dtype=jnp.int32)[None, :]).astype(jnp.int32)
    csum = jnp.cumsum(onehot, axis=0)
    rank = jnp.take_along_axis(csum, flat[:, None], axis=1)[:, 0] - 1
    counts = csum[-1]
    padded = ((counts + TM_MOE - 1) // TM_MOE) * TM_MOE
    ends = jnp.cumsum(padded)
    starts = ends - padded
    pos = starts[flat] + rank
    row_token = jnp.zeros((R_MAX,), jnp.int32).at[pos].set(jnp.arange(N_ROWS, dtype=jnp.int32) // TOP_K)
    row_gate = jnp.zeros((R_MAX,), F32).at[pos].set(gw.reshape(-1))
    tile_start = jnp.arange(NT_MOE, dtype=jnp.int32) * TM_MOE
    tile_exp = jnp.minimum(jnp.sum((ends[None, :] <= tile_start[:, None]).astype(jnp.int32), axis=1), N_EXP - 1)
    tile_valid = (tile_start < ends[-1]).astype(jnp.int32)
    return pos.reshape(T_ALL, TOP_K), row_token, row_gate.reshape(R_MAX, 1), tile_exp, tile_valid


def _final_body(x_ref, moe_ref, mod_ref, g_ref, o_ref):
    g2 = mod_ref[0][:, 5 * D:6 * D]
    o_ref[...] = _rms(x_ref[...] + g2 * moe_ref[...]) * g_ref[...]


def _final(x1, moe, mods_l, g, tile0, ntiles):
    tok = pl.BlockSpec((TM, D), lambda i: (tile0 + i, 0))
    return pl.pallas_call(
        _final_body,
        grid=(ntiles,),
        in_specs=[tok, tok, pl.BlockSpec((1, 1, 6 * D), lambda i: (_mod_row(tile0 + i), 0, 0)), _const_spec((1, D))],
        out_specs=pl.BlockSpec((TM, D), lambda i: (i, 0)),
        out_shape=jax.ShapeDtypeStruct((ntiles * TM, D), F32),
        name="final_norm",
    )(x1, moe, mods_l, g)


def _rope_perm(w):
    parts = []
    for a in range(2):
        x1, x2 = w[..., a * 16:a * 16 + 8], w[..., a * 16 + 8:a * 16 + 16]
        parts += [-x2, x1]
    return jnp.concatenate(parts, axis=-1)


def _rope_tables():
    t = np.arange(S_LAT)
    freqs = ROPE_BASE ** (-np.arange(8, dtype=np.float64) / 8)
    cs, sn = [], []
    for pos in (t // GRID_W, t % GRID_W):
        ang = pos.astype(np.float64)[:, None] * freqs[None, :]
        cs += [np.cos(ang), np.cos(ang)]
        sn += [np.sin(ang), np.sin(ang)]
    cos = np.concatenate([np.ones((ID_ROWS, 32), np.float32), np.concatenate(cs, -1).astype(np.float32)], 0)
    sin = np.concatenate([np.zeros((ID_ROWS, 32), np.float32), np.concatenate(sn, -1).astype(np.float32)], 0)
    n = cos.shape[0]
    one, zero = np.ones((n, 1), np.float32), np.zeros((n, 1), np.float32)
    cq_h = np.concatenate([np.repeat(one, 64, 1), cos, np.repeat(zero, 32, 1)], 1)
    sq_h = np.concatenate([np.repeat(zero, 64, 1), sin, np.repeat(zero, 32, 1)], 1)
    cseg = np.concatenate([cos, np.repeat(one, 96, 1)], 1)
    sseg = np.concatenate([sin, np.repeat(zero, 96, 1)], 1)
    return dict(cq=jnp.asarray(np.tile(cq_h, (1, 4))), sq=jnp.asarray(np.tile(sq_h, (1, 4))),
                cs=jnp.asarray(cseg), ss=jnp.asarray(sseg))


def _na_bias_table(rpb):
    qc = np.arange(GRID_W)[:, None]
    kc = np.arange(GRID_W)[None, :]
    cs = np.clip(qc - NA_WIN_W // 2, 0, GRID_W - NA_WIN_W)
    valid = (kc >= cs) & (kc < cs + NA_WIN_W)
    dci = np.clip(kc - qc, -(NA_WIN_W - 1), NA_WIN_W - 1) + NA_WIN_W - 1
    off = np.arange(NA_WIN_H)[:, None] + np.arange(NA_WIN_H)[None, :]
    tab = rpb[:, off[:, :, None, None], dci[None, None, :, :]]
    tab = jnp.where(valid[None, None, None], tab, NEG)
    return tab.transpose(1, 0, 3, 2, 4).reshape(NA_WIN_H, NA_HEADS, GRID_W, NK_LOC).astype(F32)


def _layer_weights(l, p):
    w_in = p["w_in"][l]
    z = lambda n: jnp.zeros((D, n), F32)
    kr = w_in[:, 320:352]
    w_perm = jnp.concatenate([
        w_in[:, 352:1120], w_in[:, 1120:1248], w_in[:, 1248:1376], w_in[:, 1376:1632], w_in[:, 1632:1888],
        w_in[:, 1920:2176], w_in[:, 2176:2432], w_in[:, 192:320], w_in[:, 0:192], z(64),
        kr, w_in[:, 1888:1920], z(64), _rope_perm(kr), z(96)], axis=1).astype(BF16)
    wuq = p["mla_w_uq"][l].reshape(MLA_Q_LORA, MLA_HEADS, MLA_NOPE + MLA_ROPE)
    zq = jnp.zeros((MLA_Q_LORA, MLA_HEADS, 32), F32)
    wq1 = jnp.concatenate([wuq, zq], axis=-1).reshape(MLA_Q_LORA, 512)
    wq2 = jnp.concatenate([jnp.zeros((MLA_Q_LORA, MLA_HEADS, MLA_NOPE), F32), _rope_perm(wuq[..., MLA_NOPE:]), zq],
                          axis=-1).reshape(MLA_Q_LORA, 512)
    wq = jnp.pad(jnp.concatenate([wq1, wq2], axis=1), ((0, 256 - MLA_Q_LORA), (0, 0))).astype(BF16)
    qg = jnp.pad(p["mla_q_norm_g"][l], (0, 256 - MLA_Q_LORA)).reshape(1, 256)
    wukv = p["mla_w_ukv"][l].reshape(MLA_KV_LORA, MLA_HEADS, MLA_NOPE + MLA_V)
    wk = jnp.concatenate([wukv[..., :MLA_NOPE], jnp.zeros((MLA_KV_LORA, MLA_HEADS, 64), F32)], -1).reshape(MLA_KV_LORA, 512)
    wv = wukv[..., MLA_NOPE:].reshape(MLA_KV_LORA, 256)
    eye = jnp.eye(MLA_ROPE, dtype=F32)[:, None, :]
    ek = jnp.concatenate([jnp.zeros((MLA_ROPE, MLA_HEADS, MLA_NOPE), F32),
                          jnp.broadcast_to(eye, (MLA_ROPE, MLA_HEADS, MLA_ROPE)),
                          jnp.zeros((MLA_ROPE, MLA_HEADS, 32), F32)], -1).reshape(MLA_ROPE, 512)
    wkv = jnp.concatenate([
        jnp.concatenate([wk, wv], axis=1),
        jnp.concatenate([ek, jnp.zeros((MLA_ROPE, 256), F32)], axis=1),
        jnp.zeros((LANES - MLA_ROPE, 768), F32)], axis=0).astype(BF16)
    wg = jnp.zeros((2, LANES, LANES), F32)
    for d in range(2):
        wg = wg.at[d, SEG_GA + d * GLA_RANK:SEG_GA + (d + 1) * GLA_RANK].set(p["gla_w_gate"][l, d])
    def bdiag(w):
        out = jnp.zeros((LRU_W, LRU_W), F32)
        nb = LRU_W // LRU_BLOCKS
        for n in range(LRU_BLOCKS):
            out = out.at[n * nb:(n + 1) * nb, n * nb:(n + 1) * nb].set(w[n])
        return out
    lru_wg = jnp.concatenate([bdiag(p["lru_w_a"][l, 0]), bdiag(p["lru_w_a"][l, 1]),
                              bdiag(p["lru_w_x"][l, 0]), bdiag(p["lru_w_x"][l, 1])], axis=1).astype(BF16)
    lru_bg = jnp.concatenate([p["lru_b_a"][l, 0], p["lru_b_a"][l, 1], p["lru_b_x"][l, 0], p["lru_b_x"][l, 1]]).reshape(1, 4 * LRU_W)
    rw = jnp.pad(p["router_w"][l], ((0, 0), (0, LANES - N_EXP)))
    rw_hi = rw.astype(BF16)
    rw_lo = (rw - rw_hi.astype(F32)).astype(BF16)
    rb = jnp.concatenate([p["router_b"][l], jnp.full((LANES - N_EXP,), -jnp.inf, F32)]).reshape(1, LANES)
    return dict(
        w_in=w_perm, norm1_g=p["norm1_g"][l].reshape(1, D), norm2_g=p["norm2_g"][l].reshape(1, D),
        qg=qg, wq=wq, kvg=p["mla_kv_norm_g"][l].reshape(1, LANES), wkv=wkv,
        wg=wg.astype(BF16), bg=p["gla_b_gate"][l].reshape(2, 1, LANES), gla_norm_g=p["gla_norm_g"][l].reshape(1, 256),
        lru_conv_w=p["lru_conv_w"][l], lru_conv_b=p["lru_conv_b"][l].reshape(1, LRU_W), lru_wg=lru_wg, lru_bg=lru_bg,
        lru_lambda=p["lru_lambda"][l].reshape(2, 1, LRU_W),
        w_out=p["w_out"][l].astype(BF16), rw_hi=rw_hi, rw_lo=rw_lo, rb=rb,
        exp_w_gu=p["exp_w_gu"][l], exp_b_gu=p["exp_b_gu"][l].reshape(N_EXP, 1, 2 * D_EXP),
        exp_w_down=p["exp_w_down"][l], exp_b_down=p["exp_b_down"][l].reshape(N_EXP, 1, D),
        na_bias=_na_bias_table(p["na_rpb"][l]),
    )


def _state_to_blockdiag(s):
    eye = jnp.eye(GLA_HEADS, dtype=s.dtype)
    return jnp.einsum("bdhkv,hg->bdhvgk", s, eye).reshape(s.shape[0], 2, 256, LANES)


def _blockdiag_to_state(st):
    b = st.shape[0]
    s6 = st.reshape(b, 2, GLA_HEADS, GLA_DV, GLA_HEADS, GLA_DK)
    diag = jnp.stack([s6[:, :, h, :, h, :] for h in range(GLA_HEADS)], axis=2)
    return diag.transpose(0, 1, 2, 4, 3)


def kernel(x_prompt, x_sample, cache_mla_ckv, cache_mla_krope, cache_na_k, cache_na_v, state_gla, state_lru, c, c_ctx, norm1_g, norm2_g, w_mod, b_mod, w_in, w_out, mla_q_norm_g, mla_w_uq, mla_kv_norm_g, mla_w_ukv, na_rpb, gla_w_gate, gla_b_gate, gla_norm_g, lru_conv_w, lru_conv_b, lru_w_a, lru_b_a, lru_w_x, lru_b_x, lru_lambda, router_w, router_b, exp_w_gu, exp_b_gu, exp_w_down, exp_b_down, final_norm_g):
    p = dict(norm1_g=norm1_g, norm2_g=norm2_g, w_in=w_in, w_out=w_out, mla_q_norm_g=mla_q_norm_g, mla_w_uq=mla_w_uq,
             mla_kv_norm_g=mla_kv_norm_g, mla_w_ukv=mla_w_ukv, na_rpb=na_rpb, gla_w_gate=gla_w_gate,
             gla_b_gate=gla_b_gate, gla_norm_g=gla_norm_g, lru_conv_w=lru_conv_w, lru_conv_b=lru_conv_b,
             lru_w_a=lru_w_a, lru_b_a=lru_b_a, lru_w_x=lru_w_x, lru_b_x=lru_b_x, lru_lambda=lru_lambda,
             router_w=router_w, router_b=router_b, exp_w_gu=exp_w_gu, exp_b_gu=exp_b_gu, exp_w_down=exp_w_down,
             exp_b_down=exp_b_down)
    x = jnp.concatenate([x_prompt.reshape(T_CTX, D), x_sample.reshape(T_LAT, D)], axis=0)
    crow = jnp.concatenate([c_ctx[None, :], c, jnp.zeros((8 - 1 - B_LAT, D), F32)], axis=0)
    mods = _modulation(crow, w_mod, b_mod).reshape(DEPTH, 8, 1, 6 * D)
    tabs = _rope_tables()

    col = [[] for _ in range(6)]
    moe = None
    x1 = None
    for l in range(DEPTH):
        lw = _layer_weights(l, p)
        if l == 0:
            x, big, seg, ckv, q, k, v = _proj(True, x, None, None, mods[l], lw, tabs)
        else:
            x, big, seg, ckv, q, k, v = _proj(False, x1, moe, mods[l - 1], mods[l], lw, tabs)
        kr_c = jnp.pad(cache_mla_krope[:, l].reshape(B_LAT * PAST, MLA_ROPE), ((0, 0), (0, LANES - MLA_ROPE)))
        kc, vc = _cache_kv(cache_mla_ckv[:, l].reshape(B_LAT * PAST, MLA_KV_LORA), kr_c, lw["wkv"])
        o_a, o_b = _ctx_attn(q, k, v, big)
        o_a = _lat_mla(q, k, v, kc, vc, o_a)
        o_b = _lat_na(big, cache_na_k[:, l].reshape(B_LAT * PAST, 256), cache_na_v[:, l].reshape(B_LAT * PAST, 256),
                      lw["na_bias"], o_b)
        o_c, sg = _gla(big, seg, lw, None, None)
        o_c, _ = _gla(big, seg, lw, _state_to_blockdiag(state_gla[:, l]), o_c)
        o_d, sl = _lru(big, lw, None, None)
        o_d, _ = _lru(big, lw, state_lru[:, l].reshape(B_LAT, 2, 1, LRU_W), o_d)
        x1, h2, idx, gw = _merge(x, o_a, o_b, o_c, o_d, mods[l], lw)
        pos, row_token, row_gate, tile_exp, tile_valid = _route(idx[:, :TOP_K], gw[:, :TOP_K])
        y = _moe(tile_exp, tile_valid, jnp.take(h2, row_token, axis=0), row_gate, lw)
        moe = jnp.take(y, pos.reshape(-1), axis=0).reshape(T_ALL, TOP_K, D).sum(axis=1)
        col[0].append(ckv[:T_CTX].reshape(B_CTX, S_CTX, MLA_KV_LORA))
        col[1].append(seg[:T_CTX, :MLA_ROPE].reshape(B_CTX, S_CTX, MLA_ROPE))
        col[2].append(big[:T_CTX, C_NAK:C_NAK + 256].reshape(B_CTX, S_CTX, NA_HEADS, NA_DH))
        col[3].append(big[:T_CTX, C_NAV:C_NAV + 256].reshape(B_CTX, S_CTX, NA_HEADS, NA_DH))
        col[4].append(_blockdiag_to_state(sg))
        col[5].append(sl.reshape(B_CTX, 2, LRU_W))
    fg = final_norm_g.reshape(1, D)
    y_prompt = _final(x1, moe, mods[DEPTH - 1], fg, 0, NT_CTX).reshape(B_CTX, S_CTX, D)
    y_sample = _final(x1, moe, mods[DEPTH - 1], fg, NT_CTX, NT_ALL - NT_CTX).reshape(B_LAT, S_LAT, D)
    return (y_prompt, y_sample) + tuple(jnp.stack(cl, axis=1) for cl in col)
```

```python
import functools

import numpy as np
import jax
import jax.numpy as jnp
from jax import lax
from jax.experimental import pallas as pl
from jax.experimental.pallas import tpu as pltpu

F32 = jnp.float32
BF16 = jnp.bfloat16
HIGHEST = lax.Precision.HIGHEST

D = 1024
DEPTH = 2
B_CTX, S_CTX = 32, 256
B_LAT, S_LAT = 4, 2048
PAST = 512
T_CTX = B_CTX * S_CTX
T_LAT = B_LAT * S_LAT
T_ALL = T_CTX + T_LAT
GRID_W = 64
ROWS = S_LAT // GRID_W

MLA_HEADS, MLA_NOPE, MLA_ROPE, MLA_V = 4, 64, 32, 64
MLA_Q_LORA, MLA_KV_LORA = 192, 128
MLA_SCALE = (MLA_NOPE + MLA_ROPE) ** -0.5
NA_HEADS, NA_DH, NA_WIN_H, NA_WIN_W = 4, 64, 8, 16
NA_SCALE = NA_DH ** -0.5
GLA_HEADS, GLA_DK, GLA_DV, GLA_RANK, GLA_NORM, GLA_CHUNK = 4, 32, 64, 16, 16.0, 64
GLA_SUB = 16
LRU_W, LRU_BLOCKS, LRU_C = 256, 4, 8.0
N_EXP, TOP_K, D_EXP = 32, 4, 1024
SWIGLU_ALPHA, SWIGLU_LIMIT = 1.702, 7.0
ROPE_BASE = 10000.0
EPS = 1e-6
NEG = -1e30

LANES = 128
VMEM_BYTES = 64 * 1024 * 1024

TM = 256
NT_CTX = T_CTX // TM
NT_ALL = T_ALL // TM
TILES_PER_LAT = S_LAT // TM
ID_ROWS = 2 * TM

TM_MOE = 512
N_ROWS = T_ALL * TOP_K
R_MAX = N_ROWS + N_EXP * TM_MOE
NT_MOE = R_MAX // TM_MOE
EXP_CHUNK = 256
N_DUMP = 2 * TM_MOE
SUB = D // LANES

C_NAQ, C_NAK, C_NAV, C_GQ, C_GK, C_GV, C_GG, C_LX, C_LG = 0, 256, 512, 768, 896, 1024, 1280, 1536, 1792
BIG = 2048
C_PKV, C_PQ, C_SEGA, C_SEGB = 2048, 2176, 2432, 2560
NC_IN = 2688
SEG_GA = 32


def _cp(vmem_mb, sem=None):
    kw = dict(vmem_limit_bytes=vmem_mb * 1024 * 1024)
    if sem is not None:
        kw["dimension_semantics"] = sem
    return pltpu.CompilerParams(**kw)


def _rms(x):
    return x * lax.rsqrt(jnp.mean(x * x, axis=-1, keepdims=True) + EPS)


def _dot(a, b, **kw):
    return jnp.dot(a, b, preferred_element_type=F32, **kw)


def _dot_nt(a, b):
    return lax.dot_general(a, b, (((1,), (1,)), ((), ())), preferred_element_type=F32)


def _dot_tn(a, b):
    return lax.dot_general(a, b, (((0,), (0,)), ((), ())), preferred_element_type=F32)


def _const_spec(shape):
    n = len(shape)
    return pl.BlockSpec(shape, lambda *_: (0,) * n)


def _mod_body(c_ref, w_ref, b_ref, o_ref):
    c = c_ref[...]
    s = c * jax.nn.sigmoid(c)
    o_ref[0] = _dot(s.astype(BF16), w_ref[0].astype(BF16)) + b_ref[0]


def _modulation(crow, w_mod, b_mod):
    tn = 1536
    return pl.pallas_call(
        _mod_body,
        grid=(DEPTH, 6 * D // tn),
        in_specs=[
            pl.BlockSpec((8, D), lambda l, j: (0, 0)),
            pl.BlockSpec((1, D, tn), lambda l, j: (l, 0, j)),
            pl.BlockSpec((1, 1, tn), lambda l, j: (l, 0, j)),
        ],
        out_specs=pl.BlockSpec((1, 8, tn), lambda l, j: (l, 0, j)),
        out_shape=jax.ShapeDtypeStruct((DEPTH, 8, 6 * D), F32),
        compiler_params=_cp(40),
        name="modulation",
    )(crow, w_mod, b_mod.reshape(DEPTH, 1, 6 * D))


def _mod_row(i):
    return jnp.where(i < NT_CTX, 0, 1 + (i - NT_CTX) // TILES_PER_LAT)


def _pos_block(i):
    return jnp.where(i < NT_CTX, 0, ID_ROWS // TM + (i - NT_CTX) % TILES_PER_LAT)


def _sum_slots(y_ref):
    parts = []
    for s in range(SUB):
        acc = y_ref[:, s, :]
        for kk in range(1, TOP_K):
            acc = acc + y_ref[:, kk * SUB + s, :]
        parts.append(acc)
    return jnp.concatenate(parts, axis=-1)


def _proj_body(first, *refs):
    if first:
        (x_ref, mod_ref, g1_ref, win_ref, qg_ref, wq_ref, kvg_ref, wkv_ref, cq_ref, sq_ref, cs_ref, ss_ref,
         big_ref, seg_ref, ckv_ref, q_ref, k_ref, v_ref) = refs
        x = x_ref[...]
    else:
        (x_ref, moe_ref, modp_ref, mod_ref, g1_ref, win_ref, qg_ref, wq_ref, kvg_ref, wkv_ref, cq_ref, sq_ref,
         cs_ref, ss_ref, xo_ref, big_ref, seg_ref, ckv_ref, q_ref, k_ref, v_ref) = refs
        g2 = modp_ref[0][:, 5 * D:6 * D]
        x = x_ref[...] + g2 * _sum_slots(moe_ref)
        xo_ref[...] = x
    m = mod_ref[0]
    sh1, sc1 = m[:, 0:D], m[:, D:2 * D]
    h = _rms(x) * g1_ref[...] * (1.0 + sc1) + sh1
    p = _dot(h.astype(BF16), win_ref[...])
    big_ref[...] = p[:, :BIG]
    ckv = _rms(p[:, C_PKV:C_PKV + MLA_KV_LORA]) * kvg_ref[...]
    ckv_ref[...] = ckv
    pq = p[:, C_PQ:C_PQ + 256]
    qn = pq * lax.rsqrt(jnp.sum(pq * pq, axis=-1, keepdims=True) * (1.0 / MLA_Q_LORA) + EPS) * qg_ref[...]
    a = _dot(qn.astype(BF16), wq_ref[...])
    q_ref[...] = (a[:, :512] * cq_ref[...] + a[:, 512:] * sq_ref[...]).astype(BF16)
    seg = p[:, C_SEGA:C_SEGA + LANES] * cs_ref[...] + p[:, C_SEGB:C_SEGB + LANES] * ss_ref[...]
    seg_ref[...] = seg
    kv = _dot(jnp.concatenate([ckv, seg], axis=-1).astype(BF16), wkv_ref[...])
    k_ref[...] = kv[:, :512].astype(BF16)
    v_ref[...] = kv[:, 512:].astype(BF16)


def _proj(first, x, moe, mods_prev, mods_l, lw, tabs):
    tok = lambda w: pl.BlockSpec((TM, w), lambda i: (i, 0))
    modspec = pl.BlockSpec((1, 1, 6 * D), lambda i: (_mod_row(i), 0, 0))
    pos = lambda w: pl.BlockSpec((TM, w), lambda i: (_pos_block(i), 0))
    in_specs = [tok(D)]
    args = [x]
    if not first:
        in_specs += [pl.BlockSpec((TM, TOP_K * SUB, LANES), lambda i: (i, 0, 0)), modspec]
        args += [moe, mods_prev]
    in_specs += [modspec, _const_spec((1, D)), _const_spec((D, NC_IN)), _const_spec((1, 256)),
                 _const_spec((256, 1024)), _const_spec((1, LANES)), _const_spec((256, 768)),
                 pos(512), pos(512), pos(LANES), pos(LANES)]
    args += [mods_l, lw["norm1_g"], lw["w_in"], lw["qg"], lw["wq"], lw["kvg"], lw["wkv"],
             tabs["cq"], tabs["sq"], tabs["cs"], tabs["ss"]]
    out_specs = [tok(BIG), tok(LANES), tok(LANES), tok(512), tok(512), tok(256)]
    out_shape = [jax.ShapeDtypeStruct((T_ALL, BIG), F32), jax.ShapeDtypeStruct((T_ALL, LANES), F32),
                 jax.ShapeDtypeStruct((T_ALL, LANES), F32), jax.ShapeDtypeStruct((T_ALL, 512), BF16),
                 jax.ShapeDtypeStruct((T_ALL, 512), BF16), jax.ShapeDtypeStruct((T_ALL, 256), BF16)]
    if not first:
        out_specs = [tok(D)] + out_specs
        out_shape = [jax.ShapeDtypeStruct((T_ALL, D), F32)] + out_shape
    outs = pl.pallas_call(
        functools.partial(_proj_body, first),
        grid=(NT_ALL,),
        in_specs=in_specs,
        out_specs=out_specs,
        out_shape=out_shape,
        compiler_params=_cp(48),
        name="proj",
    )(*args)
    if first:
        return (x,) + tuple(outs)
    return tuple(outs)


def _kvc_body(ckv_ref, kr_ref, wkv_ref, k_ref, v_ref):
    kv = _dot(jnp.concatenate([ckv_ref[...], kr_ref[...]], axis=-1).astype(BF16), wkv_ref[...])
    k_ref[...] = kv[:, :512].astype(BF16)
    v_ref[...] = kv[:, 512:].astype(BF16)


def _cache_kv(ckv_c, kr_c, wkv):
    n = B_LAT * PAST
    return pl.pallas_call(
        _kvc_body,
        grid=(B_LAT,),
        in_specs=[pl.BlockSpec((PAST, LANES), lambda b: (b, 0)), pl.BlockSpec((PAST, LANES), lambda b: (b, 0)),
                  _const_spec((256, 768))],
        out_specs=[pl.BlockSpec((PAST, 512), lambda b: (b, 0)), pl.BlockSpec((PAST, 256), lambda b: (b, 0))],
        out_shape=[jax.ShapeDtypeStruct((n, 512), BF16), jax.ShapeDtypeStruct((n, 256), BF16)],
        name="cache_kv",
    )(ckv_c, kr_c, wkv)


def _softmax_pv(scores, values):
    m = scores[0].max(axis=-1, keepdims=True)
    for s in scores[1:]:
        m = jnp.maximum(m, s.max(axis=-1, keepdims=True))
    den = 0.0
    acc = 0.0
    for s, v in zip(scores, values):
        e = jnp.exp(s - m)
        den = den + e.sum(axis=-1, keepdims=True)
        acc = acc + _dot(e.astype(BF16), v)
    return acc / den


def _lane_lo(shape):
    return lax.broadcasted_iota(jnp.int32, shape, len(shape) - 1) < 64


def _ctx_attn_body(q_ref, k_ref, v_ref, na_ref, oa_ref, ob_ref):
    lo = _lane_lo((S_CTX, LANES))
    for pp in range(2):
        vp = v_ref[:, pp * LANES:(pp + 1) * LANES]
        outs = []
        for h in (2 * pp, 2 * pp + 1):
            s = _dot_nt(q_ref[:, h * LANES:(h + 1) * LANES], k_ref[:, h * LANES:(h + 1) * LANES]) * MLA_SCALE
            outs.append(_softmax_pv([s], [vp]))
        oa_ref[:, pp * LANES:(pp + 1) * LANES] = jnp.where(lo, outs[0], outs[1])
    for pp in range(2):
        qp = na_ref[:, C_NAQ + pp * LANES:C_NAQ + (pp + 1) * LANES]
        kp = na_ref[:, C_NAK + pp * LANES:C_NAK + (pp + 1) * LANES].astype(BF16)
        vp = na_ref[:, C_NAV + pp * LANES:C_NAV + (pp + 1) * LANES].astype(BF16)
        outs = []
        for sel in (lo, ~lo):
            qh = jnp.where(sel, qp, 0.0).astype(BF16)
            outs.append(_softmax_pv([_dot_nt(qh, kp) * NA_SCALE], [vp]))
        ob_ref[:, pp * LANES:(pp + 1) * LANES] = jnp.where(lo, outs[0], outs[1])


def _ctx_attn(q, k, v, big):
    blk = lambda w: pl.BlockSpec((S_CTX, w), lambda b: (b, 0))
    return pl.pallas_call(
        _ctx_attn_body,
        grid=(B_CTX,),
        in_specs=[blk(512), blk(512), blk(256), blk(768)],
        out_specs=[blk(256), blk(256)],
        out_shape=[jax.ShapeDtypeStruct((T_CTX, 256), F32), jax.ShapeDtypeStruct((T_CTX, 256), F32)],
        name="ctx_attn",
    )(q, k, v, big)


TQ = 256


def _lat_mla_body(q_ref, kl_ref, vl_ref, kc_ref, vc_ref, o_ref):
    lo = _lane_lo((TQ, LANES))
    for pp in range(2):
        vlp = vl_ref[:, pp * LANES:(pp + 1) * LANES]
        vcp = vc_ref[:, pp * LANES:(pp + 1) * LANES]
        outs = []
        for h in (2 * pp, 2 * pp + 1):
            qh = q_ref[:, h * LANES:(h + 1) * LANES]
            s_l = _dot_nt(qh, kl_ref[:, h * LANES:(h + 1) * LANES]) * MLA_SCALE
            s_c = _dot_nt(qh, kc_ref[:, h * LANES:(h + 1) * LANES]) * MLA_SCALE
            outs.append(_softmax_pv([s_l, s_c], [vlp, vcp]))
        o_ref[:, pp * LANES:(pp + 1) * LANES] = jnp.where(lo, outs[0], outs[1])


def _lat_mla(q, k, v, kc, vc):
    nq = S_LAT // TQ
    qrow = lambda b, t: (T_CTX // TQ + b * nq + t, 0)
    lat = lambda b, t: (T_CTX // S_LAT + b, 0)
    return pl.pallas_call(
        _lat_mla_body,
        grid=(B_LAT, nq),
        in_specs=[pl.BlockSpec((TQ, 512), qrow), pl.BlockSpec((S_LAT, 512), lat), pl.BlockSpec((S_LAT, 256), lat),
                  pl.BlockSpec((PAST, 512), lambda b, t: (b, 0)), pl.BlockSpec((PAST, 256), lambda b, t: (b, 0))],
        out_specs=pl.BlockSpec((TQ, 256), lambda b, t: (b * nq + t, 0)),
        out_shape=jax.ShapeDtypeStruct((T_LAT, 256), F32),
        compiler_params=_cp(48),
        name="lat_mla",
    )(q, k, v, kc, vc)


NK_LOC = NA_WIN_H * GRID_W


def _na_row_start(r):
    return jnp.clip(r - NA_WIN_H // 2, 0, ROWS - NA_WIN_H)


def _lat_na_body(q_ref, k_ref, v_ref, kc_ref, vc_ref, bias_ref, o_ref):
    r = pl.program_id(1)
    k0 = pl.multiple_of(_na_row_start(r) * GRID_W, GRID_W)
    lo = _lane_lo((GRID_W, LANES))
    for pp in range(2):
        cols = slice(pp * LANES, (pp + 1) * LANES)
        qp = q_ref[:, cols]
        kl = k_ref[pl.ds(k0, NK_LOC), cols].astype(BF16)
        vl = v_ref[pl.ds(k0, NK_LOC), cols].astype(BF16)
        kc = kc_ref[:, cols].astype(BF16)
        vc = vc_ref[:, cols].astype(BF16)
        outs = []
        for j, sel in enumerate((lo, ~lo)):
            qh = jnp.where(sel, qp, 0.0).astype(BF16)
            s_loc = _dot_nt(qh, kl) * NA_SCALE + bias_ref[0, 2 * pp + j]
            s_ctx = _dot_nt(qh, kc) * NA_SCALE
            outs.append(_softmax_pv([s_loc, s_ctx], [vl, vc]))
        o_ref[:, cols] = jnp.where(lo, outs[0], outs[1])


def _lat_na(big, kc, vc, bias):
    qrow = lambda b, r: (T_CTX // GRID_W + b * ROWS + r, 0)
    boff = lambda b, r: (_na_row_start(r) - r + NA_WIN_H - 1, 0, 0, 0)
    lat = lambda c: (lambda b, r: (T_CTX // S_LAT + b, c))
    return pl.pallas_call(
        _lat_na_body,
        grid=(B_LAT, ROWS),
        in_specs=[pl.BlockSpec((GRID_W, 256), qrow), pl.BlockSpec((S_LAT, 256), lat(1)),
                  pl.BlockSpec((S_LAT, 256), lat(2)),
                  pl.BlockSpec((PAST, 256), lambda b, r: (b, 0)), pl.BlockSpec((PAST, 256), lambda b, r: (b, 0)),
                  pl.BlockSpec((1, NA_HEADS, GRID_W, NK_LOC), boff)],
        out_specs=pl.BlockSpec((GRID_W, 256), lambda b, r: (b * ROWS + r, 0)),
        out_shape=jax.ShapeDtypeStruct((T_LAT, 256), F32),
        compiler_params=_cp(48),
        name="lat_na",
    )(big, big, big, kc, vc, bias)


def _log_sigmoid(x):
    return jnp.minimum(x, 0.0) - jnp.log1p(jnp.exp(-jnp.abs(x)))


def _gla_chunk(rev, q, k, v, la, st):
    C, SB = GLA_CHUNK, GLA_SUB
    ri = lax.broadcasted_iota(jnp.int32, (C, C), 0)
    ci = lax.broadcasted_iota(jnp.int32, (C, C), 1)
    tri = ((ci >= ri) if rev else (ci <= ri)).astype(F32)
    cb = _dot(tri, la, precision=HIGHEST)
    cl = cb[0:1] if rev else cb[C - 1:C]
    o = _dot_nt((q * jnp.exp(cb)).astype(BF16), st.astype(BF16))
    kd = (k * jnp.exp(cl - cb)).astype(BF16)
    bd = (lax.broadcasted_iota(jnp.int32, (256, LANES), 0) // GLA_DV
          == lax.broadcasted_iota(jnp.int32, (256, LANES), 1) // GLA_DK)
    st_new = st * jnp.exp(cl) + jnp.where(bd, _dot_tn(v.astype(BF16), kd), 0.0)

    vb = v.astype(BF16)
    lane_h = lax.broadcasted_iota(jnp.int32, (SB, LANES), 1) // GLA_DK
    col_h = lax.broadcasted_iota(jnp.int32, (SB, 256), 1) // GLA_DV
    expand = (lax.broadcasted_iota(jnp.int32, (LANES, 256), 0) // GLA_DK
              == lax.broadcasted_iota(jnp.int32, (LANES, 256), 1) // GLA_DV).astype(BF16)
    row = lax.broadcasted_iota(jnp.int32, (C, LANES), 0)
    pj = lax.broadcasted_iota(jnp.int32, (SB * SB, LANES), 0) // SB
    pi = lax.broadcasted_iota(jnp.int32, (SB * SB, LANES), 0) % SB
    pmask = (pi <= pj) if rev else (pi >= pj)
    nsub = C // SB
    parts = []
    for i in range(nsub):
        sl = slice(i * SB, (i + 1) * SB)
        qi, ki, cbi, vi = q[sl], k[sl], cb[sl], v[sl]
        acc = o[sl]
        has_off = (i < nsub - 1) if rev else (i > 0)
        if has_off:
            if rev:
                ref = cb[(i + 1) * SB:(i + 1) * SB + 1]
                outside = row >= (i + 1) * SB
            else:
                ref = cb[i * SB - 1:i * SB]
                outside = row < i * SB
            a = qi * jnp.exp(cbi - ref)
            bm = jnp.where(outside, k * jnp.exp(jnp.minimum(ref - cb, 0.0)), 0.0).astype(BF16)
            a4 = jnp.concatenate([jnp.where(lane_h == h, a, 0.0) for h in range(GLA_HEADS)], axis=0)
            att = _dot_nt(a4.astype(BF16), bm)
            pv = _dot(att.astype(BF16), vb)
            for h in range(GLA_HEADS):
                acc = acc + jnp.where(col_h == h, pv[h * SB:(h + 1) * SB], 0.0)
        kj = jnp.broadcast_to(ki[:, None, :], (SB, SB, LANES)).reshape(SB * SB, LANES)
        cbj = jnp.broadcast_to(cbi[:, None, :], (SB, SB, LANES)).reshape(SB * SB, LANES)
        qq = jnp.broadcast_to(qi[None, :, :], (SB, SB, LANES)).reshape(SB * SB, LANES)
        cbq = jnp.broadcast_to(cbi[None, :, :], (SB, SB, LANES)).reshape(SB * SB, LANES)
        x = qq * kj * jnp.where(pmask, jnp.exp(jnp.where(pmask, cbq - cbj, 0.0)), 0.0)
        y = _dot(x.astype(BF16), expand).reshape(SB, SB, 256)
        acc = acc + jnp.sum(y * vi[:, None, :], axis=0)
        parts.append(acc)
    return jnp.concatenate(parts, axis=0), st_new


def _gla_body(S, has_s0, *refs):
    if has_s0:
        (gq_ref, gk_ref, gv_ref, gg_ref, seg_ref, wg_ref, bg_ref, gn_ref, s0_ref,
         o_ref, sfin_ref, la_scr, of_scr, ob_scr, st_scr) = refs
    else:
        (gq_ref, gk_ref, gv_ref, gg_ref, seg_ref, wg_ref, bg_ref, gn_ref,
         o_ref, sfin_ref, la_scr, of_scr, ob_scr, st_scr) = refs
    nc = S // GLA_CHUNK
    seg = seg_ref[...].astype(BF16)
    for d in range(2):
        la_scr[d] = _log_sigmoid(_dot(seg, wg_ref[d]) + bg_ref[d]) * (1.0 / GLA_NORM)
        if has_s0:
            st_scr[d] = s0_ref[0, d]
        else:
            st_scr[d] = jnp.zeros((256, LANES), F32)

    def body(c, carry):
        for d, out_scr in ((0, of_scr), (1, ob_scr)):
            r0 = pl.multiple_of((c if d == 0 else nc - 1 - c) * GLA_CHUNK, GLA_CHUNK)
            rows = pl.ds(r0, GLA_CHUNK)
            q = gq_ref[rows, :] * (GLA_DK ** -0.5)
            o, st = _gla_chunk(d == 1, q, gk_ref[rows, :], gv_ref[rows, :], la_scr[d, rows, :], st_scr[d])
            out_scr[rows, :] = o
            st_scr[d] = st
        return carry

    lax.fori_loop(0, nc, body, 0)
    sfin_ref[0] = st_scr[...]
    o = of_scr[...] + ob_scr[...]
    ones = (lax.broadcasted_iota(jnp.int32, (256, 256), 0) // GLA_DV
            == lax.broadcasted_iota(jnp.int32, (256, 256), 1) // GLA_DV).astype(F32)
    ms = _dot(o * o, ones, precision=HIGHEST) * (1.0 / GLA_DV)
    g = gg_ref[...]
    o_ref[...] = o * lax.rsqrt(ms + EPS) * gn_ref[...] * (g * jax.nn.sigmoid(g))


def _gla(big, seg, lw, s0):
    has_s0 = s0 is not None
    S, nb, base = (S_LAT, B_LAT, T_CTX // S_LAT) if has_s0 else (S_CTX, B_CTX, 0)
    blk = lambda w, c: pl.BlockSpec((S, w), lambda b: (base + b, c))
    in_specs = [blk(LANES, C_GQ // LANES), blk(LANES, C_GK // LANES), blk(256, C_GV // 256), blk(256, C_GG // 256),
                blk(LANES, 0), _const_spec((2, LANES, LANES)), _const_spec((2, 1, LANES)), _const_spec((1, 256))]
    args = [big, big, big, big, seg, lw["wg"], lw["bg"], lw["gla_norm_g"]]
    if has_s0:
        in_specs += [pl.BlockSpec((1, 2, 256, LANES), lambda b: (b, 0, 0, 0))]
        args += [s0]
    return pl.pallas_call(
        functools.partial(_gla_body, S, has_s0),
        grid=(nb,),
        in_specs=in_specs,
        out_specs=[pl.BlockSpec((S, 256), lambda b: (b, 0)), pl.BlockSpec((1, 2, 256, LANES), lambda b: (b, 0, 0, 0))],
        out_shape=[jax.ShapeDtypeStruct((nb * S, 256), F32), jax.ShapeDtypeStruct((nb, 2, 256, LANES), F32)],
        scratch_shapes=[pltpu.VMEM((2, S, LANES), F32), pltpu.VMEM((S, 256), F32), pltpu.VMEM((S, 256), F32),
                        pltpu.VMEM((2, 256, LANES), F32)],
        compiler_params=_cp(48),
        name="gla_lat" if has_s0 else "gla_ctx",
    )(*args)


def _softplus(x):
    return jnp.maximum(x, 0.0) + jnp.log1p(jnp.exp(-jnp.abs(x)))


def _lru_body(S, has_h0, *refs):
    if has_h0:
        (lx_ref, lg_ref, cw_ref, cb_ref, wg_ref, bgate_ref, lam_ref, h0_ref,
         o_ref, hfin_ref, a_scr, b_scr, hf_scr, hb_scr) = refs
    else:
        (lx_ref, lg_ref, cw_ref, cb_ref, wg_ref, bgate_ref, lam_ref,
         o_ref, hfin_ref, a_scr, b_scr, hf_scr, hb_scr) = refs
    x = lx_ref[...]
    row = lax.broadcasted_iota(jnp.int32, (S, LRU_W), 0)
    xm1 = jnp.where(row >= 1, pltpu.roll(x, 1, 0), 0.0)
    xp1 = jnp.where(row < S - 1, pltpu.roll(x, S - 1, 0), 0.0)
    xp2 = jnp.where(row < S - 2, pltpu.roll(x, S - 2, 0), 0.0)
    xc = cb_ref[...] + xm1 * cw_ref[0:1] + x * cw_ref[1:2] + xp1 * cw_ref[2:3] + xp2 * cw_ref[3:4]
    gates = _dot(xc.astype(BF16), wg_ref[...]) + bgate_ref[...]
    for d in range(2):
        r = jax.nn.sigmoid(gates[:, d * LRU_W:(d + 1) * LRU_W])
        ig = jax.nn.sigmoid(gates[:, (2 + d) * LRU_W:(3 + d) * LRU_W])
        log_a = -LRU_C * r * _softplus(-lam_ref[d])
        a_scr[d] = jnp.exp(log_a)
        b_scr[d] = jnp.sqrt(1.0 - jnp.exp(2.0 * log_a)) * (ig * xc)
    if has_h0:
        h_init = (h0_ref[0, 0], h0_ref[0, 1])
    else:
        h_init = (jnp.zeros((1, LRU_W), F32), jnp.zeros((1, LRU_W), F32))

    def body(i, carry):
        hf, hb = carry
        r0 = pl.multiple_of(i * 8, 8)
        rb = pl.multiple_of(S - 8 - i * 8, 8)
        af, bf = a_scr[0, pl.ds(r0, 8), :], b_scr[0, pl.ds(r0, 8), :]
        ab, bb = a_scr[1, pl.ds(rb, 8), :], b_scr[1, pl.ds(rb, 8), :]
        fs, bs = [], [None] * 8
        for t in range(8):
            hf = af[t:t + 1] * hf + bf[t:t + 1]
            fs.append(hf)
            u = 7 - t
            hb = ab[u:u + 1] * hb + bb[u:u + 1]
            bs[u] = hb
        hf_scr[pl.ds(r0, 8), :] = jnp.concatenate(fs, axis=0)
        hb_scr[pl.ds(rb, 8), :] = jnp.concatenate(bs, axis=0)
        return hf, hb

    hf, hb = lax.fori_loop(0, S // 8, body, h_init)
    hfin_ref[0, 0] = hf
    hfin_ref[0, 1] = hb
    g = lg_ref[...]
    gelu = g * (0.5 * (1.0 + jnp.tanh(np.sqrt(2.0 / np.pi).astype(np.float32) * (g + 0.044715 * (g * g * g)))))
    o_ref[...] = (hf_scr[...] + hb_scr[...]) * gelu


def _lru(big, lw, h0):
    has_h0 = h0 is not None
    S, nb, base = (S_LAT, B_LAT, T_CTX // S_LAT) if has_h0 else (S_CTX, B_CTX, 0)
    blk = lambda c: pl.BlockSpec((S, LRU_W), lambda b: (base + b, c))
    in_specs = [blk(C_LX // LRU_W), blk(C_LG // LRU_W), _const_spec((4, LRU_W)), _const_spec((1, LRU_W)),
                _const_spec((LRU_W, 4 * LRU_W)), _const_spec((1, 4 * LRU_W)), _const_spec((2, 1, LRU_W))]
    args = [big, big, lw["lru_conv_w"], lw["lru_conv_b"], lw["lru_wg"], lw["lru_bg"], lw["lru_lambda"]]
    if has_h0:
        in_specs += [pl.BlockSpec((1, 2, 1, LRU_W), lambda b: (b, 0, 0, 0))]
        args += [h0]
    return pl.pallas_call(
        functools.partial(_lru_body, S, has_h0),
        grid=(nb,),
        in_specs=in_specs,
        out_specs=[pl.BlockSpec((S, LRU_W), lambda b: (b, 0)),
                   pl.BlockSpec((1, 2, 1, LRU_W), lambda b: (b, 0, 0, 0))],
        out_shape=[jax.ShapeDtypeStruct((nb * S, LRU_W), F32), jax.ShapeDtypeStruct((nb, 2, 1, LRU_W), F32)],
        scratch_shapes=[pltpu.VMEM((2, S, LRU_W), F32), pltpu.VMEM((2, S, LRU_W), F32),
                        pltpu.VMEM((S, LRU_W), F32), pltpu.VMEM((S, LRU_W), F32)],
        compiler_params=_cp(48),
        name="lru_lat" if has_h0 else "lru_ctx",
    )(*args)


def _merge_body(x_ref, *refs):
    (mod_ref, wout_ref, g2_ref, rwh_ref, rwl_ref, rb_ref, x1_ref, h2_ref, idx_ref, gw_ref) = refs[8:]
    is_ctx = pl.program_id(0) < NT_CTX
    mix = 0.0
    for j in range(4):
        o = jnp.where(is_ctx, refs[2 * j][...], refs[2 * j + 1][...])
        mix = mix + _dot(o.astype(BF16), wout_ref[j * 256:(j + 1) * 256, :])
    m = mod_ref[0]
    g1, sh2, sc2 = m[:, 2 * D:3 * D], m[:, 3 * D:4 * D], m[:, 4 * D:5 * D]
    x1 = x_ref[...] + g1 * mix
    x1_ref[...] = x1
    h2 = _rms(x1) * g2_ref[...] * (1.0 + sc2) + sh2
    for s in range(SUB):
        h2_ref[:, s, :] = h2[:, s * LANES:(s + 1) * LANES]
    hi = h2.astype(BF16)
    lo = (h2 - hi.astype(F32)).astype(BF16)
    logits = _dot(hi, rwh_ref[...]) + _dot(hi, rwl_ref[...]) + _dot(lo, rwh_ref[...]) + rb_ref[...]
    lane = lax.broadcasted_iota(jnp.int32, (TM, LANES), 1)
    idx_out = jnp.zeros((TM, LANES), jnp.int32)
    val_out = jnp.zeros((TM, LANES), F32)
    top = None
    den = 0.0
    for kk in range(TOP_K):
        mval = logits.max(axis=-1, keepdims=True)
        midx = jnp.where(logits == mval, lane, LANES).min(axis=-1, keepdims=True)
        if kk == 0:
            top = mval
        e = jnp.exp(mval - top)
        den = den + e
        idx_out = jnp.where(lane == kk, midx, idx_out)
        val_out = jnp.where(lane == kk, e, val_out)
        logits = jnp.where(lane == midx, -jnp.inf, logits)
    idx_ref[...] = idx_out
    gw_ref[...] = val_out / den


def _merge(x, mix_parts, mods_l, lw):
    tok = lambda w: pl.BlockSpec((TM, w), lambda i: (i, 0))
    ctx = pl.BlockSpec((TM, 256), lambda i: (jnp.minimum(i, NT_CTX - 1), 0))
    lat = pl.BlockSpec((TM, 256), lambda i: (jnp.maximum(i - NT_CTX, 0), 0))
    return pl.pallas_call(
        _merge_body,
        grid=(NT_ALL,),
        in_specs=[tok(D)] + [ctx, lat] * 4 + [
                  pl.BlockSpec((1, 1, 6 * D), lambda i: (_mod_row(i), 0, 0)),
                  _const_spec((D, D)), _const_spec((1, D)), _const_spec((D, LANES)), _const_spec((D, LANES)),
                  _const_spec((1, LANES))],
        out_specs=[tok(D), pl.BlockSpec((TM, SUB, LANES), lambda i: (i, 0, 0)), tok(LANES), tok(LANES)],
        out_shape=[jax.ShapeDtypeStruct((T_ALL, D), F32), jax.ShapeDtypeStruct((T_ALL, SUB, LANES), F32),
                   jax.ShapeDtypeStruct((T_ALL, LANES), jnp.int32), jax.ShapeDtypeStruct((T_ALL, LANES), F32)],
        compiler_params=_cp(40),
        name="merge",
    )(x, *mix_parts, mods_l, lw["w_out"], lw["norm2_g"], lw["rw_hi"], lw["rw_lo"], lw["rb"])


def _moe_body(te_ref, nv_ref, tok0_ref, tokn_ref, dst_ref, g_ref, h2_hbm, wgu_ref, bgu_ref, wd_ref, bd_ref,
              y_hbm, xbuf, obuf, wgu_bf, wd_bf, gsem, ssem):
    i = pl.program_id(0)
    slot = i % 2
    valid = nv_ref[i] > 0
    nxt = jnp.minimum(i + 1, NT_MOE - 1)
    nxt_valid = (i + 1 < NT_MOE) & (nv_ref[nxt] > 0)

    def gather(tok_ref, sl):
        def body(r, c):
            pltpu.make_async_copy(h2_hbm.at[tok_ref[0, 0, r]], xbuf.at[sl, r], gsem.at[sl]).start()
            return c
        lax.fori_loop(0, TM_MOE, body, 0, unroll=8)

    def scatter_done(sl):
        pltpu.make_async_copy(obuf.at[sl], y_hbm.at[pl.ds(0, TM_MOE)], ssem.at[sl]).wait()

    @pl.when(i == 0)
    def _():
        obuf[1] = jnp.zeros((TM_MOE, SUB, LANES), F32)
        for half in range(N_DUMP // TM_MOE):
            cp = pltpu.make_async_copy(obuf.at[1], y_hbm.at[pl.ds(N_ROWS + half * TM_MOE, TM_MOE)], ssem.at[1])
            cp.start()
            cp.wait()

    @pl.when((i == 0) & valid)
    def _():
        gather(tok0_ref, 0)

    @pl.when(nxt_valid)
    def _():
        gather(tokn_ref, 1 - slot)

    e = te_ref[i]
    prev = te_ref[jnp.maximum(i - 1, 0)]

    @pl.when(valid & ((i == 0) | (e != prev)))
    def _():
        wgu_bf[...] = wgu_ref[0].astype(BF16)
        wd_bf[...] = wd_ref[0].astype(BF16)

    @pl.when(valid)
    def _():
        pltpu.make_async_copy(h2_hbm.at[pl.ds(0, TM_MOE)], xbuf.at[slot], gsem.at[slot]).wait()
        x = jnp.concatenate([xbuf[slot, :, s, :] for s in range(SUB)], axis=-1).astype(BF16)
        acc = jnp.zeros((TM_MOE, D), F32)
        for j in range(D_EXP // EXP_CHUNK):
            c0 = j * EXP_CHUNK
            xg = _dot(x, wgu_bf[:, c0:c0 + EXP_CHUNK]) + bgu_ref[0][:, c0:c0 + EXP_CHUNK]
            xl = _dot(x, wgu_bf[:, D_EXP + c0:D_EXP + c0 + EXP_CHUNK]) + bgu_ref[0][:, D_EXP + c0:D_EXP + c0 + EXP_CHUNK]
            xg = jnp.minimum(xg, SWIGLU_LIMIT)
            xl = jnp.clip(xl, -SWIGLU_LIMIT, SWIGLU_LIMIT)
            act = xg * jax.nn.sigmoid(SWIGLU_ALPHA * xg) * (xl + 1.0)
            acc = acc + _dot(act.astype(BF16), wd_bf[c0:c0 + EXP_CHUNK, :])
        out = g_ref[...] * (acc + bd_ref[0])

        @pl.when(i >= 1)
        def _():
            scatter_done(1 - slot)

        for s in range(SUB):
            obuf[slot, :, s, :] = out[:, s * LANES:(s + 1) * LANES]

        def body(r, c):
            pltpu.make_async_copy(obuf.at[slot, r], y_hbm.at[dst_ref[0, 0, r]], ssem.at[slot]).start()
            return c
        lax.fori_loop(0, TM_MOE, body, 0, unroll=8)

        @pl.when(jnp.logical_not(nxt_valid))
        def _():
            scatter_done(slot)


def _moe(tile_exp, tile_nvalid, row_tok, row_dst, row_gate, h2, lw):
    smem = lambda f: pl.BlockSpec((1, 1, TM_MOE), f, memory_space=pltpu.SMEM)
    nxt = lambda i, te, nv: (jnp.minimum(i + 1, NT_MOE - 1), 0, 0)
    grid_spec = pltpu.PrefetchScalarGridSpec(
        num_scalar_prefetch=2,
        grid=(NT_MOE,),
        in_specs=[
            smem(lambda i, te, nv: (0, 0, 0)), smem(nxt), smem(lambda i, te, nv: (i, 0, 0)),
            pl.BlockSpec((TM_MOE, 1), lambda i, te, nv: (i, 0)),
            pl.BlockSpec(memory_space=pl.ANY),
            pl.BlockSpec((1, D, 2 * D_EXP), lambda i, te, nv: (te[i], 0, 0)),
            pl.BlockSpec((1, 1, 2 * D_EXP), lambda i, te, nv: (te[i], 0, 0)),
            pl.BlockSpec((1, D_EXP, D), lambda i, te, nv: (te[i], 0, 0)),
            pl.BlockSpec((1, 1, D), lambda i, te, nv: (te[i], 0, 0)),
        ],
        out_specs=pl.BlockSpec(memory_space=pl.ANY),
        scratch_shapes=[pltpu.VMEM((2, TM_MOE, SUB, LANES), F32), pltpu.VMEM((2, TM_MOE, SUB, LANES), F32),
                        pltpu.VMEM((D, 2 * D_EXP), BF16), pltpu.VMEM((D_EXP, D), BF16),
                        pltpu.SemaphoreType.DMA((2,)), pltpu.SemaphoreType.DMA((2,))],
    )
    tok3 = row_tok.reshape(NT_MOE, 1, TM_MOE)
    return pl.pallas_call(
        _moe_body,
        grid_spec=grid_spec,
        out_shape=jax.ShapeDtypeStruct((N_ROWS + N_DUMP, SUB, LANES), F32),
        compiler_params=pltpu.CompilerParams(vmem_limit_bytes=56 * 1024 * 1024, dimension_semantics=("arbitrary",),
                                             disable_bounds_checks=True),
        name="moe",
    )(tile_exp, tile_nvalid, tok3, tok3, row_dst.reshape(NT_MOE, 1, TM_MOE), row_gate,
      h2, lw["exp_w_gu"], lw["exp_b_gu"], lw["exp_w_down"], lw["exp_b_down"])


def _route(idx, gw):
    flat = idx.reshape(-1)
    order = jnp.argsort(flat, stable=True).astype(jnp.int32)
    counts = jnp.sum((flat[:, None] == jnp.arange(N_EXP, dtype=jnp.int32)[None, :]).astype(jnp.int32), axis=0)
    padded = ((counts + TM_MOE - 1) // TM_MOE) * TM_MOE
    pad_ends = jnp.cumsum(padded)
    pad_starts = pad_ends - padded
    unp_starts = jnp.cumsum(counts) - counts
    tile_start = jnp.arange(NT_MOE, dtype=jnp.int32) * TM_MOE
    tile_exp = jnp.minimum(jnp.sum((pad_ends[None, :] <= tile_start[:, None]).astype(jnp.int32), axis=1), N_EXP - 1)
    tile_nvalid = jnp.clip(counts[tile_exp] - (tile_start - pad_starts[tile_exp]), 0, TM_MOE)
    tile_nvalid = jnp.where(tile_start < pad_ends[-1], tile_nvalid, 0).astype(jnp.int32)
    r = jnp.arange(R_MAX, dtype=jnp.int32)
    te_r = tile_exp[r // TM_MOE]
    j = r - pad_starts[te_r]
    valid = (j < counts[te_r]) & (r < pad_ends[-1])
    n = order[jnp.clip(unp_starts[te_r] + j, 0, N_ROWS - 1)]
    row_dst = jnp.where(valid, n, N_ROWS + r % N_DUMP)
    row_tok = jnp.where(valid, n // TOP_K, 0)
    row_gate = jnp.where(valid, gw.reshape(-1)[n], 0.0).reshape(R_MAX, 1)
    return tile_exp, tile_nvalid, row_tok, row_dst, row_gate


def _final_body(x_ref, moe_ref, mod_ref, g_ref, o_ref):
    g2 = mod_ref[0][:, 5 * D:6 * D]
    o_ref[...] = _rms(x_ref[...] + g2 * _sum_slots(moe_ref)) * g_ref[...]


def _final(x1, moe, mods_l, g, tile0, ntiles):
    tok = pl.BlockSpec((TM, D), lambda i: (tile0 + i, 0))
    return pl.pallas_call(
        _final_body,
        grid=(ntiles,),
        in_specs=[tok, pl.BlockSpec((TM, TOP_K * SUB, LANES), lambda i: (tile0 + i, 0, 0)),
                  pl.BlockSpec((1, 1, 6 * D), lambda i: (_mod_row(tile0 + i), 0, 0)), _const_spec((1, D))],
        out_specs=pl.BlockSpec((TM, D), lambda i: (i, 0)),
        out_shape=jax.ShapeDtypeStruct((ntiles * TM, D), F32),
        name="final_norm",
    )(x1, moe, mods_l, g)


def _rope_perm(w):
    parts = []
    for a in range(2):
        x1, x2 = w[..., a * 16:a * 16 + 8], w[..., a * 16 + 8:a * 16 + 16]
        parts += [-x2, x1]
    return jnp.concatenate(parts, axis=-1)


def _rope_tables():
    t = np.arange(S_LAT)
    freqs = ROPE_BASE ** (-np.arange(8, dtype=np.float64) / 8)
    cs, sn = [], []
    for pos in (t // GRID_W, t % GRID_W):
        ang = pos.astype(np.float64)[:, None] * freqs[None, :]
        cs += [np.cos(ang), np.cos(ang)]
        sn += [np.sin(ang), np.sin(ang)]
    cos = np.concatenate([np.ones((ID_ROWS, 32), np.float32), np.concatenate(cs, -1).astype(np.float32)], 0)
    sin = np.concatenate([np.zeros((ID_ROWS, 32), np.float32), np.concatenate(sn, -1).astype(np.float32)], 0)
    n = cos.shape[0]
    one, zero = np.ones((n, 1), np.float32), np.zeros((n, 1), np.float32)
    cq_h = np.concatenate([np.repeat(one, 64, 1), cos, np.repeat(zero, 32, 1)], 1)
    sq_h = np.concatenate([np.repeat(zero, 64, 1), sin, np.repeat(zero, 32, 1)], 1)
    cseg = np.concatenate([cos, np.repeat(one, 96, 1)], 1)
    sseg = np.concatenate([sin, np.repeat(zero, 96, 1)], 1)
    return dict(cq=jnp.asarray(np.tile(cq_h, (1, 4))), sq=jnp.asarray(np.tile(sq_h, (1, 4))),
                cs=jnp.asarray(cseg), ss=jnp.asarray(sseg))


def _na_bias_table(rpb):
    qc = np.arange(GRID_W)[:, None]
    kc = np.arange(GRID_W)[None, :]
    cs = np.clip(qc - NA_WIN_W // 2, 0, GRID_W - NA_WIN_W)
    valid = (kc >= cs) & (kc < cs + NA_WIN_W)
    dci = np.clip(kc - qc, -(NA_WIN_W - 1), NA_WIN_W - 1) + NA_WIN_W - 1
    off = np.arange(NA_WIN_H)[:, None] + np.arange(NA_WIN_H)[None, :]
    tab = rpb[:, off[:, :, None, None], dci[None, None, :, :]]
    tab = jnp.where(valid[None, None, None], tab, NEG)
    return tab.transpose(1, 0, 3, 2, 4).reshape(NA_WIN_H, NA_HEADS, GRID_W, NK_LOC).astype(F32)


def _layer_weights(l, p):
    w_in = p["w_in"][l]
    z = lambda n: jnp.zeros((D, n), F32)
    kr = w_in[:, 320:352]
    w_perm = jnp.concatenate([
        w_in[:, 352:1120], w_in[:, 1120:1248], w_in[:, 1248:1376], w_in[:, 1376:1632], w_in[:, 1632:1888],
        w_in[:, 1920:2176], w_in[:, 2176:2432], w_in[:, 192:320], w_in[:, 0:192], z(64),
        kr, w_in[:, 1888:1920], z(64), _rope_perm(kr), z(96)], axis=1).astype(BF16)
    wuq = p["mla_w_uq"][l].reshape(MLA_Q_LORA, MLA_HEADS, MLA_NOPE + MLA_ROPE)
    zq = jnp.zeros((MLA_Q_LORA, MLA_HEADS, 32), F32)
    wq1 = jnp.concatenate([wuq, zq], axis=-1).reshape(MLA_Q_LORA, 512)
    wq2 = jnp.concatenate([jnp.zeros((MLA_Q_LORA, MLA_HEADS, MLA_NOPE), F32), _rope_perm(wuq[..., MLA_NOPE:]), zq],
                          axis=-1).reshape(MLA_Q_LORA, 512)
    wq = jnp.pad(jnp.concatenate([wq1, wq2], axis=1), ((0, 256 - MLA_Q_LORA), (0, 0))).astype(BF16)
    qg = jnp.pad(p["mla_q_norm_g"][l], (0, 256 - MLA_Q_LORA)).reshape(1, 256)
    wukv = p["mla_w_ukv"][l].reshape(MLA_KV_LORA, MLA_HEADS, MLA_NOPE + MLA_V)
    wk = jnp.concatenate([wukv[..., :MLA_NOPE], jnp.zeros((MLA_KV_LORA, MLA_HEADS, 64), F32)], -1).reshape(MLA_KV_LORA, 512)
    wv = wukv[..., MLA_NOPE:].reshape(MLA_KV_LORA, 256)
    eye = jnp.eye(MLA_ROPE, dtype=F32)[:, None, :]
    ek = jnp.concatenate([jnp.zeros((MLA_ROPE, MLA_HEADS, MLA_NOPE), F32),
                          jnp.broadcast_to(eye, (MLA_ROPE, MLA_HEADS, MLA_ROPE)),
                          jnp.zeros((MLA_ROPE, MLA_HEADS, 32), F32)], -1).reshape(MLA_ROPE, 512)
    wkv = jnp.concatenate([
        jnp.concatenate([wk, wv], axis=1),
        jnp.concatenate([ek, jnp.zeros((MLA_ROPE, 256), F32)], axis=1),
        jnp.zeros((LANES - MLA_ROPE, 768), F32)], axis=0).astype(BF16)
    wg = jnp.zeros((2, LANES, LANES), F32)
    for d in range(2):
        wg = wg.at[d, SEG_GA + d * GLA_RANK:SEG_GA + (d + 1) * GLA_RANK].set(p["gla_w_gate"][l, d])
    def bdiag(w):
        out = jnp.zeros((LRU_W, LRU_W), F32)
        nb = LRU_W // LRU_BLOCKS
        for n in range(LRU_BLOCKS):
            out = out.at[n * nb:(n + 1) * nb, n * nb:(n + 1) * nb].set(w[n])
        return out
    lru_wg = jnp.concatenate([bdiag(p["lru_w_a"][l, 0]), bdiag(p["lru_w_a"][l, 1]),
                              bdiag(p["lru_w_x"][l, 0]), bdiag(p["lru_w_x"][l, 1])], axis=1).astype(BF16)
    lru_bg = jnp.concatenate([p["lru_b_a"][l, 0], p["lru_b_a"][l, 1], p["lru_b_x"][l, 0], p["lru_b_x"][l, 1]]).reshape(1, 4 * LRU_W)
    rw = jnp.pad(p["router_w"][l], ((0, 0), (0, LANES - N_EXP)))
    rw_hi = rw.astype(BF16)
    rw_lo = (rw - rw_hi.astype(F32)).astype(BF16)
    rb = jnp.concatenate([p["router_b"][l], jnp.full((LANES - N_EXP,), -jnp.inf, F32)]).reshape(1, LANES)
    return dict(
        w_in=w_perm, norm1_g=p["norm1_g"][l].reshape(1, D), norm2_g=p["norm2_g"][l].reshape(1, D),
        qg=qg, wq=wq, kvg=p["mla_kv_norm_g"][l].reshape(1, LANES), wkv=wkv,
        wg=wg.astype(BF16), bg=p["gla_b_gate"][l].reshape(2, 1, LANES), gla_norm_g=p["gla_norm_g"][l].reshape(1, 256),
        lru_conv_w=p["lru_conv_w"][l], lru_conv_b=p["lru_conv_b"][l].reshape(1, LRU_W), lru_wg=lru_wg, lru_bg=lru_bg,
        lru_lambda=p["lru_lambda"][l].reshape(2, 1, LRU_W),
        w_out=p["w_out"][l].astype(BF16), rw_hi=rw_hi, rw_lo=rw_lo, rb=rb,
        exp_w_gu=p["exp_w_gu"][l], exp_b_gu=p["exp_b_gu"][l].reshape(N_EXP, 1, 2 * D_EXP),
        exp_w_down=p["exp_w_down"][l], exp_b_down=p["exp_b_down"][l].reshape(N_EXP, 1, D),
        na_bias=_na_bias_table(p["na_rpb"][l]),
    )


def _state_to_blockdiag(s):
    eye = jnp.eye(GLA_HEADS, dtype=s.dtype)
    return jnp.einsum("bdhkv,hg->bdhvgk", s, eye).reshape(s.shape[0], 2, 256, LANES)


def _blockdiag_to_state(st):
    b = st.shape[0]
    s6 = st.reshape(b, 2, GLA_HEADS, GLA_DV, GLA_HEADS, GLA_DK)
    diag = jnp.stack([s6[:, :, h, :, h, :] for h in range(GLA_HEADS)], axis=2)
    return diag.transpose(0, 1, 2, 4, 3)


def kernel(x_prompt, x_sample, cache_mla_ckv, cache_mla_krope, cache_na_k, cache_na_v, state_gla, state_lru, c, c_ctx, norm1_g, norm2_g, w_mod, b_mod, w_in, w_out, mla_q_norm_g, mla_w_uq, mla_kv_norm_g, mla_w_ukv, na_rpb, gla_w_gate, gla_b_gate, gla_norm_g, lru_conv_w, lru_conv_b, lru_w_a, lru_b_a, lru_w_x, lru_b_x, lru_lambda, router_w, router_b, exp_w_gu, exp_b_gu, exp_w_down, exp_b_down, final_norm_g):
    p = dict(norm1_g=norm1_g, norm2_g=norm2_g, w_in=w_in, w_out=w_out, mla_q_norm_g=mla_q_norm_g, mla_w_uq=mla_w_uq,
             mla_kv_norm_g=mla_kv_norm_g, mla_w_ukv=mla_w_ukv, na_rpb=na_rpb, gla_w_gate=gla_w_gate,
             gla_b_gate=gla_b_gate, gla_norm_g=gla_norm_g, lru_conv_w=lru_conv_w, lru_conv_b=lru_conv_b,
             lru_w_a=lru_w_a, lru_b_a=lru_b_a, lru_w_x=lru_w_x, lru_b_x=lru_b_x, lru_lambda=lru_lambda,
             router_w=router_w, router_b=router_b, exp_w_gu=exp_w_gu, exp_b_gu=exp_b_gu, exp_w_down=exp_w_down,
             exp_b_down=exp_b_down)
    x = jnp.concatenate([x_prompt.reshape(T_CTX, D), x_sample.reshape(T_LAT, D)], axis=0)
    crow = jnp.concatenate([c_ctx[None, :], c, jnp.zeros((8 - 1 - B_LAT, D), F32)], axis=0)
    mods = _modulation(crow, w_mod, b_mod).reshape(DEPTH, 8, 1, 6 * D)
    tabs = _rope_tables()

    col = [[] for _ in range(6)]
    moe = None
    x1 = None
    for l in range(DEPTH):
        lw = _layer_weights(l, p)
        if l == 0:
            x, big, seg, ckv, q, k, v = _proj(True, x, None, None, mods[l], lw, tabs)
        else:
            x, big, seg, ckv, q, k, v = _proj(False, x1, moe, mods[l - 1], mods[l], lw, tabs)
        kr_c = jnp.pad(cache_mla_krope[:, l].reshape(B_LAT * PAST, MLA_ROPE), ((0, 0), (0, LANES - MLA_ROPE)))
        kc, vc = _cache_kv(cache_mla_ckv[:, l].reshape(B_LAT * PAST, MLA_KV_LORA), kr_c, lw["wkv"])
        oa_c, ob_c = _ctx_attn(q, k, v, big)
        oa_l = _lat_mla(q, k, v, kc, vc)
        ob_l = _lat_na(big, cache_na_k[:, l].reshape(B_LAT * PAST, 256), cache_na_v[:, l].reshape(B_LAT * PAST, 256),
                       lw["na_bias"])
        oc_c, sg = _gla(big, seg, lw, None)
        oc_l, _ = _gla(big, seg, lw, _state_to_blockdiag(state_gla[:, l]))
        od_c, sl = _lru(big, lw, None)
        od_l, _ = _lru(big, lw, state_lru[:, l].reshape(B_LAT, 2, 1, LRU_W))
        x1, h2, idx, gw = _merge(x, (oa_c, oa_l, ob_c, ob_l, oc_c, oc_l, od_c, od_l), mods[l], lw)
        tile_exp, tile_nvalid, row_tok, row_dst, row_gate = _route(idx[:, :TOP_K], gw[:, :TOP_K])
        y = _moe(tile_exp, tile_nvalid, row_tok, row_dst, row_gate, h2, lw)
        moe = y.reshape((N_ROWS + N_DUMP) // TOP_K, TOP_K * SUB, LANES)
        col[0].append(ckv[:T_CTX].reshape(B_CTX, S_CTX, MLA_KV_LORA))
        col[1].append(seg[:T_CTX, :MLA_ROPE].reshape(B_CTX, S_CTX, MLA_ROPE))
        col[2].append(big[:T_CTX, C_NAK:C_NAK + 256].reshape(B_CTX, S_CTX, NA_HEADS, NA_DH))
        col[3].append(big[:T_CTX, C_NAV:C_NAV + 256].reshape(B_CTX, S_CTX, NA_HEADS, NA_DH))
        col[4].append(_blockdiag_to_state(sg))
        col[5].append(sl.reshape(B_CTX, 2, LRU_W))
    fg = final_norm_g.reshape(1, D)
    y_prompt = _final(x1, moe, mods[DEPTH - 1], fg, 0, NT_CTX).reshape(B_CTX, S_CTX, D)
    y_sample = _final(x1, moe, mods[DEPTH - 1], fg, NT_CTX, NT_ALL - NT_CTX).reshape(B_LAT, S_LAT, D)
    return (y_prompt, y_sample) + tuple(jnp.stack(cl, axis=1) for cl in col)
```

```python
import functools

import numpy as np
import jax
import jax.numpy as jnp
from jax import lax
from jax.experimental import pallas as pl
from jax.experimental.pallas import tpu as pltpu

F32 = jnp.float32
BF16 = jnp.bfloat16
HIGHEST = lax.Precision.HIGHEST

D = 1024
DEPTH = 2
B_CTX, S_CTX = 32, 256
B_LAT, S_LAT = 4, 2048
PAST = 512
T_CTX = B_CTX * S_CTX
T_LAT = B_LAT * S_LAT
T_ALL = T_CTX + T_LAT
GRID_W = 64
ROWS = S_LAT // GRID_W

MLA_HEADS, MLA_NOPE, MLA_ROPE, MLA_V = 4, 64, 32, 64
MLA_Q_LORA, MLA_KV_LORA = 192, 128
MLA_SCALE = (MLA_NOPE + MLA_ROPE) ** -0.5
NA_HEADS, NA_DH, NA_WIN_H, NA_WIN_W = 4, 64, 8, 16
NA_SCALE = NA_DH ** -0.5
GLA_HEADS, GLA_DK, GLA_DV, GLA_RANK, GLA_NORM, GLA_CHUNK = 4, 32, 64, 16, 16.0, 64
GLA_SUB = 16
LRU_W, LRU_BLOCKS, LRU_C = 256, 4, 8.0
N_EXP, TOP_K, D_EXP = 32, 4, 1024
SWIGLU_ALPHA, SWIGLU_LIMIT = 1.702, 7.0
ROPE_BASE = 10000.0
EPS = 1e-6
NEG = -1e30

LANES = 128
VMEM_BYTES = 64 * 1024 * 1024

TM = 256
NT_CTX = T_CTX // TM
NT_ALL = T_ALL // TM
TILES_PER_LAT = S_LAT // TM
ID_ROWS = 2 * TM

TM_MOE = 512
N_ROWS = T_ALL * TOP_K
R_MAX = N_ROWS + N_EXP * TM_MOE
NT_MOE = R_MAX // TM_MOE
EXP_CHUNK = 256
N_DUMP = 2 * TM_MOE
SUB = D // LANES
TOP_K_SHIFT = 2
assert 1 << TOP_K_SHIFT == TOP_K

C_NAQ, C_NAK, C_NAV, C_GQ, C_GK, C_GV, C_GG, C_LX, C_LG = 0, 256, 512, 768, 896, 1024, 1280, 1536, 1792
BIG = 2048
C_PKV, C_PQ, C_SEGA, C_SEGB = 2048, 2176, 2432, 2560
NC_IN = 2688
SEG_GA = 32


def _cp(vmem_mb, sem=None):
    kw = dict(vmem_limit_bytes=vmem_mb * 1024 * 1024)
    if sem is not None:
        kw["dimension_semantics"] = sem
    return pltpu.CompilerParams(**kw)


def _rms(x):
    return x * lax.rsqrt(jnp.mean(x * x, axis=-1, keepdims=True) + EPS)


def _dot(a, b, **kw):
    return jnp.dot(a, b, preferred_element_type=F32, **kw)


def _dot_nt(a, b):
    return lax.dot_general(a, b, (((1,), (1,)), ((), ())), preferred_element_type=F32)


def _dot_tn(a, b):
    return lax.dot_general(a, b, (((0,), (0,)), ((), ())), preferred_element_type=F32)


def _const_spec(shape):
    n = len(shape)
    return pl.BlockSpec(shape, lambda *_: (0,) * n)


def _mod_body(c_ref, w_ref, b_ref, o_ref):
    c = c_ref[...]
    s = c * jax.nn.sigmoid(c)
    o_ref[0] = _dot(s.astype(BF16), w_ref[0].astype(BF16)) + b_ref[0]


def _modulation(crow, w_mod, b_mod):
    tn = 1536
    return pl.pallas_call(
        _mod_body,
        grid=(DEPTH, 6 * D // tn),
        in_specs=[
            pl.BlockSpec((8, D), lambda l, j: (0, 0)),
            pl.BlockSpec((1, D, tn), lambda l, j: (l, 0, j)),
            pl.BlockSpec((1, 1, tn), lambda l, j: (l, 0, j)),
        ],
        out_specs=pl.BlockSpec((1, 8, tn), lambda l, j: (l, 0, j)),
        out_shape=jax.ShapeDtypeStruct((DEPTH, 8, 6 * D), F32),
        compiler_params=_cp(40),
        name="modulation",
    )(crow, w_mod, b_mod.reshape(DEPTH, 1, 6 * D))


def _mod_row(i):
    return jnp.where(i < NT_CTX, 0, 1 + (i - NT_CTX) // TILES_PER_LAT)


def _pos_block(i):
    return jnp.where(i < NT_CTX, 0, ID_ROWS // TM + (i - NT_CTX) % TILES_PER_LAT)


def _sum_slots(y_ref, gw_ref):
    gw = gw_ref[...]
    cols = [gw[:, kk:kk + 1] for kk in range(TOP_K)]
    parts = []
    for s in range(SUB):
        acc = cols[0] * y_ref[:, s, :]
        for kk in range(1, TOP_K):
            acc = acc + cols[kk] * y_ref[:, kk * SUB + s, :]
        parts.append(acc)
    return jnp.concatenate(parts, axis=-1)


def _proj_body(first, *refs):
    if first:
        (x_ref, mod_ref, g1_ref, win_ref, qg_ref, wq_ref, kvg_ref, wkv_ref, cq_ref, sq_ref, cs_ref, ss_ref,
         big_ref, seg_ref, ckv_ref, q_ref, k_ref, v_ref) = refs
        x = x_ref[...]
    else:
        (x_ref, moe_ref, gw_ref, modp_ref, mod_ref, g1_ref, win_ref, qg_ref, wq_ref, kvg_ref, wkv_ref, cq_ref, sq_ref,
         cs_ref, ss_ref, xo_ref, big_ref, seg_ref, ckv_ref, q_ref, k_ref, v_ref) = refs
        g2 = modp_ref[0][:, 5 * D:6 * D]
        x = x_ref[...] + g2 * _sum_slots(moe_ref, gw_ref)
        xo_ref[...] = x
    m = mod_ref[0]
    sh1, sc1 = m[:, 0:D], m[:, D:2 * D]
    h = _rms(x) * g1_ref[...] * (1.0 + sc1) + sh1
    p = _dot(h.astype(BF16), win_ref[...])
    big_ref[...] = p[:, :BIG]
    ckv = _rms(p[:, C_PKV:C_PKV + MLA_KV_LORA]) * kvg_ref[...]
    ckv_ref[...] = ckv
    pq = p[:, C_PQ:C_PQ + 256]
    qn = pq * lax.rsqrt(jnp.sum(pq * pq, axis=-1, keepdims=True) * (1.0 / MLA_Q_LORA) + EPS) * qg_ref[...]
    a = _dot(qn.astype(BF16), wq_ref[...])
    q_ref[...] = (a[:, :512] * cq_ref[...] + a[:, 512:] * sq_ref[...]).astype(BF16)
    seg = p[:, C_SEGA:C_SEGA + LANES] * cs_ref[...] + p[:, C_SEGB:C_SEGB + LANES] * ss_ref[...]
    seg_ref[...] = seg
    kv = _dot(jnp.concatenate([ckv, seg], axis=-1).astype(BF16), wkv_ref[...])
    k_ref[...] = kv[:, :512].astype(BF16)
    v_ref[...] = kv[:, 512:].astype(BF16)


def _proj(first, x, moe, gw, mods_prev, mods_l, lw, tabs):
    tok = lambda w: pl.BlockSpec((TM, w), lambda i: (i, 0))
    modspec = pl.BlockSpec((1, 1, 6 * D), lambda i: (_mod_row(i), 0, 0))
    pos = lambda w: pl.BlockSpec((TM, w), lambda i: (_pos_block(i), 0))
    in_specs = [tok(D)]
    args = [x]
    if not first:
        in_specs += [pl.BlockSpec((TM, TOP_K * SUB, LANES), lambda i: (i, 0, 0)), tok(LANES), modspec]
        args += [moe, gw, mods_prev]
    in_specs += [modspec, _const_spec((1, D)), _const_spec((D, NC_IN)), _const_spec((1, 256)),
                 _const_spec((256, 1024)), _const_spec((1, LANES)), _const_spec((256, 768)),
                 pos(512), pos(512), pos(LANES), pos(LANES)]
    args += [mods_l, lw["norm1_g"], lw["w_in"], lw["qg"], lw["wq"], lw["kvg"], lw["wkv"],
             tabs["cq"], tabs["sq"], tabs["cs"], tabs["ss"]]
    out_specs = [tok(BIG), tok(LANES), tok(LANES), tok(512), tok(512), tok(256)]
    out_shape = [jax.ShapeDtypeStruct((T_ALL, BIG), F32), jax.ShapeDtypeStruct((T_ALL, LANES), F32),
                 jax.ShapeDtypeStruct((T_ALL, LANES), F32), jax.ShapeDtypeStruct((T_ALL, 512), BF16),
                 jax.ShapeDtypeStruct((T_ALL, 512), BF16), jax.ShapeDtypeStruct((T_ALL, 256), BF16)]
    if not first:
        out_specs = [tok(D)] + out_specs
        out_shape = [jax.ShapeDtypeStruct((T_ALL, D), F32)] + out_shape
    outs = pl.pallas_call(
        functools.partial(_proj_body, first),
        grid=(NT_ALL,),
        in_specs=in_specs,
        out_specs=out_specs,
        out_shape=out_shape,
        compiler_params=_cp(48),
        name="proj",
    )(*args)
    if first:
        return (x,) + tuple(outs)
    return tuple(outs)


def _kvc_body(ckv_ref, kr_ref, wkv_ref, k_ref, v_ref):
    kv = _dot(jnp.concatenate([ckv_ref[...], kr_ref[...]], axis=-1).astype(BF16), wkv_ref[...])
    k_ref[...] = kv[:, :512].astype(BF16)
    v_ref[...] = kv[:, 512:].astype(BF16)


def _cache_kv(ckv_c, kr_c, wkv):
    n = B_LAT * PAST
    return pl.pallas_call(
        _kvc_body,
        grid=(B_LAT,),
        in_specs=[pl.BlockSpec((PAST, LANES), lambda b: (b, 0)), pl.BlockSpec((PAST, LANES), lambda b: (b, 0)),
                  _const_spec((256, 768))],
        out_specs=[pl.BlockSpec((PAST, 512), lambda b: (b, 0)), pl.BlockSpec((PAST, 256), lambda b: (b, 0))],
        out_shape=[jax.ShapeDtypeStruct((n, 512), BF16), jax.ShapeDtypeStruct((n, 256), BF16)],
        name="cache_kv",
    )(ckv_c, kr_c, wkv)


def _softmax_pv(scores, values):
    m = scores[0].max(axis=-1, keepdims=True)
    for s in scores[1:]:
        m = jnp.maximum(m, s.max(axis=-1, keepdims=True))
    den = 0.0
    acc = 0.0
    for s, v in zip(scores, values):
        e = jnp.exp(s - m)
        den = den + e.sum(axis=-1, keepdims=True)
        acc = acc + _dot(e.astype(BF16), v)
    return acc / den


def _lane_lo(shape):
    return lax.broadcasted_iota(jnp.int32, shape, len(shape) - 1) < 64


def _ctx_attn_body(q_ref, k_ref, v_ref, na_ref, oa_ref, ob_ref):
    lo = _lane_lo((S_CTX, LANES))
    for pp in range(2):
        vp = v_ref[:, pp * LANES:(pp + 1) * LANES]
        outs = []
        for h in (2 * pp, 2 * pp + 1):
            s = _dot_nt(q_ref[:, h * LANES:(h + 1) * LANES], k_ref[:, h * LANES:(h + 1) * LANES]) * MLA_SCALE
            outs.append(_softmax_pv([s], [vp]))
        oa_ref[:, pp * LANES:(pp + 1) * LANES] = jnp.where(lo, outs[0], outs[1])
    for pp in range(2):
        qp = na_ref[:, C_NAQ + pp * LANES:C_NAQ + (pp + 1) * LANES]
        kp = na_ref[:, C_NAK + pp * LANES:C_NAK + (pp + 1) * LANES].astype(BF16)
        vp = na_ref[:, C_NAV + pp * LANES:C_NAV + (pp + 1) * LANES].astype(BF16)
        outs = []
        for sel in (lo, ~lo):
            qh = jnp.where(sel, qp, 0.0).astype(BF16)
            outs.append(_softmax_pv([_dot_nt(qh, kp) * NA_SCALE], [vp]))
        ob_ref[:, pp * LANES:(pp + 1) * LANES] = jnp.where(lo, outs[0], outs[1])


def _ctx_attn(q, k, v, big):
    blk = lambda w: pl.BlockSpec((S_CTX, w), lambda b: (b, 0))
    return pl.pallas_call(
        _ctx_attn_body,
        grid=(B_CTX,),
        in_specs=[blk(512), blk(512), blk(256), blk(768)],
        out_specs=[blk(256), blk(256)],
        out_shape=[jax.ShapeDtypeStruct((T_CTX, 256), F32), jax.ShapeDtypeStruct((T_CTX, 256), F32)],
        name="ctx_attn",
    )(q, k, v, big)


TQ = 256


def _lat_mla_body(q_ref, kl_ref, vl_ref, kc_ref, vc_ref, o_ref):
    lo = _lane_lo((TQ, LANES))
    for pp in range(2):
        vlp = vl_ref[:, pp * LANES:(pp + 1) * LANES]
        vcp = vc_ref[:, pp * LANES:(pp + 1) * LANES]
        outs = []
        for h in (2 * pp, 2 * pp + 1):
            qh = q_ref[:, h * LANES:(h + 1) * LANES]
            s_l = _dot_nt(qh, kl_ref[:, h * LANES:(h + 1) * LANES]) * MLA_SCALE
            s_c = _dot_nt(qh, kc_ref[:, h * LANES:(h + 1) * LANES]) * MLA_SCALE
            outs.append(_softmax_pv([s_l, s_c], [vlp, vcp]))
        o_ref[:, pp * LANES:(pp + 1) * LANES] = jnp.where(lo, outs[0], outs[1])


def _lat_mla(q, k, v, kc, vc):
    nq = S_LAT // TQ
    qrow = lambda b, t: (T_CTX // TQ + b * nq + t, 0)
    lat = lambda b, t: (T_CTX // S_LAT + b, 0)
    return pl.pallas_call(
        _lat_mla_body,
        grid=(B_LAT, nq),
        in_specs=[pl.BlockSpec((TQ, 512), qrow), pl.BlockSpec((S_LAT, 512), lat), pl.BlockSpec((S_LAT, 256), lat),
                  pl.BlockSpec((PAST, 512), lambda b, t: (b, 0)), pl.BlockSpec((PAST, 256), lambda b, t: (b, 0))],
        out_specs=pl.BlockSpec((TQ, 256), lambda b, t: (b * nq + t, 0)),
        out_shape=jax.ShapeDtypeStruct((T_LAT, 256), F32),
        compiler_params=_cp(48),
        name="lat_mla",
    )(q, k, v, kc, vc)


NK_LOC = NA_WIN_H * GRID_W


def _na_row_start(r):
    return jnp.clip(r - NA_WIN_H // 2, 0, ROWS - NA_WIN_H)


def _lat_na_body(q_ref, k_ref, v_ref, kc_ref, vc_ref, bias_ref, o_ref):
    r = pl.program_id(1)
    k0 = pl.multiple_of(_na_row_start(r) * GRID_W, GRID_W)
    lo = _lane_lo((GRID_W, LANES))
    for pp in range(2):
        cols = slice(pp * LANES, (pp + 1) * LANES)
        qp = q_ref[:, cols]
        kl = k_ref[pl.ds(k0, NK_LOC), cols].astype(BF16)
        vl = v_ref[pl.ds(k0, NK_LOC), cols].astype(BF16)
        kc = kc_ref[:, cols].astype(BF16)
        vc = vc_ref[:, cols].astype(BF16)
        outs = []
        for j, sel in enumerate((lo, ~lo)):
            qh = jnp.where(sel, qp, 0.0).astype(BF16)
            s_loc = _dot_nt(qh, kl) * NA_SCALE + bias_ref[0, 2 * pp + j]
            s_ctx = _dot_nt(qh, kc) * NA_SCALE
            outs.append(_softmax_pv([s_loc, s_ctx], [vl, vc]))
        o_ref[:, cols] = jnp.where(lo, outs[0], outs[1])


def _lat_na(big, kc, vc, bias):
    qrow = lambda b, r: (T_CTX // GRID_W + b * ROWS + r, 0)
    boff = lambda b, r: (_na_row_start(r) - r + NA_WIN_H - 1, 0, 0, 0)
    lat = lambda c: (lambda b, r: (T_CTX // S_LAT + b, c))
    return pl.pallas_call(
        _lat_na_body,
        grid=(B_LAT, ROWS),
        in_specs=[pl.BlockSpec((GRID_W, 256), qrow), pl.BlockSpec((S_LAT, 256), lat(1)),
                  pl.BlockSpec((S_LAT, 256), lat(2)),
                  pl.BlockSpec((PAST, 256), lambda b, r: (b, 0)), pl.BlockSpec((PAST, 256), lambda b, r: (b, 0)),
                  pl.BlockSpec((1, NA_HEADS, GRID_W, NK_LOC), boff)],
        out_specs=pl.BlockSpec((GRID_W, 256), lambda b, r: (b * ROWS + r, 0)),
        out_shape=jax.ShapeDtypeStruct((T_LAT, 256), F32),
        compiler_params=_cp(48),
        name="lat_na",
    )(big, big, big, kc, vc, bias)


def _log_sigmoid(x):
    return jnp.minimum(x, 0.0) - jnp.log1p(jnp.exp(-jnp.abs(x)))


def _gla_chunk(rev, q, k, v, la, st):
    C, SB = GLA_CHUNK, GLA_SUB
    ri = lax.broadcasted_iota(jnp.int32, (C, C), 0)
    ci = lax.broadcasted_iota(jnp.int32, (C, C), 1)
    tri = ((ci >= ri) if rev else (ci <= ri)).astype(F32)
    cb = _dot(tri, la, precision=HIGHEST)
    cl = cb[0:1] if rev else cb[C - 1:C]
    o = _dot_nt((q * jnp.exp(cb)).astype(BF16), st.astype(BF16))
    kd = (k * jnp.exp(cl - cb)).astype(BF16)
    bd = (lax.broadcasted_iota(jnp.int32, (256, LANES), 0) // GLA_DV
          == lax.broadcasted_iota(jnp.int32, (256, LANES), 1) // GLA_DK)
    st_new = st * jnp.exp(cl) + jnp.where(bd, _dot_tn(v.astype(BF16), kd), 0.0)

    vb = v.astype(BF16)
    lane_h = lax.broadcasted_iota(jnp.int32, (SB, LANES), 1) // GLA_DK
    col_h = lax.broadcasted_iota(jnp.int32, (SB, 256), 1) // GLA_DV
    expand = (lax.broadcasted_iota(jnp.int32, (LANES, 256), 0) // GLA_DK
              == lax.broadcasted_iota(jnp.int32, (LANES, 256), 1) // GLA_DV).astype(BF16)
    row = lax.broadcasted_iota(jnp.int32, (C, LANES), 0)
    pj = lax.broadcasted_iota(jnp.int32, (SB * SB, LANES), 0) // SB
    pi = lax.broadcasted_iota(jnp.int32, (SB * SB, LANES), 0) % SB
    pmask = (pi <= pj) if rev else (pi >= pj)
    nsub = C // SB
    parts = []
    for i in range(nsub):
        sl = slice(i * SB, (i + 1) * SB)
        qi, ki, cbi, vi = q[sl], k[sl], cb[sl], v[sl]
        acc = o[sl]
        has_off = (i < nsub - 1) if rev else (i > 0)
        if has_off:
            if rev:
                ref = cb[(i + 1) * SB:(i + 1) * SB + 1]
                outside = row >= (i + 1) * SB
            else:
                ref = cb[i * SB - 1:i * SB]
                outside = row < i * SB
            a = qi * jnp.exp(cbi - ref)
            bm = jnp.where(outside, k * jnp.exp(jnp.minimum(ref - cb, 0.0)), 0.0).astype(BF16)
            a4 = jnp.concatenate([jnp.where(lane_h == h, a, 0.0) for h in range(GLA_HEADS)], axis=0)
            att = _dot_nt(a4.astype(BF16), bm)
            pv = _dot(att.astype(BF16), vb)
            for h in range(GLA_HEADS):
                acc = acc + jnp.where(col_h == h, pv[h * SB:(h + 1) * SB], 0.0)
        kj = jnp.broadcast_to(ki[:, None, :], (SB, SB, LANES)).reshape(SB * SB, LANES)
        cbj = jnp.broadcast_to(cbi[:, None, :], (SB, SB, LANES)).reshape(SB * SB, LANES)
        qq = jnp.broadcast_to(qi[None, :, :], (SB, SB, LANES)).reshape(SB * SB, LANES)
        cbq = jnp.broadcast_to(cbi[None, :, :], (SB, SB, LANES)).reshape(SB * SB, LANES)
        x = qq * kj * jnp.where(pmask, jnp.exp(jnp.where(pmask, cbq - cbj, 0.0)), 0.0)
        y = _dot(x.astype(BF16), expand).reshape(SB, SB, 256)
        acc = acc + jnp.sum(y * vi[:, None, :], axis=0)
        parts.append(acc)
    return jnp.concatenate(parts, axis=0), st_new


def _gla_body(S, has_s0, *refs):
    if has_s0:
        (gq_ref, gk_ref, gv_ref, gg_ref, seg_ref, wg_ref, bg_ref, gn_ref, s0_ref,
         o_ref, sfin_ref, la_scr, of_scr, ob_scr, st_scr) = refs
    else:
        (gq_ref, gk_ref, gv_ref, gg_ref, seg_ref, wg_ref, bg_ref, gn_ref,
         o_ref, sfin_ref, la_scr, of_scr, ob_scr, st_scr) = refs
    nc = S // GLA_CHUNK
    seg = seg_ref[...].astype(BF16)
    for d in range(2):
        la_scr[d] = _log_sigmoid(_dot(seg, wg_ref[d]) + bg_ref[d]) * (1.0 / GLA_NORM)
        if has_s0:
            st_scr[d] = s0_ref[0, d]
        else:
            st_scr[d] = jnp.zeros((256, LANES), F32)

    def body(c, carry):
        for d, out_scr in ((0, of_scr), (1, ob_scr)):
            r0 = pl.multiple_of((c if d == 0 else nc - 1 - c) * GLA_CHUNK, GLA_CHUNK)
            rows = pl.ds(r0, GLA_CHUNK)
            q = gq_ref[rows, :] * (GLA_DK ** -0.5)
            o, st = _gla_chunk(d == 1, q, gk_ref[rows, :], gv_ref[rows, :], la_scr[d, rows, :], st_scr[d])
            out_scr[rows, :] = o
            st_scr[d] = st
        return carry

    lax.fori_loop(0, nc, body, 0)
    sfin_ref[0] = st_scr[...]
    o = of_scr[...] + ob_scr[...]
    ones = (lax.broadcasted_iota(jnp.int32, (256, 256), 0) // GLA_DV
            == lax.broadcasted_iota(jnp.int32, (256, 256), 1) // GLA_DV).astype(F32)
    ms = _dot(o * o, ones, precision=HIGHEST) * (1.0 / GLA_DV)
    g = gg_ref[...]
    o_ref[...] = o * lax.rsqrt(ms + EPS) * gn_ref[...] * (g * jax.nn.sigmoid(g))


def _gla(big, seg, lw, s0):
    has_s0 = s0 is not None
    S, nb, base = (S_LAT, B_LAT, T_CTX // S_LAT) if has_s0 else (S_CTX, B_CTX, 0)
    blk = lambda w, c: pl.BlockSpec((S, w), lambda b: (base + b, c))
    in_specs = [blk(LANES, C_GQ // LANES), blk(LANES, C_GK // LANES), blk(256, C_GV // 256), blk(256, C_GG // 256),
                blk(LANES, 0), _const_spec((2, LANES, LANES)), _const_spec((2, 1, LANES)), _const_spec((1, 256))]
    args = [big, big, big, big, seg, lw["wg"], lw["bg"], lw["gla_norm_g"]]
    if has_s0:
        in_specs += [pl.BlockSpec((1, 2, 256, LANES), lambda b: (b, 0, 0, 0))]
        args += [s0]
    return pl.pallas_call(
        functools.partial(_gla_body, S, has_s0),
        grid=(nb,),
        in_specs=in_specs,
        out_specs=[pl.BlockSpec((S, 256), lambda b: (b, 0)), pl.BlockSpec((1, 2, 256, LANES), lambda b: (b, 0, 0, 0))],
        out_shape=[jax.ShapeDtypeStruct((nb * S, 256), F32), jax.ShapeDtypeStruct((nb, 2, 256, LANES), F32)],
        scratch_shapes=[pltpu.VMEM((2, S, LANES), F32), pltpu.VMEM((S, 256), F32), pltpu.VMEM((S, 256), F32),
                        pltpu.VMEM((2, 256, LANES), F32)],
        compiler_params=_cp(48),
        name="gla_lat" if has_s0 else "gla_ctx",
    )(*args)


def _softplus(x):
    return jnp.maximum(x, 0.0) + jnp.log1p(jnp.exp(-jnp.abs(x)))


def _lru_body(S, has_h0, *refs):
    if has_h0:
        (lx_ref, lg_ref, cw_ref, cb_ref, wg_ref, bgate_ref, lam_ref, h0_ref,
         o_ref, hfin_ref, a_scr, b_scr, hf_scr, hb_scr) = refs
    else:
        (lx_ref, lg_ref, cw_ref, cb_ref, wg_ref, bgate_ref, lam_ref,
         o_ref, hfin_ref, a_scr, b_scr, hf_scr, hb_scr) = refs
    x = lx_ref[...]
    row = lax.broadcasted_iota(jnp.int32, (S, LRU_W), 0)
    xm1 = jnp.where(row >= 1, pltpu.roll(x, 1, 0), 0.0)
    xp1 = jnp.where(row < S - 1, pltpu.roll(x, S - 1, 0), 0.0)
    xp2 = jnp.where(row < S - 2, pltpu.roll(x, S - 2, 0), 0.0)
    xc = cb_ref[...] + xm1 * cw_ref[0:1] + x * cw_ref[1:2] + xp1 * cw_ref[2:3] + xp2 * cw_ref[3:4]
    gates = _dot(xc.astype(BF16), wg_ref[...]) + bgate_ref[...]
    for d in range(2):
        r = jax.nn.sigmoid(gates[:, d * LRU_W:(d + 1) * LRU_W])
        ig = jax.nn.sigmoid(gates[:, (2 + d) * LRU_W:(3 + d) * LRU_W])
        log_a = -LRU_C * r * _softplus(-lam_ref[d])
        a_scr[d] = jnp.exp(log_a)
        b_scr[d] = jnp.sqrt(1.0 - jnp.exp(2.0 * log_a)) * (ig * xc)
    if has_h0:
        h_init = (h0_ref[0, 0], h0_ref[0, 1])
    else:
        h_init = (jnp.zeros((1, LRU_W), F32), jnp.zeros((1, LRU_W), F32))

    def body(i, carry):
        hf, hb = carry
        r0 = pl.multiple_of(i * 8, 8)
        rb = pl.multiple_of(S - 8 - i * 8, 8)
        af, bf = a_scr[0, pl.ds(r0, 8), :], b_scr[0, pl.ds(r0, 8), :]
        ab, bb = a_scr[1, pl.ds(rb, 8), :], b_scr[1, pl.ds(rb, 8), :]
        fs, bs = [], [None] * 8
        for t in range(8):
            hf = af[t:t + 1] * hf + bf[t:t + 1]
            fs.append(hf)
            u = 7 - t
            hb = ab[u:u + 1] * hb + bb[u:u + 1]
            bs[u] = hb
        hf_scr[pl.ds(r0, 8), :] = jnp.concatenate(fs, axis=0)
        hb_scr[pl.ds(rb, 8), :] = jnp.concatenate(bs, axis=0)
        return hf, hb

    hf, hb = lax.fori_loop(0, S // 8, body, h_init)
    hfin_ref[0, 0] = hf
    hfin_ref[0, 1] = hb
    g = lg_ref[...]
    gelu = g * (0.5 * (1.0 + jnp.tanh(np.sqrt(2.0 / np.pi).astype(np.float32) * (g + 0.044715 * (g * g * g)))))
    o_ref[...] = (hf_scr[...] + hb_scr[...]) * gelu


def _lru(big, lw, h0):
    has_h0 = h0 is not None
    S, nb, base = (S_LAT, B_LAT, T_CTX // S_LAT) if has_h0 else (S_CTX, B_CTX, 0)
    blk = lambda c: pl.BlockSpec((S, LRU_W), lambda b: (base + b, c))
    in_specs = [blk(C_LX // LRU_W), blk(C_LG // LRU_W), _const_spec((4, LRU_W)), _const_spec((1, LRU_W)),
                _const_spec((LRU_W, 4 * LRU_W)), _const_spec((1, 4 * LRU_W)), _const_spec((2, 1, LRU_W))]
    args = [big, big, lw["lru_conv_w"], lw["lru_conv_b"], lw["lru_wg"], lw["lru_bg"], lw["lru_lambda"]]
    if has_h0:
        in_specs += [pl.BlockSpec((1, 2, 1, LRU_W), lambda b: (b, 0, 0, 0))]
        args += [h0]
    return pl.pallas_call(
        functools.partial(_lru_body, S, has_h0),
        grid=(nb,),
        in_specs=in_specs,
        out_specs=[pl.BlockSpec((S, LRU_W), lambda b: (b, 0)),
                   pl.BlockSpec((1, 2, 1, LRU_W), lambda b: (b, 0, 0, 0))],
        out_shape=[jax.ShapeDtypeStruct((nb * S, LRU_W), F32), jax.ShapeDtypeStruct((nb, 2, 1, LRU_W), F32)],
        scratch_shapes=[pltpu.VMEM((2, S, LRU_W), F32), pltpu.VMEM((2, S, LRU_W), F32),
                        pltpu.VMEM((S, LRU_W), F32), pltpu.VMEM((S, LRU_W), F32)],
        compiler_params=_cp(48),
        name="lru_lat" if has_h0 else "lru_ctx",
    )(*args)


def _merge_body(x_ref, *refs):
    (mod_ref, wout_ref, g2_ref, rwh_ref, rwl_ref, rb_ref, x1_ref, h2_ref, idx_ref, gw_ref) = refs[8:]
    is_ctx = pl.program_id(0) < NT_CTX
    mix = 0.0
    for j in range(4):
        o = jnp.where(is_ctx, refs[2 * j][...], refs[2 * j + 1][...])
        mix = mix + _dot(o.astype(BF16), wout_ref[j * 256:(j + 1) * 256, :])
    m = mod_ref[0]
    g1, sh2, sc2 = m[:, 2 * D:3 * D], m[:, 3 * D:4 * D], m[:, 4 * D:5 * D]
    x1 = x_ref[...] + g1 * mix
    x1_ref[...] = x1
    h2 = _rms(x1) * g2_ref[...] * (1.0 + sc2) + sh2
    for s in range(SUB):
        h2_ref[:, s, :] = h2[:, s * LANES:(s + 1) * LANES]
    hi = h2.astype(BF16)
    lo = (h2 - hi.astype(F32)).astype(BF16)
    logits = _dot(hi, rwh_ref[...]) + _dot(hi, rwl_ref[...]) + _dot(lo, rwh_ref[...]) + rb_ref[...]
    lane = lax.broadcasted_iota(jnp.int32, (TM, LANES), 1)
    idx_out = jnp.zeros((TM, LANES), jnp.int32)
    val_out = jnp.zeros((TM, LANES), F32)
    top = None
    den = 0.0
    for kk in range(TOP_K):
        mval = logits.max(axis=-1, keepdims=True)
        midx = jnp.where(logits == mval, lane, LANES).min(axis=-1, keepdims=True)
        if kk == 0:
            top = mval
        e = jnp.exp(mval - top)
        den = den + e
        idx_out = jnp.where(lane == kk, midx, idx_out)
        val_out = jnp.where(lane == kk, e, val_out)
        logits = jnp.where(lane == midx, -jnp.inf, logits)
    idx_ref[...] = idx_out
    gw_ref[...] = val_out / den


def _merge(x, mix_parts, mods_l, lw):
    tok = lambda w: pl.BlockSpec((TM, w), lambda i: (i, 0))
    ctx = pl.BlockSpec((TM, 256), lambda i: (jnp.minimum(i, NT_CTX - 1), 0))
    lat = pl.BlockSpec((TM, 256), lambda i: (jnp.maximum(i - NT_CTX, 0), 0))
    return pl.pallas_call(
        _merge_body,
        grid=(NT_ALL,),
        in_specs=[tok(D)] + [ctx, lat] * 4 + [
                  pl.BlockSpec((1, 1, 6 * D), lambda i: (_mod_row(i), 0, 0)),
                  _const_spec((D, D)), _const_spec((1, D)), _const_spec((D, LANES)), _const_spec((D, LANES)),
                  _const_spec((1, LANES))],
        out_specs=[tok(D), pl.BlockSpec((TM, SUB, LANES), lambda i: (i, 0, 0)), tok(LANES), tok(LANES)],
        out_shape=[jax.ShapeDtypeStruct((T_ALL, D), F32), jax.ShapeDtypeStruct((T_ALL, SUB, LANES), F32),
                   jax.ShapeDtypeStruct((T_ALL, LANES), jnp.int32), jax.ShapeDtypeStruct((T_ALL, LANES), F32)],
        compiler_params=_cp(40),
        name="merge",
    )(x, *mix_parts, mods_l, lw["w_out"], lw["norm2_g"], lw["rw_hi"], lw["rw_lo"], lw["rb"])


def _moe_body(sn_ref, te_ref, nv_ref, tb_ref, h2_hbm, wgu_ref, bgu_ref, wd_ref, bd_ref,
              y_hbm, xbuf, obuf, wgu_bf, wd_bf, gsem, ssem):
    i = pl.program_id(0)
    slot = i % 2
    cur = jnp.minimum(i, NT_MOE - 1)
    valid = (i < NT_MOE) & (nv_ref[cur] > 0)
    prv = jnp.maximum(i - 1, 0)
    nv_prev = jnp.where(i >= 1, nv_ref[prv], 0)
    base_prev = tb_ref[prv]
    base_next = tb_ref[jnp.minimum(i + 1, NT_MOE - 1)]

    def gather_row(base, sl, r):
        tok = lax.shift_right_logical(sn_ref[base + r], TOP_K_SHIFT)
        pltpu.make_async_copy(h2_hbm.at[tok], xbuf.at[sl, r], gsem.at[sl]).start()

    def scatter_row(sl, r):
        dst = jnp.where(r < nv_prev, sn_ref[base_prev + r], N_ROWS + sl * TM_MOE + r)
        pltpu.make_async_copy(obuf.at[sl, r], y_hbm.at[dst], ssem.at[sl]).start()

    def for_other_slot(fn):
        for other in range(2):
            @pl.when(slot == 1 - other)
            def _():
                lax.fori_loop(0, TM_MOE, lambda r, c: fn(other, r) or c, 0, unroll=8)

    def gather_done(sl):
        pltpu.make_async_copy(h2_hbm.at[pl.ds(0, TM_MOE)], xbuf.at[sl], gsem.at[sl]).wait()

    def scatter_done(sl):
        pltpu.make_async_copy(obuf.at[sl], y_hbm.at[pl.ds(0, TM_MOE)], ssem.at[sl]).wait()

    @pl.when(i == 0)
    def _():
        obuf[1] = jnp.zeros((TM_MOE, SUB, LANES), F32)
        cp = pltpu.make_async_copy(obuf.at[1], y_hbm.at[pl.ds(N_ROWS, TM_MOE)], ssem.at[1])
        cp.start()
        cp.wait()

        def body(r, c):
            gather_row(tb_ref[0], 0, r)
            return c
        lax.fori_loop(0, TM_MOE, body, 0, unroll=8)

    e = te_ref[cur]
    prev_e = te_ref[prv]

    @pl.when(valid & ((i == 0) | (e != prev_e)))
    def _():
        wgu_bf[...] = wgu_ref[0].astype(BF16)
        wd_bf[...] = wd_ref[0].astype(BF16)

    @pl.when(valid)
    def _():
        def both(other, r):
            gather_row(base_next, other, r)
            scatter_row(other, r)
        for_other_slot(both)
        gather_done(slot)
        x = jnp.concatenate([xbuf[slot, :, s, :] for s in range(SUB)], axis=-1).astype(BF16)
        acc = jnp.zeros((TM_MOE, D), F32)
        for j in range(D_EXP // EXP_CHUNK):
            c0 = j * EXP_CHUNK
            xg = _dot(x, wgu_bf[:, c0:c0 + EXP_CHUNK]) + bgu_ref[0][:, c0:c0 + EXP_CHUNK]
            xl = _dot(x, wgu_bf[:, D_EXP + c0:D_EXP + c0 + EXP_CHUNK]) + bgu_ref[0][:, D_EXP + c0:D_EXP + c0 + EXP_CHUNK]
            xg = jnp.minimum(xg, SWIGLU_LIMIT)
            xl = jnp.clip(xl, -SWIGLU_LIMIT, SWIGLU_LIMIT)
            act = xg * jax.nn.sigmoid(SWIGLU_ALPHA * xg) * (xl + 1.0)
            acc = acc + _dot(act.astype(BF16), wd_bf[c0:c0 + EXP_CHUNK, :])
        out = acc + bd_ref[0]
        for s in range(SUB):
            obuf[slot, :, s, :] = out[:, s * LANES:(s + 1) * LANES]
        scatter_done(1 - slot)

    @pl.when(jnp.logical_not(valid) & (nv_prev > 0))
    def _():
        gather_done(slot)
        for_other_slot(scatter_row)
        scatter_done(1 - slot)


def _moe(sorted_n, tile_exp, tile_nvalid, tile_base, h2, lw):
    wmap = lambda i, sn, te, nv, tb: (te[jnp.minimum(i, NT_MOE - 1)], 0, 0)
    grid_spec = pltpu.PrefetchScalarGridSpec(
        num_scalar_prefetch=4,
        grid=(NT_MOE + 1,),
        in_specs=[
            pl.BlockSpec(memory_space=pl.ANY),
            pl.BlockSpec((1, D, 2 * D_EXP), wmap),
            pl.BlockSpec((1, 1, 2 * D_EXP), wmap),
            pl.BlockSpec((1, D_EXP, D), wmap),
            pl.BlockSpec((1, 1, D), wmap),
        ],
        out_specs=pl.BlockSpec(memory_space=pl.ANY),
        scratch_shapes=[pltpu.VMEM((2, TM_MOE, SUB, LANES), F32), pltpu.VMEM((2, TM_MOE, SUB, LANES), F32),
                        pltpu.VMEM((D, 2 * D_EXP), BF16), pltpu.VMEM((D_EXP, D), BF16),
                        pltpu.SemaphoreType.DMA((2,)), pltpu.SemaphoreType.DMA((2,))],
    )
    return pl.pallas_call(
        _moe_body,
        grid_spec=grid_spec,
        out_shape=jax.ShapeDtypeStruct((N_ROWS + N_DUMP, SUB, LANES), F32),
        compiler_params=pltpu.CompilerParams(vmem_limit_bytes=56 * 1024 * 1024, dimension_semantics=("arbitrary",),
                                             disable_bounds_checks=True),
        name="moe",
    )(sorted_n, tile_exp, tile_nvalid, tile_base, h2,
      lw["exp_w_gu"], lw["exp_b_gu"], lw["exp_w_down"], lw["exp_b_down"])


def _route(idx):
    flat = idx.reshape(-1)
    ids = jnp.arange(N_ROWS, dtype=jnp.int32)
    sorted_n = jnp.concatenate([jnp.sort(flat * N_ROWS + ids) % N_ROWS, jnp.zeros((TM_MOE,), jnp.int32)])
    experts = jnp.arange(N_EXP, dtype=jnp.int32)
    counts = jnp.sum((flat[None, :] == experts[:, None]).astype(jnp.int32), axis=1)
    padded = ((counts + TM_MOE - 1) // TM_MOE) * TM_MOE
    pad_ends = jnp.cumsum(padded)
    pad_starts = pad_ends - padded
    unp_starts = jnp.cumsum(counts) - counts
    tile_start = jnp.arange(NT_MOE, dtype=jnp.int32) * TM_MOE
    tile_exp = jnp.minimum(jnp.sum((pad_ends[None, :] <= tile_start[:, None]).astype(jnp.int32), axis=1), N_EXP - 1)
    onehot = (tile_exp[:, None] == experts[None, :]).astype(jnp.int32)
    pick = lambda v: jnp.sum(onehot * v[None, :], axis=1)
    offset = tile_start - pick(pad_starts)
    tile_nvalid = jnp.where(tile_start < pad_ends[-1], jnp.clip(pick(counts) - offset, 0, TM_MOE), 0)
    tile_base = jnp.clip(pick(unp_starts) + offset, 0, N_ROWS - 1)
    return sorted_n, tile_exp, tile_nvalid.astype(jnp.int32), tile_base.astype(jnp.int32)


def _final_body(x_ref, moe_ref, gw_ref, mod_ref, g_ref, o_ref):
    g2 = mod_ref[0][:, 5 * D:6 * D]
    o_ref[...] = _rms(x_ref[...] + g2 * _sum_slots(moe_ref, gw_ref)) * g_ref[...]


def _final(x1, moe, gw, mods_l, g, tile0, ntiles):
    tok = pl.BlockSpec((TM, D), lambda i: (tile0 + i, 0))
    return pl.pallas_call(
        _final_body,
        grid=(ntiles,),
        in_specs=[tok, pl.BlockSpec((TM, TOP_K * SUB, LANES), lambda i: (tile0 + i, 0, 0)),
                  pl.BlockSpec((TM, LANES), lambda i: (tile0 + i, 0)),
                  pl.BlockSpec((1, 1, 6 * D), lambda i: (_mod_row(tile0 + i), 0, 0)), _const_spec((1, D))],
        out_specs=pl.BlockSpec((TM, D), lambda i: (i, 0)),
        out_shape=jax.ShapeDtypeStruct((ntiles * TM, D), F32),
        name="final_norm",
    )(x1, moe, gw, mods_l, g)


def _rope_perm(w):
    parts = []
    for a in range(2):
        x1, x2 = w[..., a * 16:a * 16 + 8], w[..., a * 16 + 8:a * 16 + 16]
        parts += [-x2, x1]
    return jnp.concatenate(parts, axis=-1)


def _rope_tables():
    t = np.arange(S_LAT)
    freqs = ROPE_BASE ** (-np.arange(8, dtype=np.float64) / 8)
    cs, sn = [], []
    for pos in (t // GRID_W, t % GRID_W):
        ang = pos.astype(np.float64)[:, None] * freqs[None, :]
        cs += [np.cos(ang), np.cos(ang)]
        sn += [np.sin(ang), np.sin(ang)]
    cos = np.concatenate([np.ones((ID_ROWS, 32), np.float32), np.concatenate(cs, -1).astype(np.float32)], 0)
    sin = np.concatenate([np.zeros((ID_ROWS, 32), np.float32), np.concatenate(sn, -1).astype(np.float32)], 0)
    n = cos.shape[0]
    one, zero = np.ones((n, 1), np.float32), np.zeros((n, 1), np.float32)
    cq_h = np.concatenate([np.repeat(one, 64, 1), cos, np.repeat(zero, 32, 1)], 1)
    sq_h = np.concatenate([np.repeat(zero, 64, 1), sin, np.repeat(zero, 32, 1)], 1)
    cseg = np.concatenate([cos, np.repeat(one, 96, 1)], 1)
    sseg = np.concatenate([sin, np.repeat(zero, 96, 1)], 1)
    return dict(cq=jnp.asarray(np.tile(cq_h, (1, 4))), sq=jnp.asarray(np.tile(sq_h, (1, 4))),
                cs=jnp.asarray(cseg), ss=jnp.asarray(sseg))


def _na_bias_table(rpb):
    qc = np.arange(GRID_W)[:, None]
    kc = np.arange(GRID_W)[None, :]
    cs = np.clip(qc - NA_WIN_W // 2, 0, GRID_W - NA_WIN_W)
    valid = (kc >= cs) & (kc < cs + NA_WIN_W)
    dci = np.clip(kc - qc, -(NA_WIN_W - 1), NA_WIN_W - 1) + NA_WIN_W - 1
    off = np.arange(NA_WIN_H)[:, None] + np.arange(NA_WIN_H)[None, :]
    tab = rpb[:, off[:, :, None, None], dci[None, None, :, :]]
    tab = jnp.where(valid[None, None, None], tab, NEG)
    return tab.transpose(1, 0, 3, 2, 4).reshape(NA_WIN_H, NA_HEADS, GRID_W, NK_LOC).astype(F32)


def _layer_weights(l, p):
    w_in = p["w_in"][l]
    z = lambda n: jnp.zeros((D, n), F32)
    kr = w_in[:, 320:352]
    w_perm = jnp.concatenate([
        w_in[:, 352:1120], w_in[:, 1120:1248], w_in[:, 1248:1376], w_in[:, 1376:1632], w_in[:, 1632:1888],
        w_in[:, 1920:2176], w_in[:, 2176:2432], w_in[:, 192:320], w_in[:, 0:192], z(64),
        kr, w_in[:, 1888:1920], z(64), _rope_perm(kr), z(96)], axis=1).astype(BF16)
    wuq = p["mla_w_uq"][l].reshape(MLA_Q_LORA, MLA_HEADS, MLA_NOPE + MLA_ROPE)
    zq = jnp.zeros((MLA_Q_LORA, MLA_HEADS, 32), F32)
    wq1 = jnp.concatenate([wuq, zq], axis=-1).reshape(MLA_Q_LORA, 512)
    wq2 = jnp.concatenate([jnp.zeros((MLA_Q_LORA, MLA_HEADS, MLA_NOPE), F32), _rope_perm(wuq[..., MLA_NOPE:]), zq],
                          axis=-1).reshape(MLA_Q_LORA, 512)
    wq = jnp.pad(jnp.concatenate([wq1, wq2], axis=1), ((0, 256 - MLA_Q_LORA), (0, 0))).astype(BF16)
    qg = jnp.pad(p["mla_q_norm_g"][l], (0, 256 - MLA_Q_LORA)).reshape(1, 256)
    wukv = p["mla_w_ukv"][l].reshape(MLA_KV_LORA, MLA_HEADS, MLA_NOPE + MLA_V)
    wk = jnp.concatenate([wukv[..., :MLA_NOPE], jnp.zeros((MLA_KV_LORA, MLA_HEADS, 64), F32)], -1).reshape(MLA_KV_LORA, 512)
    wv = wukv[..., MLA_NOPE:].reshape(MLA_KV_LORA, 256)
    eye = jnp.eye(MLA_ROPE, dtype=F32)[:, None, :]
    ek = jnp.concatenate([jnp.zeros((MLA_ROPE, MLA_HEADS, MLA_NOPE), F32),
                          jnp.broadcast_to(eye, (MLA_ROPE, MLA_HEADS, MLA_ROPE)),
                          jnp.zeros((MLA_ROPE, MLA_HEADS, 32), F32)], -1).reshape(MLA_ROPE, 512)
    wkv = jnp.concatenate([
        jnp.concatenate([wk, wv], axis=1),
        jnp.concatenate([ek, jnp.zeros((MLA_ROPE, 256), F32)], axis=1),
        jnp.zeros((LANES - MLA_ROPE, 768), F32)], axis=0).astype(BF16)
    wg = jnp.zeros((2, LANES, LANES), F32)
    for d in range(2):
        wg = wg.at[d, SEG_GA + d * GLA_RANK:SEG_GA + (d + 1) * GLA_RANK].set(p["gla_w_gate"][l, d])
    def bdiag(w):
        out = jnp.zeros((LRU_W, LRU_W), F32)
        nb = LRU_W // LRU_BLOCKS
        for n in range(LRU_BLOCKS):
            out = out.at[n * nb:(n + 1) * nb, n * nb:(n + 1) * nb].set(w[n])
        return out
    lru_wg = jnp.concatenate([bdiag(p["lru_w_a"][l, 0]), bdiag(p["lru_w_a"][l, 1]),
                              bdiag(p["lru_w_x"][l, 0]), bdiag(p["lru_w_x"][l, 1])], axis=1).astype(BF16)
    lru_bg = jnp.concatenate([p["lru_b_a"][l, 0], p["lru_b_a"][l, 1], p["lru_b_x"][l, 0], p["lru_b_x"][l, 1]]).reshape(1, 4 * LRU_W)
    rw = jnp.pad(p["router_w"][l], ((0, 0), (0, LANES - N_EXP)))
    rw_hi = rw.astype(BF16)
    rw_lo = (rw - rw_hi.astype(F32)).astype(BF16)
    rb = jnp.concatenate([p["router_b"][l], jnp.full((LANES - N_EXP,), -jnp.inf, F32)]).reshape(1, LANES)
    return dict(
        w_in=w_perm, norm1_g=p["norm1_g"][l].reshape(1, D), norm2_g=p["norm2_g"][l].reshape(1, D),
        qg=qg, wq=wq, kvg=p["mla_kv_norm_g"][l].reshape(1, LANES), wkv=wkv,
        wg=wg.astype(BF16), bg=p["gla_b_gate"][l].reshape(2, 1, LANES), gla_norm_g=p["gla_norm_g"][l].reshape(1, 256),
        lru_conv_w=p["lru_conv_w"][l], lru_conv_b=p["lru_conv_b"][l].reshape(1, LRU_W), lru_wg=lru_wg, lru_bg=lru_bg,
        lru_lambda=p["lru_lambda"][l].reshape(2, 1, LRU_W),
        w_out=p["w_out"][l].astype(BF16), rw_hi=rw_hi, rw_lo=rw_lo, rb=rb,
        exp_w_gu=p["exp_w_gu"][l], exp_b_gu=p["exp_b_gu"][l].reshape(N_EXP, 1, 2 * D_EXP),
        exp_w_down=p["exp_w_down"][l], exp_b_down=p["exp_b_down"][l].reshape(N_EXP, 1, D),
        na_bias=_na_bias_table(p["na_rpb"][l]),
    )


def _state_to_blockdiag(s):
    eye = jnp.eye(GLA_HEADS, dtype=s.dtype)
    return jnp.einsum("bdhkv,hg->bdhvgk", s, eye).reshape(s.shape[0], 2, 256, LANES)


def _blockdiag_to_state(st):
    b = st.shape[0]
    s6 = st.reshape(b, 2, GLA_HEADS, GLA_DV, GLA_HEADS, GLA_DK)
    diag = jnp.stack([s6[:, :, h, :, h, :] for h in range(GLA_HEADS)], axis=2)
    return diag.transpose(0, 1, 2, 4, 3)


def kernel(x_prompt, x_sample, cache_mla_ckv, cache_mla_krope, cache_na_k, cache_na_v, state_gla, state_lru, c, c_ctx, norm1_g, norm2_g, w_mod, b_mod, w_in, w_out, mla_q_norm_g, mla_w_uq, mla_kv_norm_g, mla_w_ukv, na_rpb, gla_w_gate, gla_b_gate, gla_norm_g, lru_conv_w, lru_conv_b, lru_w_a, lru_b_a, lru_w_x, lru_b_x, lru_lambda, router_w, router_b, exp_w_gu, exp_b_gu, exp_w_down, exp_b_down, final_norm_g):
    p = dict(norm1_g=norm1_g, norm2_g=norm2_g, w_in=w_in, w_out=w_out, mla_q_norm_g=mla_q_norm_g, mla_w_uq=mla_w_uq,
             mla_kv_norm_g=mla_kv_norm_g, mla_w_ukv=mla_w_ukv, na_rpb=na_rpb, gla_w_gate=gla_w_gate,
             gla_b_gate=gla_b_gate, gla_norm_g=gla_norm_g, lru_conv_w=lru_conv_w, lru_conv_b=lru_conv_b,
             lru_w_a=lru_w_a, lru_b_a=lru_b_a, lru_w_x=lru_w_x, lru_b_x=lru_b_x, lru_lambda=lru_lambda,
             router_w=router_w, router_b=router_b, exp_w_gu=exp_w_gu, exp_b_gu=exp_b_gu, exp_w_down=exp_w_down,
             exp_b_down=exp_b_down)
    x = jnp.concatenate([x_prompt.reshape(T_CTX, D), x_sample.reshape(T_LAT, D)], axis=0)
    crow = jnp.concatenate([c_ctx[None, :], c, jnp.zeros((8 - 1 - B_LAT, D), F32)], axis=0)
    mods = _modulation(crow, w_mod, b_mod).reshape(DEPTH, 8, 1, 6 * D)
    tabs = _rope_tables()

    col = [[] for _ in range(6)]
    moe = None
    x1 = None
    for l in range(DEPTH):
        lw = _layer_weights(l, p)
        if l == 0:
            x, big, seg, ckv, q, k, v = _proj(True, x, None, None, None, mods[l], lw, tabs)
        else:
            x, big, seg, ckv, q, k, v = _proj(False, x1, moe, gw, mods[l - 1], mods[l], lw, tabs)
        kr_c = jnp.pad(cache_mla_krope[:, l].reshape(B_LAT * PAST, MLA_ROPE), ((0, 0), (0, LANES - MLA_ROPE)))
        kc, vc = _cache_kv(cache_mla_ckv[:, l].reshape(B_LAT * PAST, MLA_KV_LORA), kr_c, lw["wkv"])
        oa_c, ob_c = _ctx_attn(q, k, v, big)
        oa_l = _lat_mla(q, k, v, kc, vc)
        ob_l = _lat_na(big, cache_na_k[:, l].reshape(B_LAT * PAST, 256), cache_na_v[:, l].reshape(B_LAT * PAST, 256),
                       lw["na_bias"])
        oc_c, sg = _gla(big, seg, lw, None)
        oc_l, _ = _gla(big, seg, lw, _state_to_blockdiag(state_gla[:, l]))
        od_c, sl = _lru(big, lw, None)
        od_l, _ = _lru(big, lw, state_lru[:, l].reshape(B_LAT, 2, 1, LRU_W))
        x1, h2, idx, gw = _merge(x, (oa_c, oa_l, ob_c, ob_l, oc_c, oc_l, od_c, od_l), mods[l], lw)
        y = _moe(*_route(idx[:, :TOP_K]), h2, lw)
        moe = y.reshape((N_ROWS + N_DUMP) // TOP_K, TOP_K * SUB, LANES)
        col[0].append(ckv[:T_CTX].reshape(B_CTX, S_CTX, MLA_KV_LORA))
        col[1].append(seg[:T_CTX, :MLA_ROPE].reshape(B_CTX, S_CTX, MLA_ROPE))
        col[2].append(big[:T_CTX, C_NAK:C_NAK + 256].reshape(B_CTX, S_CTX, NA_HEADS, NA_DH))
        col[3].append(big[:T_CTX, C_NAV:C_NAV + 256].reshape(B_CTX, S_CTX, NA_HEADS, NA_DH))
        col[4].append(_blockdiag_to_state(sg))
        col[5].append(sl.reshape(B_CTX, 2, LRU_W))
    fg = final_norm_g.reshape(1, D)
    y_prompt = _final(x1, moe, gw, mods[DEPTH - 1], fg, 0, NT_CTX).reshape(B_CTX, S_CTX, D)
    y_sample = _final(x1, moe, gw, mods[DEPTH - 1], fg, NT_CTX, NT_ALL - NT_CTX).reshape(B_LAT, S_LAT, D)
    return (y_prompt, y_sample) + tuple(jnp.stack(cl, axis=1) for cl in col)
```

```python
import functools

import numpy as np
import jax
import jax.numpy as jnp
from jax import lax
from jax.experimental import pallas as pl
from jax.experimental.pallas import tpu as pltpu

F32 = jnp.float32
BF16 = jnp.bfloat16
HIGHEST = lax.Precision.HIGHEST

D = 1024
DEPTH = 2
B_CTX, S_CTX = 32, 256
B_LAT, S_LAT = 4, 2048
PAST = 512
T_CTX = B_CTX * S_CTX
T_LAT = B_LAT * S_LAT
T_ALL = T_CTX + T_LAT
GRID_W = 64
ROWS = S_LAT // GRID_W

MLA_HEADS, MLA_NOPE, MLA_ROPE, MLA_V = 4, 64, 32, 64
MLA_Q_LORA, MLA_KV_LORA = 192, 128
MLA_SCALE = (MLA_NOPE + MLA_ROPE) ** -0.5
NA_HEADS, NA_DH, NA_WIN_H, NA_WIN_W = 4, 64, 8, 16
NA_SCALE = NA_DH ** -0.5
GLA_HEADS, GLA_DK, GLA_DV, GLA_RANK, GLA_NORM, GLA_CHUNK = 4, 32, 64, 16, 16.0, 64
GLA_SUB = 16
LRU_W, LRU_BLOCKS, LRU_C = 256, 4, 8.0
N_EXP, TOP_K, D_EXP = 32, 4, 1024
SWIGLU_ALPHA, SWIGLU_LIMIT = 1.702, 7.0
ROPE_BASE = 10000.0
EPS = 1e-6
NEG = -1e30

LANES = 128
VMEM_BYTES = 64 * 1024 * 1024

TM = 256
NT_CTX = T_CTX // TM
NT_ALL = T_ALL // TM
TILES_PER_LAT = S_LAT // TM
ID_ROWS = 2 * TM

TM_MOE = 512
N_ROWS = T_ALL * TOP_K
R_MAX = N_ROWS + N_EXP * TM_MOE
NT_MOE = R_MAX // TM_MOE
EXP_CHUNK = 256
N_DUMP = 2 * TM_MOE
SUB = D // LANES
TOP_K_SHIFT = 2
assert 1 << TOP_K_SHIFT == TOP_K

C_NAQ, C_NAK, C_NAV, C_GQ, C_GK, C_GV, C_GG, C_LX, C_LG = 0, 256, 512, 768, 896, 1024, 1280, 1536, 1792
BIG = 2048
C_PKV, C_PQ, C_SEGA, C_SEGB = 2048, 2176, 2432, 2560
NC_IN = 2688
SEG_GA = 32


def _cp(vmem_mb, sem=None):
    kw = dict(vmem_limit_bytes=vmem_mb * 1024 * 1024)
    if sem is not None:
        kw["dimension_semantics"] = sem
    return pltpu.CompilerParams(**kw)


def _rms(x):
    return x * lax.rsqrt(jnp.mean(x * x, axis=-1, keepdims=True) + EPS)


def _dot(a, b, **kw):
    return jnp.dot(a, b, preferred_element_type=F32, **kw)


def _dot_nt(a, b):
    return lax.dot_general(a, b, (((1,), (1,)), ((), ())), preferred_element_type=F32)


def _dot_tn(a, b):
    return lax.dot_general(a, b, (((0,), (0,)), ((), ())), preferred_element_type=F32)


def _const_spec(shape):
    n = len(shape)
    return pl.BlockSpec(shape, lambda *_: (0,) * n)


def _mod_body(c_ref, w_ref, b_ref, o_ref):
    c = c_ref[...]
    s = c * jax.nn.sigmoid(c)
    o_ref[0] = _dot(s.astype(BF16), w_ref[0].astype(BF16)) + b_ref[0]


def _modulation(crow, w_mod, b_mod):
    tn = 1536
    return pl.pallas_call(
        _mod_body,
        grid=(DEPTH, 6 * D // tn),
        in_specs=[
            pl.BlockSpec((8, D), lambda l, j: (0, 0)),
            pl.BlockSpec((1, D, tn), lambda l, j: (l, 0, j)),
            pl.BlockSpec((1, 1, tn), lambda l, j: (l, 0, j)),
        ],
        out_specs=pl.BlockSpec((1, 8, tn), lambda l, j: (l, 0, j)),
        out_shape=jax.ShapeDtypeStruct((DEPTH, 8, 6 * D), F32),
        compiler_params=_cp(40),
        name="modulation",
    )(crow, w_mod, b_mod.reshape(DEPTH, 1, 6 * D))


def _mod_row(i):
    return jnp.where(i < NT_CTX, 0, 1 + (i - NT_CTX) // TILES_PER_LAT)


def _pos_block(i):
    return jnp.where(i < NT_CTX, 0, ID_ROWS // TM + (i - NT_CTX) % TILES_PER_LAT)


def _sum_slots(y_ref, gw_ref):
    gw = gw_ref[...]
    cols = [gw[:, kk:kk + 1] for kk in range(TOP_K)]
    parts = []
    for s in range(SUB):
        acc = cols[0] * y_ref[:, s, :]
        for kk in range(1, TOP_K):
            acc = acc + cols[kk] * y_ref[:, kk * SUB + s, :]
        parts.append(acc)
    return jnp.concatenate(parts, axis=-1)


def _proj_body(first, *refs):
    if first:
        (x_ref, mod_ref, g1_ref, win_ref, qg_ref, wq_ref, kvg_ref, wkv_ref, cq_ref, sq_ref, cs_ref, ss_ref,
         big_ref, seg_ref, ckv_ref, q_ref, k_ref, v_ref) = refs
        x = x_ref[...]
    else:
        (x_ref, moe_ref, gw_ref, modp_ref, mod_ref, g1_ref, win_ref, qg_ref, wq_ref, kvg_ref, wkv_ref, cq_ref, sq_ref,
         cs_ref, ss_ref, xo_ref, big_ref, seg_ref, ckv_ref, q_ref, k_ref, v_ref) = refs
        g2 = modp_ref[0][:, 5 * D:6 * D]
        x = x_ref[...] + g2 * _sum_slots(moe_ref, gw_ref)
        xo_ref[...] = x
    m = mod_ref[0]
    sh1, sc1 = m[:, 0:D], m[:, D:2 * D]
    h = _rms(x) * g1_ref[...] * (1.0 + sc1) + sh1
    p = _dot(h.astype(BF16), win_ref[...])
    big_ref[...] = p[:, :BIG]
    ckv = _rms(p[:, C_PKV:C_PKV + MLA_KV_LORA]) * kvg_ref[...]
    ckv_ref[...] = ckv
    pq = p[:, C_PQ:C_PQ + 256]
    qn = pq * lax.rsqrt(jnp.sum(pq * pq, axis=-1, keepdims=True) * (1.0 / MLA_Q_LORA) + EPS) * qg_ref[...]
    a = _dot(qn.astype(BF16), wq_ref[...])
    q_ref[...] = (a[:, :512] * cq_ref[...] + a[:, 512:] * sq_ref[...]).astype(BF16)
    seg = p[:, C_SEGA:C_SEGA + LANES] * cs_ref[...] + p[:, C_SEGB:C_SEGB + LANES] * ss_ref[...]
    seg_ref[...] = seg
    kv = _dot(jnp.concatenate([ckv, seg], axis=-1).astype(BF16), wkv_ref[...])
    k_ref[...] = kv[:, :512].astype(BF16)
    v_ref[...] = kv[:, 512:].astype(BF16)


def _proj(first, x, moe, gw, mods_prev, mods_l, lw, tabs):
    tok = lambda w: pl.BlockSpec((TM, w), lambda i: (i, 0))
    modspec = pl.BlockSpec((1, 1, 6 * D), lambda i: (_mod_row(i), 0, 0))
    pos = lambda w: pl.BlockSpec((TM, w), lambda i: (_pos_block(i), 0))
    in_specs = [tok(D)]
    args = [x]
    if not first:
        in_specs += [pl.BlockSpec((TM, TOP_K * SUB, LANES), lambda i: (i, 0, 0)), tok(LANES), modspec]
        args += [moe, gw, mods_prev]
    in_specs += [modspec, _const_spec((1, D)), _const_spec((D, NC_IN)), _const_spec((1, 256)),
                 _const_spec((256, 1024)), _const_spec((1, LANES)), _const_spec((256, 768)),
                 pos(512), pos(512), pos(LANES), pos(LANES)]
    args += [mods_l, lw["norm1_g"], lw["w_in"], lw["qg"], lw["wq"], lw["kvg"], lw["wkv"],
             tabs["cq"], tabs["sq"], tabs["cs"], tabs["ss"]]
    out_specs = [tok(BIG), tok(LANES), tok(LANES), tok(512), tok(512), tok(256)]
    out_shape = [jax.ShapeDtypeStruct((T_ALL, BIG), F32), jax.ShapeDtypeStruct((T_ALL, LANES), F32),
                 jax.ShapeDtypeStruct((T_ALL, LANES), F32), jax.ShapeDtypeStruct((T_ALL, 512), BF16),
                 jax.ShapeDtypeStruct((T_ALL, 512), BF16), jax.ShapeDtypeStruct((T_ALL, 256), BF16)]
    if not first:
        out_specs = [tok(D)] + out_specs
        out_shape = [jax.ShapeDtypeStruct((T_ALL, D), F32)] + out_shape
    outs = pl.pallas_call(
        functools.partial(_proj_body, first),
        grid=(NT_ALL,),
        in_specs=in_specs,
        out_specs=out_specs,
        out_shape=out_shape,
        compiler_params=_cp(48),
        name="proj",
    )(*args)
    if first:
        return (x,) + tuple(outs)
    return tuple(outs)


def _kvc_body(ckv_ref, kr_ref, wkv_ref, k_ref, v_ref):
    kv = _dot(jnp.concatenate([ckv_ref[...], kr_ref[...]], axis=-1).astype(BF16), wkv_ref[...])
    k_ref[...] = kv[:, :512].astype(BF16)
    v_ref[...] = kv[:, 512:].astype(BF16)


def _cache_kv(ckv_c, kr_c, wkv):
    n = B_LAT * PAST
    return pl.pallas_call(
        _kvc_body,
        grid=(B_LAT,),
        in_specs=[pl.BlockSpec((PAST, LANES), lambda b: (b, 0)), pl.BlockSpec((PAST, LANES), lambda b: (b, 0)),
                  _const_spec((256, 768))],
        out_specs=[pl.BlockSpec((PAST, 512), lambda b: (b, 0)), pl.BlockSpec((PAST, 256), lambda b: (b, 0))],
        out_shape=[jax.ShapeDtypeStruct((n, 512), BF16), jax.ShapeDtypeStruct((n, 256), BF16)],
        name="cache_kv",
    )(ckv_c, kr_c, wkv)


def _softmax_pv(scores, values):
    m = scores[0].max(axis=-1, keepdims=True)
    for s in scores[1:]:
        m = jnp.maximum(m, s.max(axis=-1, keepdims=True))
    den = 0.0
    acc = 0.0
    for s, v in zip(scores, values):
        e = jnp.exp(s - m)
        den = den + e.sum(axis=-1, keepdims=True)
        acc = acc + _dot(e.astype(BF16), v)
    return acc / den


def _lane_lo(shape):
    return lax.broadcasted_iota(jnp.int32, shape, len(shape) - 1) < 64


def _ctx_attn_body(q_ref, k_ref, v_ref, na_ref, oa_ref, ob_ref):
    lo = _lane_lo((S_CTX, LANES))
    for pp in range(2):
        vp = v_ref[:, pp * LANES:(pp + 1) * LANES]
        outs = []
        for h in (2 * pp, 2 * pp + 1):
            s = _dot_nt(q_ref[:, h * LANES:(h + 1) * LANES], k_ref[:, h * LANES:(h + 1) * LANES]) * MLA_SCALE
            outs.append(_softmax_pv([s], [vp]))
        oa_ref[:, pp * LANES:(pp + 1) * LANES] = jnp.where(lo, outs[0], outs[1])
    for pp in range(2):
        qp = na_ref[:, C_NAQ + pp * LANES:C_NAQ + (pp + 1) * LANES]
        kp = na_ref[:, C_NAK + pp * LANES:C_NAK + (pp + 1) * LANES].astype(BF16)
        vp = na_ref[:, C_NAV + pp * LANES:C_NAV + (pp + 1) * LANES].astype(BF16)
        outs = []
        for sel in (lo, ~lo):
            qh = jnp.where(sel, qp, 0.0).astype(BF16)
            outs.append(_softmax_pv([_dot_nt(qh, kp) * NA_SCALE], [vp]))
        ob_ref[:, pp * LANES:(pp + 1) * LANES] = jnp.where(lo, outs[0], outs[1])


def _ctx_attn(q, k, v, big):
    blk = lambda w: pl.BlockSpec((S_CTX, w), lambda b: (b, 0))
    return pl.pallas_call(
        _ctx_attn_body,
        grid=(B_CTX,),
        in_specs=[blk(512), blk(512), blk(256), blk(768)],
        out_specs=[blk(256), blk(256)],
        out_shape=[jax.ShapeDtypeStruct((T_CTX, 256), F32), jax.ShapeDtypeStruct((T_CTX, 256), F32)],
        name="ctx_attn",
    )(q, k, v, big)


TQ = 256


def _lat_mla_body(q_ref, kl_ref, vl_ref, kc_ref, vc_ref, o_ref):
    lo = _lane_lo((TQ, LANES))
    for pp in range(2):
        vlp = vl_ref[:, pp * LANES:(pp + 1) * LANES]
        vcp = vc_ref[:, pp * LANES:(pp + 1) * LANES]
        outs = []
        for h in (2 * pp, 2 * pp + 1):
            qh = q_ref[:, h * LANES:(h + 1) * LANES]
            s_l = _dot_nt(qh, kl_ref[:, h * LANES:(h + 1) * LANES]) * MLA_SCALE
            s_c = _dot_nt(qh, kc_ref[:, h * LANES:(h + 1) * LANES]) * MLA_SCALE
            outs.append(_softmax_pv([s_l, s_c], [vlp, vcp]))
        o_ref[:, pp * LANES:(pp + 1) * LANES] = jnp.where(lo, outs[0], outs[1])


def _lat_mla(q, k, v, kc, vc):
    nq = S_LAT // TQ
    qrow = lambda b, t: (T_CTX // TQ + b * nq + t, 0)
    lat = lambda b, t: (T_CTX // S_LAT + b, 0)
    return pl.pallas_call(
        _lat_mla_body,
        grid=(B_LAT, nq),
        in_specs=[pl.BlockSpec((TQ, 512), qrow), pl.BlockSpec((S_LAT, 512), lat), pl.BlockSpec((S_LAT, 256), lat),
                  pl.BlockSpec((PAST, 512), lambda b, t: (b, 0)), pl.BlockSpec((PAST, 256), lambda b, t: (b, 0))],
        out_specs=pl.BlockSpec((TQ, 256), lambda b, t: (b * nq + t, 0)),
        out_shape=jax.ShapeDtypeStruct((T_LAT, 256), F32),
        compiler_params=_cp(48),
        name="lat_mla",
    )(q, k, v, kc, vc)


NK_LOC = NA_WIN_H * GRID_W


def _na_row_start(r):
    return jnp.clip(r - NA_WIN_H // 2, 0, ROWS - NA_WIN_H)


def _lat_na_body(q_ref, k_ref, v_ref, kc_ref, vc_ref, bias_ref, o_ref):
    r = pl.program_id(1)
    k0 = pl.multiple_of(_na_row_start(r) * GRID_W, GRID_W)
    lo = _lane_lo((GRID_W, LANES))
    for pp in range(2):
        cols = slice(pp * LANES, (pp + 1) * LANES)
        qp = q_ref[:, cols]
        kl = k_ref[pl.ds(k0, NK_LOC), cols].astype(BF16)
        vl = v_ref[pl.ds(k0, NK_LOC), cols].astype(BF16)
        kc = kc_ref[:, cols].astype(BF16)
        vc = vc_ref[:, cols].astype(BF16)
        outs = []
        for j, sel in enumerate((lo, ~lo)):
            qh = jnp.where(sel, qp, 0.0).astype(BF16)
            s_loc = _dot_nt(qh, kl) * NA_SCALE + bias_ref[0, 2 * pp + j]
            s_ctx = _dot_nt(qh, kc) * NA_SCALE
            outs.append(_softmax_pv([s_loc, s_ctx], [vl, vc]))
        o_ref[:, cols] = jnp.where(lo, outs[0], outs[1])


def _lat_na(big, kc, vc, bias):
    qrow = lambda b, r: (T_CTX // GRID_W + b * ROWS + r, 0)
    boff = lambda b, r: (_na_row_start(r) - r + NA_WIN_H - 1, 0, 0, 0)
    lat = lambda c: (lambda b, r: (T_CTX // S_LAT + b, c))
    return pl.pallas_call(
        _lat_na_body,
        grid=(B_LAT, ROWS),
        in_specs=[pl.BlockSpec((GRID_W, 256), qrow), pl.BlockSpec((S_LAT, 256), lat(1)),
                  pl.BlockSpec((S_LAT, 256), lat(2)),
                  pl.BlockSpec((PAST, 256), lambda b, r: (b, 0)), pl.BlockSpec((PAST, 256), lambda b, r: (b, 0)),
                  pl.BlockSpec((1, NA_HEADS, GRID_W, NK_LOC), boff)],
        out_specs=pl.BlockSpec((GRID_W, 256), lambda b, r: (b * ROWS + r, 0)),
        out_shape=jax.ShapeDtypeStruct((T_LAT, 256), F32),
        compiler_params=_cp(48),
        name="lat_na",
    )(big, big, big, kc, vc, bias)


def _log_sigmoid(x):
    return jnp.minimum(x, 0.0) - jnp.log1p(jnp.exp(-jnp.abs(x)))


def _gla_chunk(rev, q, k, v, la, st):
    C, SB = GLA_CHUNK, GLA_SUB
    ri = lax.broadcasted_iota(jnp.int32, (C, C), 0)
    ci = lax.broadcasted_iota(jnp.int32, (C, C), 1)
    tri = ((ci >= ri) if rev else (ci <= ri)).astype(F32)
    cb = _dot(tri, la, precision=HIGHEST)
    cl = cb[0:1] if rev else cb[C - 1:C]
    o = _dot_nt((q * jnp.exp(cb)).astype(BF16), st.astype(BF16))
    kd = (k * jnp.exp(cl - cb)).astype(BF16)
    bd = (lax.broadcasted_iota(jnp.int32, (256, LANES), 0) // GLA_DV
          == lax.broadcasted_iota(jnp.int32, (256, LANES), 1) // GLA_DK)
    st_new = st * jnp.exp(cl) + jnp.where(bd, _dot_tn(v.astype(BF16), kd), 0.0)

    vb = v.astype(BF16)
    lane_h = lax.broadcasted_iota(jnp.int32, (SB, LANES), 1) // GLA_DK
    col_h = lax.broadcasted_iota(jnp.int32, (SB, 256), 1) // GLA_DV
    expand = (lax.broadcasted_iota(jnp.int32, (LANES, 256), 0) // GLA_DK
              == lax.broadcasted_iota(jnp.int32, (LANES, 256), 1) // GLA_DV).astype(BF16)
    row = lax.broadcasted_iota(jnp.int32, (C, LANES), 0)
    pj = lax.broadcasted_iota(jnp.int32, (SB * SB, LANES), 0) // SB
    pi = lax.broadcasted_iota(jnp.int32, (SB * SB, LANES), 0) % SB
    pmask = (pi <= pj) if rev else (pi >= pj)
    nsub = C // SB
    parts = []
    for i in range(nsub):
        sl = slice(i * SB, (i + 1) * SB)
        qi, ki, cbi, vi = q[sl], k[sl], cb[sl], v[sl]
        acc = o[sl]
        has_off = (i < nsub - 1) if rev else (i > 0)
        if has_off:
            if rev:
                ref = cb[(i + 1) * SB:(i + 1) * SB + 1]
                outside = row >= (i + 1) * SB
            else:
                ref = cb[i * SB - 1:i * SB]
                outside = row < i * SB
            a = qi * jnp.exp(cbi - ref)
            bm = jnp.where(outside, k * jnp.exp(jnp.minimum(ref - cb, 0.0)), 0.0).astype(BF16)
            a4 = jnp.concatenate([jnp.where(lane_h == h, a, 0.0) for h in range(GLA_HEADS)], axis=0)
            att = _dot_nt(a4.astype(BF16), bm)
            pv = _dot(att.astype(BF16), vb)
            for h in range(GLA_HEADS):
                acc = acc + jnp.where(col_h == h, pv[h * SB:(h + 1) * SB], 0.0)
        kj = jnp.broadcast_to(ki[:, None, :], (SB, SB, LANES)).reshape(SB * SB, LANES)
        cbj = jnp.broadcast_to(cbi[:, None, :], (SB, SB, LANES)).reshape(SB * SB, LANES)
        qq = jnp.broadcast_to(qi[None, :, :], (SB, SB, LANES)).reshape(SB * SB, LANES)
        cbq = jnp.broadcast_to(cbi[None, :, :], (SB, SB, LANES)).reshape(SB * SB, LANES)
        x = qq * kj * jnp.where(pmask, jnp.exp(jnp.where(pmask, cbq - cbj, 0.0)), 0.0)
        y = _dot(x.astype(BF16), expand).reshape(SB, SB, 256)
        acc = acc + jnp.sum(y * vi[:, None, :], axis=0)
        parts.append(acc)
    return jnp.concatenate(parts, axis=0), st_new


def _gla_body(S, has_s0, *refs):
    if has_s0:
        (gq_ref, gk_ref, gv_ref, gg_ref, seg_ref, wg_ref, bg_ref, gn_ref, s0_ref,
         o_ref, sfin_ref, la_scr, of_scr, ob_scr, st_scr) = refs
    else:
        (gq_ref, gk_ref, gv_ref, gg_ref, seg_ref, wg_ref, bg_ref, gn_ref,
         o_ref, sfin_ref, la_scr, of_scr, ob_scr, st_scr) = refs
    nc = S // GLA_CHUNK
    seg = seg_ref[...].astype(BF16)
    for d in range(2):
        la_scr[d] = _log_sigmoid(_dot(seg, wg_ref[d]) + bg_ref[d]) * (1.0 / GLA_NORM)
        if has_s0:
            st_scr[d] = s0_ref[0, d]
        else:
            st_scr[d] = jnp.zeros((256, LANES), F32)

    def body(c, carry):
        for d, out_scr in ((0, of_scr), (1, ob_scr)):
            r0 = pl.multiple_of((c if d == 0 else nc - 1 - c) * GLA_CHUNK, GLA_CHUNK)
            rows = pl.ds(r0, GLA_CHUNK)
            q = gq_ref[rows, :] * (GLA_DK ** -0.5)
            o, st = _gla_chunk(d == 1, q, gk_ref[rows, :], gv_ref[rows, :], la_scr[d, rows, :], st_scr[d])
            out_scr[rows, :] = o
            st_scr[d] = st
        return carry

    lax.fori_loop(0, nc, body, 0)
    sfin_ref[0] = st_scr[...]
    o = of_scr[...] + ob_scr[...]
    ones = (lax.broadcasted_iota(jnp.int32, (256, 256), 0) // GLA_DV
            == lax.broadcasted_iota(jnp.int32, (256, 256), 1) // GLA_DV).astype(F32)
    ms = _dot(o * o, ones, precision=HIGHEST) * (1.0 / GLA_DV)
    g = gg_ref[...]
    o_ref[...] = o * lax.rsqrt(ms + EPS) * gn_ref[...] * (g * jax.nn.sigmoid(g))


def _gla(big, seg, lw, s0):
    has_s0 = s0 is not None
    S, nb, base = (S_LAT, B_LAT, T_CTX // S_LAT) if has_s0 else (S_CTX, B_CTX, 0)
    blk = lambda w, c: pl.BlockSpec((S, w), lambda b: (base + b, c))
    in_specs = [blk(LANES, C_GQ // LANES), blk(LANES, C_GK // LANES), blk(256, C_GV // 256), blk(256, C_GG // 256),
                blk(LANES, 0), _const_spec((2, LANES, LANES)), _const_spec((2, 1, LANES)), _const_spec((1, 256))]
    args = [big, big, big, big, seg, lw["wg"], lw["bg"], lw["gla_norm_g"]]
    if has_s0:
        in_specs += [pl.BlockSpec((1, 2, 256, LANES), lambda b: (b, 0, 0, 0))]
        args += [s0]
    return pl.pallas_call(
        functools.partial(_gla_body, S, has_s0),
        grid=(nb,),
        in_specs=in_specs,
        out_specs=[pl.BlockSpec((S, 256), lambda b: (b, 0)), pl.BlockSpec((1, 2, 256, LANES), lambda b: (b, 0, 0, 0))],
        out_shape=[jax.ShapeDtypeStruct((nb * S, 256), F32), jax.ShapeDtypeStruct((nb, 2, 256, LANES), F32)],
        scratch_shapes=[pltpu.VMEM((2, S, LANES), F32), pltpu.VMEM((S, 256), F32), pltpu.VMEM((S, 256), F32),
                        pltpu.VMEM((2, 256, LANES), F32)],
        compiler_params=_cp(48),
        name="gla_lat" if has_s0 else "gla_ctx",
    )(*args)


def _softplus(x):
    return jnp.maximum(x, 0.0) + jnp.log1p(jnp.exp(-jnp.abs(x)))


def _lru_body(S, has_h0, *refs):
    if has_h0:
        (lx_ref, lg_ref, cw_ref, cb_ref, wg_ref, bgate_ref, lam_ref, h0_ref,
         o_ref, hfin_ref, a_scr, b_scr, hf_scr, hb_scr) = refs
    else:
        (lx_ref, lg_ref, cw_ref, cb_ref, wg_ref, bgate_ref, lam_ref,
         o_ref, hfin_ref, a_scr, b_scr, hf_scr, hb_scr) = refs
    x = lx_ref[...]
    row = lax.broadcasted_iota(jnp.int32, (S, LRU_W), 0)
    xm1 = jnp.where(row >= 1, pltpu.roll(x, 1, 0), 0.0)
    xp1 = jnp.where(row < S - 1, pltpu.roll(x, S - 1, 0), 0.0)
    xp2 = jnp.where(row < S - 2, pltpu.roll(x, S - 2, 0), 0.0)
    xc = cb_ref[...] + xm1 * cw_ref[0:1] + x * cw_ref[1:2] + xp1 * cw_ref[2:3] + xp2 * cw_ref[3:4]
    gates = _dot(xc.astype(BF16), wg_ref[...]) + bgate_ref[...]
    for d in range(2):
        r = jax.nn.sigmoid(gates[:, d * LRU_W:(d + 1) * LRU_W])
        ig = jax.nn.sigmoid(gates[:, (2 + d) * LRU_W:(3 + d) * LRU_W])
        log_a = -LRU_C * r * _softplus(-lam_ref[d])
        a_scr[d] = jnp.exp(log_a)
        b_scr[d] = jnp.sqrt(1.0 - jnp.exp(2.0 * log_a)) * (ig * xc)
    if has_h0:
        h_init = (h0_ref[0, 0], h0_ref[0, 1])
    else:
        h_init = (jnp.zeros((1, LRU_W), F32), jnp.zeros((1, LRU_W), F32))

    def body(i, carry):
        hf, hb = carry
        r0 = pl.multiple_of(i * 8, 8)
        rb = pl.multiple_of(S - 8 - i * 8, 8)
        af, bf = a_scr[0, pl.ds(r0, 8), :], b_scr[0, pl.ds(r0, 8), :]
        ab, bb = a_scr[1, pl.ds(rb, 8), :], b_scr[1, pl.ds(rb, 8), :]
        fs, bs = [], [None] * 8
        for t in range(8):
            hf = af[t:t + 1] * hf + bf[t:t + 1]
            fs.append(hf)
            u = 7 - t
            hb = ab[u:u + 1] * hb + bb[u:u + 1]
            bs[u] = hb
        hf_scr[pl.ds(r0, 8), :] = jnp.concatenate(fs, axis=0)
        hb_scr[pl.ds(rb, 8), :] = jnp.concatenate(bs, axis=0)
        return hf, hb

    hf, hb = lax.fori_loop(0, S // 8, body, h_init)
    hfin_ref[0, 0] = hf
    hfin_ref[0, 1] = hb
    g = lg_ref[...]
    gelu = g * (0.5 * (1.0 + jnp.tanh(np.sqrt(2.0 / np.pi).astype(np.float32) * (g + 0.044715 * (g * g * g)))))
    o_ref[...] = (hf_scr[...] + hb_scr[...]) * gelu


def _lru(big, lw, h0):
    has_h0 = h0 is not None
    S, nb, base = (S_LAT, B_LAT, T_CTX // S_LAT) if has_h0 else (S_CTX, B_CTX, 0)
    blk = lambda c: pl.BlockSpec((S, LRU_W), lambda b: (base + b, c))
    in_specs = [blk(C_LX // LRU_W), blk(C_LG // LRU_W), _const_spec((4, LRU_W)), _const_spec((1, LRU_W)),
                _const_spec((LRU_W, 4 * LRU_W)), _const_spec((1, 4 * LRU_W)), _const_spec((2, 1, LRU_W))]
    args = [big, big, lw["lru_conv_w"], lw["lru_conv_b"], lw["lru_wg"], lw["lru_bg"], lw["lru_lambda"]]
    if has_h0:
        in_specs += [pl.BlockSpec((1, 2, 1, LRU_W), lambda b: (b, 0, 0, 0))]
        args += [h0]
    return pl.pallas_call(
        functools.partial(_lru_body, S, has_h0),
        grid=(nb,),
        in_specs=in_specs,
        out_specs=[pl.BlockSpec((S, LRU_W), lambda b: (b, 0)),
                   pl.BlockSpec((1, 2, 1, LRU_W), lambda b: (b, 0, 0, 0))],
        out_shape=[jax.ShapeDtypeStruct((nb * S, LRU_W), F32), jax.ShapeDtypeStruct((nb, 2, 1, LRU_W), F32)],
        scratch_shapes=[pltpu.VMEM((2, S, LRU_W), F32), pltpu.VMEM((2, S, LRU_W), F32),
                        pltpu.VMEM((S, LRU_W), F32), pltpu.VMEM((S, LRU_W), F32)],
        compiler_params=_cp(48),
        name="lru_lat" if has_h0 else "lru_ctx",
    )(*args)


def _merge_body(x_ref, *refs):
    (mod_ref, wout_ref, g2_ref, rwh_ref, rwl_ref, rb_ref, x1_ref, h2_ref, idx_ref, gw_ref) = refs[8:]
    is_ctx = pl.program_id(0) < NT_CTX
    mix = 0.0
    for j in range(4):
        o = jnp.where(is_ctx, refs[2 * j][...], refs[2 * j + 1][...])
        mix = mix + _dot(o.astype(BF16), wout_ref[j * 256:(j + 1) * 256, :])
    m = mod_ref[0]
    g1, sh2, sc2 = m[:, 2 * D:3 * D], m[:, 3 * D:4 * D], m[:, 4 * D:5 * D]
    x1 = x_ref[...] + g1 * mix
    x1_ref[...] = x1
    h2 = _rms(x1) * g2_ref[...] * (1.0 + sc2) + sh2
    for s in range(SUB):
        h2_ref[:, s, :] = h2[:, s * LANES:(s + 1) * LANES]
    hi = h2.astype(BF16)
    lo = (h2 - hi.astype(F32)).astype(BF16)
    logits = _dot(hi, rwh_ref[...]) + _dot(hi, rwl_ref[...]) + _dot(lo, rwh_ref[...]) + rb_ref[...]
    lane = lax.broadcasted_iota(jnp.int32, (TM, LANES), 1)
    idx_out = jnp.zeros((TM, LANES), jnp.int32)
    val_out = jnp.zeros((TM, LANES), F32)
    top = None
    den = 0.0
    for kk in range(TOP_K):
        mval = logits.max(axis=-1, keepdims=True)
        midx = jnp.where(logits == mval, lane, LANES).min(axis=-1, keepdims=True)
        if kk == 0:
            top = mval
        e = jnp.exp(mval - top)
        den = den + e
        idx_out = jnp.where(lane == kk, midx, idx_out)
        val_out = jnp.where(lane == kk, e, val_out)
        logits = jnp.where(lane == midx, -jnp.inf, logits)
    idx_ref[...] = idx_out
    gw_ref[...] = val_out / den


def _merge(x, mix_parts, mods_l, lw):
    tok = lambda w: pl.BlockSpec((TM, w), lambda i: (i, 0))
    ctx = pl.BlockSpec((TM, 256), lambda i: (jnp.minimum(i, NT_CTX - 1), 0))
    lat = pl.BlockSpec((TM, 256), lambda i: (jnp.maximum(i - NT_CTX, 0), 0))
    return pl.pallas_call(
        _merge_body,
        grid=(NT_ALL,),
        in_specs=[tok(D)] + [ctx, lat] * 4 + [
                  pl.BlockSpec((1, 1, 6 * D), lambda i: (_mod_row(i), 0, 0)),
                  _const_spec((D, D)), _const_spec((1, D)), _const_spec((D, LANES)), _const_spec((D, LANES)),
                  _const_spec((1, LANES))],
        out_specs=[tok(D), pl.BlockSpec((TM, SUB, LANES), lambda i: (i, 0, 0)), tok(LANES), tok(LANES)],
        out_shape=[jax.ShapeDtypeStruct((T_ALL, D), F32), jax.ShapeDtypeStruct((T_ALL, SUB, LANES), F32),
                   jax.ShapeDtypeStruct((T_ALL, LANES), jnp.int32), jax.ShapeDtypeStruct((T_ALL, LANES), F32)],
        compiler_params=_cp(40),
        name="merge",
    )(x, *mix_parts, mods_l, lw["w_out"], lw["norm2_g"], lw["rw_hi"], lw["rw_lo"], lw["rb"])


def _moe_body(sn_ref, te_ref, nv_ref, tb_ref, h2_hbm, wgu_ref, bgu_ref, wd_ref, bd_ref,
              y_hbm, xbuf, obuf, wgu_bf, wd_bf, gsem, ssem):
    i = pl.program_id(0)
    slot = i % 2
    cur = jnp.minimum(i, NT_MOE - 1)
    valid = (i < NT_MOE) & (nv_ref[cur] > 0)
    prv = jnp.maximum(i - 1, 0)
    nv_prev = jnp.where(i >= 1, nv_ref[prv], 0)
    base_prev = tb_ref[prv]
    base_next = tb_ref[jnp.minimum(i + 1, NT_MOE - 1)]

    def gather_row(base, sl, r):
        tok = lax.shift_right_logical(sn_ref[base + r], TOP_K_SHIFT)
        pltpu.make_async_copy(h2_hbm.at[tok], xbuf.at[sl, r], gsem.at[sl]).start()

    def scatter_row(sl, r):
        dst = jnp.where(r < nv_prev, sn_ref[base_prev + r], N_ROWS + sl * TM_MOE + r)
        pltpu.make_async_copy(obuf.at[sl, r], y_hbm.at[dst], ssem.at[sl]).start()

    def for_other_slot(fn, lo=0, hi=TM_MOE):
        for other in range(2):
            @pl.when(slot == 1 - other)
            def _():
                lax.fori_loop(lo, hi, lambda r, c: fn(other, r) or c, 0, unroll=8)

    def gather_done(sl):
        pltpu.make_async_copy(h2_hbm.at[pl.ds(0, TM_MOE)], xbuf.at[sl], gsem.at[sl]).wait()

    def scatter_done(sl):
        pltpu.make_async_copy(obuf.at[sl], y_hbm.at[pl.ds(0, TM_MOE)], ssem.at[sl]).wait()

    @pl.when(i == 0)
    def _():
        obuf[1] = jnp.zeros((TM_MOE, SUB, LANES), F32)
        cp = pltpu.make_async_copy(obuf.at[1], y_hbm.at[pl.ds(N_ROWS, TM_MOE)], ssem.at[1])
        cp.start()
        cp.wait()

        def body(r, c):
            gather_row(tb_ref[0], 0, r)
            return c
        lax.fori_loop(0, TM_MOE, body, 0, unroll=8)

    e = te_ref[cur]
    prev_e = te_ref[prv]

    @pl.when(valid & ((i == 0) | (e != prev_e)))
    def _():
        wgu_bf[...] = wgu_ref[0].astype(BF16)
        wd_bf[...] = wd_ref[0].astype(BF16)

    @pl.when(valid)
    def _():
        def both(other, r):
            gather_row(base_next, other, r)
            scatter_row(other, r)
        n_chunk = D_EXP // EXP_CHUNK
        part = TM_MOE // n_chunk
        for_other_slot(both, 0, part)
        gather_done(slot)
        x = jnp.concatenate([xbuf[slot, :, s, :] for s in range(SUB)], axis=-1).astype(BF16)
        acc = jnp.zeros((TM_MOE, D), F32)
        for j in range(n_chunk):
            c0 = j * EXP_CHUNK
            if j > 0:
                for_other_slot(both, j * part, (j + 1) * part)
            xg = _dot(x, wgu_bf[:, c0:c0 + EXP_CHUNK]) + bgu_ref[0][:, c0:c0 + EXP_CHUNK]
            xl = _dot(x, wgu_bf[:, D_EXP + c0:D_EXP + c0 + EXP_CHUNK]) + bgu_ref[0][:, D_EXP + c0:D_EXP + c0 + EXP_CHUNK]
            xg = jnp.minimum(xg, SWIGLU_LIMIT)
            xl = jnp.clip(xl, -SWIGLU_LIMIT, SWIGLU_LIMIT)
            act = xg * jax.nn.sigmoid(SWIGLU_ALPHA * xg) * (xl + 1.0)
            acc = acc + _dot(act.astype(BF16), wd_bf[c0:c0 + EXP_CHUNK, :])
        out = acc + bd_ref[0]
        for s in range(SUB):
            obuf[slot, :, s, :] = out[:, s * LANES:(s + 1) * LANES]
        scatter_done(1 - slot)

    @pl.when(jnp.logical_not(valid) & (nv_prev > 0))
    def _():
        gather_done(slot)
        for_other_slot(scatter_row)
        scatter_done(1 - slot)


def _moe(sorted_n, tile_exp, tile_nvalid, tile_base, h2, lw):
    wmap = lambda i, sn, te, nv, tb: (te[jnp.minimum(i, NT_MOE - 1)], 0, 0)
    grid_spec = pltpu.PrefetchScalarGridSpec(
        num_scalar_prefetch=4,
        grid=(NT_MOE + 1,),
        in_specs=[
            pl.BlockSpec(memory_space=pl.ANY),
            pl.BlockSpec((1, D, 2 * D_EXP), wmap),
            pl.BlockSpec((1, 1, 2 * D_EXP), wmap),
            pl.BlockSpec((1, D_EXP, D), wmap),
            pl.BlockSpec((1, 1, D), wmap),
        ],
        out_specs=pl.BlockSpec(memory_space=pl.ANY),
        scratch_shapes=[pltpu.VMEM((2, TM_MOE, SUB, LANES), F32), pltpu.VMEM((2, TM_MOE, SUB, LANES), F32),
                        pltpu.VMEM((D, 2 * D_EXP), BF16), pltpu.VMEM((D_EXP, D), BF16),
                        pltpu.SemaphoreType.DMA((2,)), pltpu.SemaphoreType.DMA((2,))],
    )
    return pl.pallas_call(
        _moe_body,
        grid_spec=grid_spec,
        out_shape=jax.ShapeDtypeStruct((N_ROWS + N_DUMP, SUB, LANES), F32),
        compiler_params=pltpu.CompilerParams(vmem_limit_bytes=56 * 1024 * 1024, dimension_semantics=("arbitrary",),
                                             disable_bounds_checks=True),
        name="moe",
    )(sorted_n, tile_exp, tile_nvalid, tile_base, h2,
      lw["exp_w_gu"], lw["exp_b_gu"], lw["exp_w_down"], lw["exp_b_down"])


def _route(idx):
    flat = idx.reshape(-1)
    ids = jnp.arange(N_ROWS, dtype=jnp.int32)
    sorted_n = jnp.concatenate([jnp.sort(flat * N_ROWS + ids) % N_ROWS, jnp.zeros((TM_MOE,), jnp.int32)])
    experts = jnp.arange(N_EXP, dtype=jnp.int32)
    counts = jnp.sum((flat[None, :] == experts[:, None]).astype(jnp.int32), axis=1)
    padded = ((counts + TM_MOE - 1) // TM_MOE) * TM_MOE
    pad_ends = jnp.cumsum(padded)
    pad_starts = pad_ends - padded
    unp_starts = jnp.cumsum(counts) - counts
    tile_start = jnp.arange(NT_MOE, dtype=jnp.int32) * TM_MOE
    tile_exp = jnp.minimum(jnp.sum((pad_ends[None, :] <= tile_start[:, None]).astype(jnp.int32), axis=1), N_EXP - 1)
    onehot = (tile_exp[:, None] == experts[None, :]).astype(jnp.int32)
    pick = lambda v: jnp.sum(onehot * v[None, :], axis=1)
    offset = tile_start - pick(pad_starts)
    tile_nvalid = jnp.where(tile_start < pad_ends[-1], jnp.clip(pick(counts) - offset, 0, TM_MOE), 0)
    tile_base = jnp.clip(pick(unp_starts) + offset, 0, N_ROWS - 1)
    return sorted_n, tile_exp, tile_nvalid.astype(jnp.int32), tile_base.astype(jnp.int32)


def _final_body(x_ref, moe_ref, gw_ref, mod_ref, g_ref, o_ref):
    g2 = mod_ref[0][:, 5 * D:6 * D]
    o_ref[...] = _rms(x_ref[...] + g2 * _sum_slots(moe_ref, gw_ref)) * g_ref[...]


def _final(x1, moe, gw, mods_l, g, tile0, ntiles):
    tok = pl.BlockSpec((TM, D), lambda i: (tile0 + i, 0))
    return pl.pallas_call(
        _final_body,
        grid=(ntiles,),
        in_specs=[tok, pl.BlockSpec((TM, TOP_K * SUB, LANES), lambda i: (tile0 + i, 0, 0)),
                  pl.BlockSpec((TM, LANES), lambda i: (tile0 + i, 0)),
                  pl.BlockSpec((1, 1, 6 * D), lambda i: (_mod_row(tile0 + i), 0, 0)), _const_spec((1, D))],
        out_specs=pl.BlockSpec((TM, D), lambda i: (i, 0)),
        out_shape=jax.ShapeDtypeStruct((ntiles * TM, D), F32),
        name="final_norm",
    )(x1, moe, gw, mods_l, g)


def _rope_perm(w):
    parts = []
    for a in range(2):
        x1, x2 = w[..., a * 16:a * 16 + 8], w[..., a * 16 + 8:a * 16 + 16]
        parts += [-x2, x1]
    return jnp.concatenate(parts, axis=-1)


def _rope_tables():
    t = np.arange(S_LAT)
    freqs = ROPE_BASE ** (-np.arange(8, dtype=np.float64) / 8)
    cs, sn = [], []
    for pos in (t // GRID_W, t % GRID_W):
        ang = pos.astype(np.float64)[:, None] * freqs[None, :]
        cs += [np.cos(ang), np.cos(ang)]
        sn += [np.sin(ang), np.sin(ang)]
    cos = np.concatenate([np.ones((ID_ROWS, 32), np.float32), np.concatenate(cs, -1).astype(np.float32)], 0)
    sin = np.concatenate([np.zeros((ID_ROWS, 32), np.float32), np.concatenate(sn, -1).astype(np.float32)], 0)
    n = cos.shape[0]
    one, zero = np.ones((n, 1), np.float32), np.zeros((n, 1), np.float32)
    cq_h = np.concatenate([np.repeat(one, 64, 1), cos, np.repeat(zero, 32, 1)], 1)
    sq_h = np.concatenate([np.repeat(zero, 64, 1), sin, np.repeat(zero, 32, 1)], 1)
    cseg = np.concatenate([cos, np.repeat(one, 96, 1)], 1)
    sseg = np.concatenate([sin, np.repeat(zero, 96, 1)], 1)
    return dict(cq=jnp.asarray(np.tile(cq_h, (1, 4))), sq=jnp.asarray(np.tile(sq_h, (1, 4))),
                cs=jnp.asarray(cseg), ss=jnp.asarray(sseg))


def _na_bias_table(rpb):
    qc = np.arange(GRID_W)[:, None]
    kc = np.arange(GRID_W)[None, :]
    cs = np.clip(qc - NA_WIN_W // 2, 0, GRID_W - NA_WIN_W)
    valid = (kc >= cs) & (kc < cs + NA_WIN_W)
    pad = GRID_W - NA_WIN_W
    ext = jnp.pad(rpb, ((0, 0), (0, 0), (pad, pad)))
    toe = jnp.stack([ext[:, :, GRID_W - 1 - q:2 * GRID_W - 1 - q] for q in range(GRID_W)], axis=2)
    toe = jnp.where(valid[None, None], toe, NEG)
    tab = jnp.stack([toe[:, o:o + NA_WIN_H] for o in range(NA_WIN_H)], axis=0)
    return tab.transpose(0, 1, 3, 2, 4).reshape(NA_WIN_H, NA_HEADS, GRID_W, NK_LOC).astype(F32)


def _layer_weights(l, p):
    w_in = p["w_in"][l]
    z = lambda n: jnp.zeros((D, n), F32)
    kr = w_in[:, 320:352]
    w_perm = jnp.concatenate([
        w_in[:, 352:1120], w_in[:, 1120:1248], w_in[:, 1248:1376], w_in[:, 1376:1632], w_in[:, 1632:1888],
        w_in[:, 1920:2176], w_in[:, 2176:2432], w_in[:, 192:320], w_in[:, 0:192], z(64),
        kr, w_in[:, 1888:1920], z(64), _rope_perm(kr), z(96)], axis=1).astype(BF16)
    wuq = p["mla_w_uq"][l].reshape(MLA_Q_LORA, MLA_HEADS, MLA_NOPE + MLA_ROPE)
    zq = jnp.zeros((MLA_Q_LORA, MLA_HEADS, 32), F32)
    wq1 = jnp.concatenate([wuq, zq], axis=-1).reshape(MLA_Q_LORA, 512)
    wq2 = jnp.concatenate([jnp.zeros((MLA_Q_LORA, MLA_HEADS, MLA_NOPE), F32), _rope_perm(wuq[..., MLA_NOPE:]), zq],
                          axis=-1).reshape(MLA_Q_LORA, 512)
    wq = jnp.pad(jnp.concatenate([wq1, wq2], axis=1), ((0, 256 - MLA_Q_LORA), (0, 0))).astype(BF16)
    qg = jnp.pad(p["mla_q_norm_g"][l], (0, 256 - MLA_Q_LORA)).reshape(1, 256)
    wukv = p["mla_w_ukv"][l].reshape(MLA_KV_LORA, MLA_HEADS, MLA_NOPE + MLA_V)
    wk = jnp.concatenate([wukv[..., :MLA_NOPE], jnp.zeros((MLA_KV_LORA, MLA_HEADS, 64), F32)], -1).reshape(MLA_KV_LORA, 512)
    wv = wukv[..., MLA_NOPE:].reshape(MLA_KV_LORA, 256)
    eye = jnp.eye(MLA_ROPE, dtype=F32)[:, None, :]
    ek = jnp.concatenate([jnp.zeros((MLA_ROPE, MLA_HEADS, MLA_NOPE), F32),
                          jnp.broadcast_to(eye, (MLA_ROPE, MLA_HEADS, MLA_ROPE)),
                          jnp.zeros((MLA_ROPE, MLA_HEADS, 32), F32)], -1).reshape(MLA_ROPE, 512)
    wkv = jnp.concatenate([
        jnp.concatenate([wk, wv], axis=1),
        jnp.concatenate([ek, jnp.zeros((MLA_ROPE, 256), F32)], axis=1),
        jnp.zeros((LANES - MLA_ROPE, 768), F32)], axis=0).astype(BF16)
    wg = jnp.zeros((2, LANES, LANES), F32)
    for d in range(2):
        wg = wg.at[d, SEG_GA + d * GLA_RANK:SEG_GA + (d + 1) * GLA_RANK].set(p["gla_w_gate"][l, d])
    def bdiag(w):
        out = jnp.zeros((LRU_W, LRU_W), F32)
        nb = LRU_W // LRU_BLOCKS
        for n in range(LRU_BLOCKS):
            out = out.at[n * nb:(n + 1) * nb, n * nb:(n + 1) * nb].set(w[n])
        return out
    lru_wg = jnp.concatenate([bdiag(p["lru_w_a"][l, 0]), bdiag(p["lru_w_a"][l, 1]),
                              bdiag(p["lru_w_x"][l, 0]), bdiag(p["lru_w_x"][l, 1])], axis=1).astype(BF16)
    lru_bg = jnp.concatenate([p["lru_b_a"][l, 0], p["lru_b_a"][l, 1], p["lru_b_x"][l, 0], p["lru_b_x"][l, 1]]).reshape(1, 4 * LRU_W)
    rw = jnp.pad(p["router_w"][l], ((0, 0), (0, LANES - N_EXP)))
    rw_hi = rw.astype(BF16)
    rw_lo = (rw - rw_hi.astype(F32)).astype(BF16)
    rb = jnp.concatenate([p["router_b"][l], jnp.full((LANES - N_EXP,), -jnp.inf, F32)]).reshape(1, LANES)
    return dict(
        w_in=w_perm, norm1_g=p["norm1_g"][l].reshape(1, D), norm2_g=p["norm2_g"][l].reshape(1, D),
        qg=qg, wq=wq, kvg=p["mla_kv_norm_g"][l].reshape(1, LANES), wkv=wkv,
        wg=wg.astype(BF16), bg=p["gla_b_gate"][l].reshape(2, 1, LANES), gla_norm_g=p["gla_norm_g"][l].reshape(1, 256),
        lru_conv_w=p["lru_conv_w"][l], lru_conv_b=p["lru_conv_b"][l].reshape(1, LRU_W), lru_wg=lru_wg, lru_bg=lru_bg,
        lru_lambda=p["lru_lambda"][l].reshape(2, 1, LRU_W),
        w_out=p["w_out"][l].astype(BF16), rw_hi=rw_hi, rw_lo=rw_lo, rb=rb,
        exp_w_gu=p["exp_w_gu"][l], exp_b_gu=p["exp_b_gu"][l].reshape(N_EXP, 1, 2 * D_EXP),
        exp_w_down=p["exp_w_down"][l], exp_b_down=p["exp_b_down"][l].reshape(N_EXP, 1, D),
        na_bias=_na_bias_table(p["na_rpb"][l]),
    )


def _state_to_blockdiag(s):
    eye = jnp.eye(GLA_HEADS, dtype=s.dtype)
    return jnp.einsum("bdhkv,hg->bdhvgk", s, eye).reshape(s.shape[0], 2, 256, LANES)


def _blockdiag_to_state(st):
    b = st.shape[0]
    s6 = st.reshape(b, 2, GLA_HEADS, GLA_DV, GLA_HEADS, GLA_DK)
    diag = jnp.stack([s6[:, :, h, :, h, :] for h in range(GLA_HEADS)], axis=2)
    return diag.transpose(0, 1, 2, 4, 3)


def kernel(x_prompt, x_sample, cache_mla_ckv, cache_mla_krope, cache_na_k, cache_na_v, state_gla, state_lru, c, c_ctx, norm1_g, norm2_g, w_mod, b_mod, w_in, w_out, mla_q_norm_g, mla_w_uq, mla_kv_norm_g, mla_w_ukv, na_rpb, gla_w_gate, gla_b_gate, gla_norm_g, lru_conv_w, lru_conv_b, lru_w_a, lru_b_a, lru_w_x, lru_b_x, lru_lambda, router_w, router_b, exp_w_gu, exp_b_gu, exp_w_down, exp_b_down, final_norm_g):
    p = dict(norm1_g=norm1_g, norm2_g=norm2_g, w_in=w_in, w_out=w_out, mla_q_norm_g=mla_q_norm_g, mla_w_uq=mla_w_uq,
             mla_kv_norm_g=mla_kv_norm_g, mla_w_ukv=mla_w_ukv, na_rpb=na_rpb, gla_w_gate=gla_w_gate,
             gla_b_gate=gla_b_gate, gla_norm_g=gla_norm_g, lru_conv_w=lru_conv_w, lru_conv_b=lru_conv_b,
             lru_w_a=lru_w_a, lru_b_a=lru_b_a, lru_w_x=lru_w_x, lru_b_x=lru_b_x, lru_lambda=lru_lambda,
             router_w=router_w, router_b=router_b, exp_w_gu=exp_w_gu, exp_b_gu=exp_b_gu, exp_w_down=exp_w_down,
             exp_b_down=exp_b_down)
    x = jnp.concatenate([x_prompt.reshape(T_CTX, D), x_sample.reshape(T_LAT, D)], axis=0)
    crow = jnp.concatenate([c_ctx[None, :], c, jnp.zeros((8 - 1 - B_LAT, D), F32)], axis=0)
    mods = _modulation(crow, w_mod, b_mod).reshape(DEPTH, 8, 1, 6 * D)
    tabs = _rope_tables()

    col = [[] for _ in range(6)]
    moe = None
    x1 = None
    for l in range(DEPTH):
        lw = _layer_weights(l, p)
        if l == 0:
            x, big, seg, ckv, q, k, v = _proj(True, x, None, None, None, mods[l], lw, tabs)
        else:
            x, big, seg, ckv, q, k, v = _proj(False, x1, moe, gw, mods[l - 1], mods[l], lw, tabs)
        kr_c = jnp.pad(cache_mla_krope[:, l].reshape(B_LAT * PAST, MLA_ROPE), ((0, 0), (0, LANES - MLA_ROPE)))
        kc, vc = _cache_kv(cache_mla_ckv[:, l].reshape(B_LAT * PAST, MLA_KV_LORA), kr_c, lw["wkv"])
        oa_c, ob_c = _ctx_attn(q, k, v, big)
        oa_l = _lat_mla(q, k, v, kc, vc)
        ob_l = _lat_na(big, cache_na_k[:, l].reshape(B_LAT * PAST, 256), cache_na_v[:, l].reshape(B_LAT * PAST, 256),
                       lw["na_bias"])
        oc_c, sg = _gla(big, seg, lw, None)
        oc_l, _ = _gla(big, seg, lw, _state_to_blockdiag(state_gla[:, l]))
        od_c, sl = _lru(big, lw, None)
        od_l, _ = _lru(big, lw, state_lru[:, l].reshape(B_LAT, 2, 1, LRU_W))
        x1, h2, idx, gw = _merge(x, (oa_c, oa_l, ob_c, ob_l, oc_c, oc_l, od_c, od_l), mods[l], lw)
        y = _moe(*_route(idx[:, :TOP_K]), h2, lw)
        moe = y.reshape((N_ROWS + N_DUMP) // TOP_K, TOP_K * SUB, LANES)
        col[0].append(ckv[:T_CTX].reshape(B_CTX, S_CTX, MLA_KV_LORA))
        col[1].append(seg[:T_CTX, :MLA_ROPE].reshape(B_CTX, S_CTX, MLA_ROPE))
        col[2].append(big[:T_CTX, C_NAK:C_NAK + 256].reshape(B_CTX, S_CTX, NA_HEADS, NA_DH))
        col[3].append(big[:T_CTX, C_NAV:C_NAV + 256].reshape(B_CTX, S_CTX, NA_HEADS, NA_DH))
        col[4].append(_blockdiag_to_state(sg))
        col[5].append(sl.reshape(B_CTX, 2, LRU_W))
    fg = final_norm_g.reshape(1, D)
    y_prompt = _final(x1, moe, gw, mods[DEPTH - 1], fg, 0, NT_CTX).reshape(B_CTX, S_CTX, D)
    y_sample = _final(x1, moe, gw, mods[DEPTH - 1], fg, NT_CTX, NT_ALL - NT_CTX).reshape(B_LAT, S_LAT, D)
    return (y_prompt, y_sample) + tuple(jnp.stack(cl, axis=1) for cl in col)
```

```python
import functools

import numpy as np
import jax
import jax.numpy as jnp
from jax import lax
from jax.experimental import pallas as pl
from jax.experimental.pallas import tpu as pltpu

F32 = jnp.float32
BF16 = jnp.bfloat16
HIGHEST = lax.Precision.HIGHEST

D = 1024
DEPTH = 2
B_CTX, S_CTX = 32, 256
B_LAT, S_LAT = 4, 2048
PAST = 512
T_CTX = B_CTX * S_CTX
T_LAT = B_LAT * S_LAT
T_ALL = T_CTX + T_LAT
GRID_W = 64
ROWS = S_LAT // GRID_W

MLA_HEADS, MLA_NOPE, MLA_ROPE, MLA_V = 4, 64, 32, 64
MLA_Q_LORA, MLA_KV_LORA = 192, 128
MLA_SCALE = (MLA_NOPE + MLA_ROPE) ** -0.5
NA_HEADS, NA_DH, NA_WIN_H, NA_WIN_W = 4, 64, 8, 16
NA_SCALE = NA_DH ** -0.5
GLA_HEADS, GLA_DK, GLA_DV, GLA_RANK, GLA_NORM, GLA_CHUNK = 4, 32, 64, 16, 16.0, 64
GLA_SUB = 16
LRU_W, LRU_BLOCKS, LRU_C = 256, 4, 8.0
N_EXP, TOP_K, D_EXP = 32, 4, 1024
SWIGLU_ALPHA, SWIGLU_LIMIT = 1.702, 7.0
ROPE_BASE = 10000.0
EPS = 1e-6
NEG = -1e30

LANES = 128
VMEM_BYTES = 64 * 1024 * 1024

TM = 256
NT_CTX = T_CTX // TM
NT_ALL = T_ALL // TM
TILES_PER_LAT = S_LAT // TM
ID_ROWS = 2 * TM

TM_MOE = 512
N_ROWS = T_ALL * TOP_K
R_MAX = N_ROWS + N_EXP * TM_MOE
NT_MOE = R_MAX // TM_MOE
EXP_CHUNK = 256
N_DUMP = 2 * TM_MOE
SUB = D // LANES
assert T_ALL & (T_ALL - 1) == 0

C_NAQ, C_NAK, C_NAV, C_GQ, C_GK, C_GV, C_GG, C_LX, C_LG = 0, 256, 512, 768, 896, 1024, 1280, 1536, 1792
BIG = 2048
C_PKV, C_PQ, C_SEGA, C_SEGB = 2048, 2176, 2432, 2560
NC_IN = 2688
SEG_GA = 32


def _cp(vmem_mb, sem=None):
    kw = dict(vmem_limit_bytes=vmem_mb * 1024 * 1024)
    if sem is not None:
        kw["dimension_semantics"] = sem
    return pltpu.CompilerParams(**kw)


def _rms(x):
    return x * lax.rsqrt(jnp.mean(x * x, axis=-1, keepdims=True) + EPS)


def _dot(a, b, **kw):
    return jnp.dot(a, b, preferred_element_type=F32, **kw)


def _dot_nt(a, b):
    return lax.dot_general(a, b, (((1,), (1,)), ((), ())), preferred_element_type=F32)


def _dot_tn(a, b):
    return lax.dot_general(a, b, (((0,), (0,)), ((), ())), preferred_element_type=F32)


def _const_spec(shape):
    n = len(shape)
    return pl.BlockSpec(shape, lambda *_: (0,) * n)


def _mod_body(c_ref, w_ref, b_ref, o_ref):
    c = c_ref[...]
    s = c * jax.nn.sigmoid(c)
    o_ref[0] = _dot(s.astype(BF16), w_ref[0].astype(BF16)) + b_ref[0]


def _modulation(crow, w_mod, b_mod):
    tn = 1536
    return pl.pallas_call(
        _mod_body,
        grid=(DEPTH, 6 * D // tn),
        in_specs=[
            pl.BlockSpec((8, D), lambda l, j: (0, 0)),
            pl.BlockSpec((1, D, tn), lambda l, j: (l, 0, j)),
            pl.BlockSpec((1, 1, tn), lambda l, j: (l, 0, j)),
        ],
        out_specs=pl.BlockSpec((1, 8, tn), lambda l, j: (l, 0, j)),
        out_shape=jax.ShapeDtypeStruct((DEPTH, 8, 6 * D), F32),
        compiler_params=_cp(40),
        name="modulation",
    )(crow, w_mod, b_mod.reshape(DEPTH, 1, 6 * D))


def _mod_row(i):
    return jnp.where(i < NT_CTX, 0, 1 + (i - NT_CTX) // TILES_PER_LAT)


def _pos_block(i):
    return jnp.where(i < NT_CTX, 0, ID_ROWS // TM + (i - NT_CTX) % TILES_PER_LAT)


def _sum_slots(y_refs, gw_ref):
    gw = gw_ref[...]
    cols = [gw[:, kk:kk + 1] for kk in range(TOP_K)]
    parts = []
    for s in range(SUB):
        acc = cols[0] * y_refs[0][pl.ds(s, TM, stride=SUB), :]
        for kk in range(1, TOP_K):
            acc = acc + cols[kk] * y_refs[kk][pl.ds(s, TM, stride=SUB), :]
        parts.append(acc)
    return jnp.concatenate(parts, axis=-1)


def _slot_specs(tile0):
    return [pl.BlockSpec((TM * SUB, LANES), functools.partial(lambda kk, i: (kk * NT_ALL + tile0 + i, 0), kk))
            for kk in range(TOP_K)]


def _proj_body(first, *refs):
    if first:
        (x_ref, mod_ref, g1_ref, win_ref, qg_ref, wq_ref, kvg_ref, wkv_ref, cq_ref, sq_ref, cs_ref, ss_ref,
         big_ref, seg_ref, ckv_ref, q_ref, k_ref, v_ref) = refs
        x = x_ref[...]
    else:
        (x_ref, y0_ref, y1_ref, y2_ref, y3_ref, gw_ref, modp_ref, mod_ref, g1_ref, win_ref, qg_ref, wq_ref, kvg_ref, wkv_ref, cq_ref, sq_ref,
         cs_ref, ss_ref, xo_ref, big_ref, seg_ref, ckv_ref, q_ref, k_ref, v_ref) = refs
        g2 = modp_ref[0][:, 5 * D:6 * D]
        x = x_ref[...] + g2 * _sum_slots((y0_ref, y1_ref, y2_ref, y3_ref), gw_ref)
        xo_ref[...] = x
    m = mod_ref[0]
    sh1, sc1 = m[:, 0:D], m[:, D:2 * D]
    h = _rms(x) * g1_ref[...] * (1.0 + sc1) + sh1
    p = _dot(h.astype(BF16), win_ref[...])
    big_ref[...] = p[:, :BIG]
    ckv = _rms(p[:, C_PKV:C_PKV + MLA_KV_LORA]) * kvg_ref[...]
    ckv_ref[...] = ckv
    pq = p[:, C_PQ:C_PQ + 256]
    qn = pq * lax.rsqrt(jnp.sum(pq * pq, axis=-1, keepdims=True) * (1.0 / MLA_Q_LORA) + EPS) * qg_ref[...]
    a = _dot(qn.astype(BF16), wq_ref[...])
    q_ref[...] = (a[:, :512] * cq_ref[...] + a[:, 512:] * sq_ref[...]).astype(BF16)
    seg = p[:, C_SEGA:C_SEGA + LANES] * cs_ref[...] + p[:, C_SEGB:C_SEGB + LANES] * ss_ref[...]
    seg_ref[...] = seg
    kv = _dot(jnp.concatenate([ckv, seg], axis=-1).astype(BF16), wkv_ref[...])
    k_ref[...] = kv[:, :512].astype(BF16)
    v_ref[...] = kv[:, 512:].astype(BF16)


def _proj(first, x, moe, gw, mods_prev, mods_l, lw, tabs):
    tok = lambda w: pl.BlockSpec((TM, w), lambda i: (i, 0))
    modspec = pl.BlockSpec((1, 1, 6 * D), lambda i: (_mod_row(i), 0, 0))
    pos = lambda w: pl.BlockSpec((TM, w), lambda i: (_pos_block(i), 0))
    in_specs = [tok(D)]
    args = [x]
    if not first:
        in_specs += _slot_specs(0) + [tok(LANES), modspec]
        args += [moe] * TOP_K + [gw, mods_prev]
    in_specs += [modspec, _const_spec((1, D)), _const_spec((D, NC_IN)), _const_spec((1, 256)),
                 _const_spec((256, 1024)), _const_spec((1, LANES)), _const_spec((256, 768)),
                 pos(512), pos(512), pos(LANES), pos(LANES)]
    args += [mods_l, lw["norm1_g"], lw["w_in"], lw["qg"], lw["wq"], lw["kvg"], lw["wkv"],
             tabs["cq"], tabs["sq"], tabs["cs"], tabs["ss"]]
    out_specs = [tok(BIG), tok(LANES), tok(LANES), tok(512), tok(512), tok(256)]
    out_shape = [jax.ShapeDtypeStruct((T_ALL, BIG), F32), jax.ShapeDtypeStruct((T_ALL, LANES), F32),
                 jax.ShapeDtypeStruct((T_ALL, LANES), F32), jax.ShapeDtypeStruct((T_ALL, 512), BF16),
                 jax.ShapeDtypeStruct((T_ALL, 512), BF16), jax.ShapeDtypeStruct((T_ALL, 256), BF16)]
    if not first:
        out_specs = [tok(D)] + out_specs
        out_shape = [jax.ShapeDtypeStruct((T_ALL, D), F32)] + out_shape
    outs = pl.pallas_call(
        functools.partial(_proj_body, first),
        grid=(NT_ALL,),
        in_specs=in_specs,
        out_specs=out_specs,
        out_shape=out_shape,
        compiler_params=_cp(48),
        name="proj",
    )(*args)
    if first:
        return (x,) + tuple(outs)
    return tuple(outs)


def _kvc_body(ckv_ref, kr_ref, wkv_ref, k_ref, v_ref):
    kv = _dot(jnp.concatenate([ckv_ref[...], kr_ref[...]], axis=-1).astype(BF16), wkv_ref[...])
    k_ref[...] = kv[:, :512].astype(BF16)
    v_ref[...] = kv[:, 512:].astype(BF16)


def _cache_kv(ckv_c, kr_c, wkv):
    n = B_LAT * PAST
    return pl.pallas_call(
        _kvc_body,
        grid=(B_LAT,),
        in_specs=[pl.BlockSpec((PAST, LANES), lambda b: (b, 0)), pl.BlockSpec((PAST, LANES), lambda b: (b, 0)),
                  _const_spec((256, 768))],
        out_specs=[pl.BlockSpec((PAST, 512), lambda b: (b, 0)), pl.BlockSpec((PAST, 256), lambda b: (b, 0))],
        out_shape=[jax.ShapeDtypeStruct((n, 512), BF16), jax.ShapeDtypeStruct((n, 256), BF16)],
        name="cache_kv",
    )(ckv_c, kr_c, wkv)


def _softmax_pv(scores, values):
    m = scores[0].max(axis=-1, keepdims=True)
    for s in scores[1:]:
        m = jnp.maximum(m, s.max(axis=-1, keepdims=True))
    den = 0.0
    acc = 0.0
    for s, v in zip(scores, values):
        e = jnp.exp(s - m)
        den = den + e.sum(axis=-1, keepdims=True)
        acc = acc + _dot(e.astype(BF16), v)
    return acc / den


def _lane_lo(shape):
    return lax.broadcasted_iota(jnp.int32, shape, len(shape) - 1) < 64


def _ctx_attn_body(q_ref, k_ref, v_ref, na_ref, oa_ref, ob_ref):
    lo = _lane_lo((S_CTX, LANES))
    for pp in range(2):
        vp = v_ref[:, pp * LANES:(pp + 1) * LANES]
        outs = []
        for h in (2 * pp, 2 * pp + 1):
            s = _dot_nt(q_ref[:, h * LANES:(h + 1) * LANES], k_ref[:, h * LANES:(h + 1) * LANES]) * MLA_SCALE
            outs.append(_softmax_pv([s], [vp]))
        oa_ref[:, pp * LANES:(pp + 1) * LANES] = jnp.where(lo, outs[0], outs[1])
    for pp in range(2):
        qp = na_ref[:, C_NAQ + pp * LANES:C_NAQ + (pp + 1) * LANES]
        kp = na_ref[:, C_NAK + pp * LANES:C_NAK + (pp + 1) * LANES].astype(BF16)
        vp = na_ref[:, C_NAV + pp * LANES:C_NAV + (pp + 1) * LANES].astype(BF16)
        outs = []
        for sel in (lo, ~lo):
            qh = jnp.where(sel, qp, 0.0).astype(BF16)
            outs.append(_softmax_pv([_dot_nt(qh, kp) * NA_SCALE], [vp]))
        ob_ref[:, pp * LANES:(pp + 1) * LANES] = jnp.where(lo, outs[0], outs[1])


def _ctx_attn(q, k, v, big):
    blk = lambda w: pl.BlockSpec((S_CTX, w), lambda b: (b, 0))
    return pl.pallas_call(
        _ctx_attn_body,
        grid=(B_CTX,),
        in_specs=[blk(512), blk(512), blk(256), blk(768)],
        out_specs=[blk(256), blk(256)],
        out_shape=[jax.ShapeDtypeStruct((T_CTX, 256), F32), jax.ShapeDtypeStruct((T_CTX, 256), F32)],
        name="ctx_attn",
    )(q, k, v, big)


TQ = 256


def _lat_mla_body(q_ref, kl_ref, vl_ref, kc_ref, vc_ref, o_ref):
    lo = _lane_lo((TQ, LANES))
    for pp in range(2):
        vlp = vl_ref[:, pp * LANES:(pp + 1) * LANES]
        vcp = vc_ref[:, pp * LANES:(pp + 1) * LANES]
        outs = []
        for h in (2 * pp, 2 * pp + 1):
            qh = q_ref[:, h * LANES:(h + 1) * LANES]
            s_l = _dot_nt(qh, kl_ref[:, h * LANES:(h + 1) * LANES]) * MLA_SCALE
            s_c = _dot_nt(qh, kc_ref[:, h * LANES:(h + 1) * LANES]) * MLA_SCALE
            outs.append(_softmax_pv([s_l, s_c], [vlp, vcp]))
        o_ref[:, pp * LANES:(pp + 1) * LANES] = jnp.where(lo, outs[0], outs[1])


def _lat_mla(q, k, v, kc, vc):
    nq = S_LAT // TQ
    qrow = lambda b, t: (T_CTX // TQ + b * nq + t, 0)
    lat = lambda b, t: (T_CTX // S_LAT + b, 0)
    return pl.pallas_call(
        _lat_mla_body,
        grid=(B_LAT, nq),
        in_specs=[pl.BlockSpec((TQ, 512), qrow), pl.BlockSpec((S_LAT, 512), lat), pl.BlockSpec((S_LAT, 256), lat),
                  pl.BlockSpec((PAST, 512), lambda b, t: (b, 0)), pl.BlockSpec((PAST, 256), lambda b, t: (b, 0))],
        out_specs=pl.BlockSpec((TQ, 256), lambda b, t: (b * nq + t, 0)),
        out_shape=jax.ShapeDtypeStruct((T_LAT, 256), F32),
        compiler_params=_cp(48),
        name="lat_mla",
    )(q, k, v, kc, vc)


NK_LOC = NA_WIN_H * GRID_W


def _na_row_start(r):
    return jnp.clip(r - NA_WIN_H // 2, 0, ROWS - NA_WIN_H)


def _lat_na_body(q_ref, k_ref, v_ref, kc_ref, vc_ref, bias_ref, o_ref):
    r = pl.program_id(1)
    k0 = pl.multiple_of(_na_row_start(r) * GRID_W, GRID_W)
    lo = _lane_lo((GRID_W, LANES))
    for pp in range(2):
        cols = slice(pp * LANES, (pp + 1) * LANES)
        qp = q_ref[:, cols]
        kl = k_ref[pl.ds(k0, NK_LOC), cols].astype(BF16)
        vl = v_ref[pl.ds(k0, NK_LOC), cols].astype(BF16)
        kc = kc_ref[:, cols].astype(BF16)
        vc = vc_ref[:, cols].astype(BF16)
        outs = []
        for j, sel in enumerate((lo, ~lo)):
            qh = jnp.where(sel, qp, 0.0).astype(BF16)
            s_loc = _dot_nt(qh, kl) * NA_SCALE + bias_ref[0, 2 * pp + j]
            s_ctx = _dot_nt(qh, kc) * NA_SCALE
            outs.append(_softmax_pv([s_loc, s_ctx], [vl, vc]))
        o_ref[:, cols] = jnp.where(lo, outs[0], outs[1])


def _lat_na(big, kc, vc, bias):
    qrow = lambda b, r: (T_CTX // GRID_W + b * ROWS + r, 0)
    boff = lambda b, r: (_na_row_start(r) - r + NA_WIN_H - 1, 0, 0, 0)
    lat = lambda c: (lambda b, r: (T_CTX // S_LAT + b, c))
    return pl.pallas_call(
        _lat_na_body,
        grid=(B_LAT, ROWS),
        in_specs=[pl.BlockSpec((GRID_W, 256), qrow), pl.BlockSpec((S_LAT, 256), lat(1)),
                  pl.BlockSpec((S_LAT, 256), lat(2)),
                  pl.BlockSpec((PAST, 256), lambda b, r: (b, 0)), pl.BlockSpec((PAST, 256), lambda b, r: (b, 0)),
                  pl.BlockSpec((1, NA_HEADS, GRID_W, NK_LOC), boff)],
        out_specs=pl.BlockSpec((GRID_W, 256), lambda b, r: (b * ROWS + r, 0)),
        out_shape=jax.ShapeDtypeStruct((T_LAT, 256), F32),
        compiler_params=_cp(48),
        name="lat_na",
    )(big, big, big, kc, vc, bias)


def _log_sigmoid(x):
    return jnp.minimum(x, 0.0) - jnp.log1p(jnp.exp(-jnp.abs(x)))


def _gla_chunk(rev, q, k, v, la, st):
    C, SB = GLA_CHUNK, GLA_SUB
    ri = lax.broadcasted_iota(jnp.int32, (C, C), 0)
    ci = lax.broadcasted_iota(jnp.int32, (C, C), 1)
    tri = ((ci >= ri) if rev else (ci <= ri)).astype(F32)
    cb = _dot(tri, la, precision=HIGHEST)
    cl = cb[0:1] if rev else cb[C - 1:C]
    o = _dot_nt((q * jnp.exp(cb)).astype(BF16), st.astype(BF16))
    kd = (k * jnp.exp(cl - cb)).astype(BF16)
    bd = (lax.broadcasted_iota(jnp.int32, (256, LANES), 0) // GLA_DV
          == lax.broadcasted_iota(jnp.int32, (256, LANES), 1) // GLA_DK)
    st_new = st * jnp.exp(cl) + jnp.where(bd, _dot_tn(v.astype(BF16), kd), 0.0)

    vb = v.astype(BF16)
    lane_h = lax.broadcasted_iota(jnp.int32, (SB, LANES), 1) // GLA_DK
    col_h = lax.broadcasted_iota(jnp.int32, (SB, 256), 1) // GLA_DV
    expand = (lax.broadcasted_iota(jnp.int32, (LANES, 256), 0) // GLA_DK
              == lax.broadcasted_iota(jnp.int32, (LANES, 256), 1) // GLA_DV).astype(BF16)
    row = lax.broadcasted_iota(jnp.int32, (C, LANES), 0)
    pj = lax.broadcasted_iota(jnp.int32, (SB * SB, LANES), 0) // SB
    pi = lax.broadcasted_iota(jnp.int32, (SB * SB, LANES), 0) % SB
    pmask = (pi <= pj) if rev else (pi >= pj)
    nsub = C // SB
    parts = []
    for i in range(nsub):
        sl = slice(i * SB, (i + 1) * SB)
        qi, ki, cbi, vi = q[sl], k[sl], cb[sl], v[sl]
        acc = o[sl]
        has_off = (i < nsub - 1) if rev else (i > 0)
        if has_off:
            if rev:
                ref = cb[(i + 1) * SB:(i + 1) * SB + 1]
                outside = row >= (i + 1) * SB
            else:
                ref = cb[i * SB - 1:i * SB]
                outside = row < i * SB
            a = qi * jnp.exp(cbi - ref)
            bm = jnp.where(outside, k * jnp.exp(jnp.minimum(ref - cb, 0.0)), 0.0).astype(BF16)
            a4 = jnp.concatenate([jnp.where(lane_h == h, a, 0.0) for h in range(GLA_HEADS)], axis=0)
            att = _dot_nt(a4.astype(BF16), bm)
            pv = _dot(att.astype(BF16), vb)
            for h in range(GLA_HEADS):
                acc = acc + jnp.where(col_h == h, pv[h * SB:(h + 1) * SB], 0.0)
        kj = jnp.broadcast_to(ki[:, None, :], (SB, SB, LANES)).reshape(SB * SB, LANES)
        cbj = jnp.broadcast_to(cbi[:, None, :], (SB, SB, LANES)).reshape(SB * SB, LANES)
        qq = jnp.broadcast_to(qi[None, :, :], (SB, SB, LANES)).reshape(SB * SB, LANES)
        cbq = jnp.broadcast_to(cbi[None, :, :], (SB, SB, LANES)).reshape(SB * SB, LANES)
        x = qq * kj * jnp.where(pmask, jnp.exp(jnp.where(pmask, cbq - cbj, 0.0)), 0.0)
        y = _dot(x.astype(BF16), expand).reshape(SB, SB, 256)
        acc = acc + jnp.sum(y * vi[:, None, :], axis=0)
        parts.append(acc)
    return jnp.concatenate(parts, axis=0), st_new


def _gla_body(S, has_s0, *refs):
    if has_s0:
        (gq_ref, gk_ref, gv_ref, gg_ref, seg_ref, wg_ref, bg_ref, gn_ref, s0_ref,
         o_ref, sfin_ref, la_scr, of_scr, ob_scr, st_scr) = refs
    else:
        (gq_ref, gk_ref, gv_ref, gg_ref, seg_ref, wg_ref, bg_ref, gn_ref,
         o_ref, sfin_ref, la_scr, of_scr, ob_scr, st_scr) = refs
    nc = S // GLA_CHUNK
    seg = seg_ref[...].astype(BF16)
    for d in range(2):
        la_scr[d] = _log_sigmoid(_dot(seg, wg_ref[d]) + bg_ref[d]) * (1.0 / GLA_NORM)
        if has_s0:
            st_scr[d] = s0_ref[0, d]
        else:
            st_scr[d] = jnp.zeros((256, LANES), F32)

    def body(c, carry):
        for d, out_scr in ((0, of_scr), (1, ob_scr)):
            r0 = pl.multiple_of((c if d == 0 else nc - 1 - c) * GLA_CHUNK, GLA_CHUNK)
            rows = pl.ds(r0, GLA_CHUNK)
            q = gq_ref[rows, :] * (GLA_DK ** -0.5)
            o, st = _gla_chunk(d == 1, q, gk_ref[rows, :], gv_ref[rows, :], la_scr[d, rows, :], st_scr[d])
            out_scr[rows, :] = o
            st_scr[d] = st
        return carry

    lax.fori_loop(0, nc, body, 0)
    sfin_ref[0] = st_scr[...]
    o = of_scr[...] + ob_scr[...]
    ones = (lax.broadcasted_iota(jnp.int32, (256, 256), 0) // GLA_DV
            == lax.broadcasted_iota(jnp.int32, (256, 256), 1) // GLA_DV).astype(F32)
    ms = _dot(o * o, ones, precision=HIGHEST) * (1.0 / GLA_DV)
    g = gg_ref[...]
    o_ref[...] = o * lax.rsqrt(ms + EPS) * gn_ref[...] * (g * jax.nn.sigmoid(g))


def _gla(big, seg, lw, s0):
    has_s0 = s0 is not None
    S, nb, base = (S_LAT, B_LAT, T_CTX // S_LAT) if has_s0 else (S_CTX, B_CTX, 0)
    blk = lambda w, c: pl.BlockSpec((S, w), lambda b: (base + b, c))
    in_specs = [blk(LANES, C_GQ // LANES), blk(LANES, C_GK // LANES), blk(256, C_GV // 256), blk(256, C_GG // 256),
                blk(LANES, 0), _const_spec((2, LANES, LANES)), _const_spec((2, 1, LANES)), _const_spec((1, 256))]
    args = [big, big, big, big, seg, lw["wg"], lw["bg"], lw["gla_norm_g"]]
    if has_s0:
        in_specs += [pl.BlockSpec((1, 2, 256, LANES), lambda b: (b, 0, 0, 0))]
        args += [s0]
    return pl.pallas_call(
        functools.partial(_gla_body, S, has_s0),
        grid=(nb,),
        in_specs=in_specs,
        out_specs=[pl.BlockSpec((S, 256), lambda b: (b, 0)), pl.BlockSpec((1, 2, 256, LANES), lambda b: (b, 0, 0, 0))],
        out_shape=[jax.ShapeDtypeStruct((nb * S, 256), F32), jax.ShapeDtypeStruct((nb, 2, 256, LANES), F32)],
        scratch_shapes=[pltpu.VMEM((2, S, LANES), F32), pltpu.VMEM((S, 256), F32), pltpu.VMEM((S, 256), F32),
                        pltpu.VMEM((2, 256, LANES), F32)],
        compiler_params=_cp(48),
        name="gla_lat" if has_s0 else "gla_ctx",
    )(*args)


def _softplus(x):
    return jnp.maximum(x, 0.0) + jnp.log1p(jnp.exp(-jnp.abs(x)))


def _lru_body(S, has_h0, *refs):
    if has_h0:
        (lx_ref, lg_ref, cw_ref, cb_ref, wg_ref, bgate_ref, lam_ref, h0_ref,
         o_ref, hfin_ref, a_scr, b_scr, hf_scr, hb_scr) = refs
    else:
        (lx_ref, lg_ref, cw_ref, cb_ref, wg_ref, bgate_ref, lam_ref,
         o_ref, hfin_ref, a_scr, b_scr, hf_scr, hb_scr) = refs
    x = lx_ref[...]
    row = lax.broadcasted_iota(jnp.int32, (S, LRU_W), 0)
    xm1 = jnp.where(row >= 1, pltpu.roll(x, 1, 0), 0.0)
    xp1 = jnp.where(row < S - 1, pltpu.roll(x, S - 1, 0), 0.0)
    xp2 = jnp.where(row < S - 2, pltpu.roll(x, S - 2, 0), 0.0)
    xc = cb_ref[...] + xm1 * cw_ref[0:1] + x * cw_ref[1:2] + xp1 * cw_ref[2:3] + xp2 * cw_ref[3:4]
    gates = _dot(xc.astype(BF16), wg_ref[...]) + bgate_ref[...]
    for d in range(2):
        r = jax.nn.sigmoid(gates[:, d * LRU_W:(d + 1) * LRU_W])
        ig = jax.nn.sigmoid(gates[:, (2 + d) * LRU_W:(3 + d) * LRU_W])
        log_a = -LRU_C * r * _softplus(-lam_ref[d])
        a_scr[d] = jnp.exp(log_a)
        b_scr[d] = jnp.sqrt(1.0 - jnp.exp(2.0 * log_a)) * (ig * xc)
    if has_h0:
        h_init = (h0_ref[0, 0], h0_ref[0, 1])
    else:
        h_init = (jnp.zeros((1, LRU_W), F32), jnp.zeros((1, LRU_W), F32))

    def body(i, carry):
        hf, hb = carry
        r0 = pl.multiple_of(i * 8, 8)
        rb = pl.multiple_of(S - 8 - i * 8, 8)
        af, bf = a_scr[0, pl.ds(r0, 8), :], b_scr[0, pl.ds(r0, 8), :]
        ab, bb = a_scr[1, pl.ds(rb, 8), :], b_scr[1, pl.ds(rb, 8), :]
        fs, bs = [], [None] * 8
        for t in range(8):
            hf = af[t:t + 1] * hf + bf[t:t + 1]
            fs.append(hf)
            u = 7 - t
            hb = ab[u:u + 1] * hb + bb[u:u + 1]
            bs[u] = hb
        hf_scr[pl.ds(r0, 8), :] = jnp.concatenate(fs, axis=0)
        hb_scr[pl.ds(rb, 8), :] = jnp.concatenate(bs, axis=0)
        return hf, hb

    hf, hb = lax.fori_loop(0, S // 8, body, h_init)
    hfin_ref[0, 0] = hf
    hfin_ref[0, 1] = hb
    g = lg_ref[...]
    gelu = g * (0.5 * (1.0 + jnp.tanh(np.sqrt(2.0 / np.pi).astype(np.float32) * (g + 0.044715 * (g * g * g)))))
    o_ref[...] = (hf_scr[...] + hb_scr[...]) * gelu


def _lru(big, lw, h0):
    has_h0 = h0 is not None
    S, nb, base = (S_LAT, B_LAT, T_CTX // S_LAT) if has_h0 else (S_CTX, B_CTX, 0)
    blk = lambda c: pl.BlockSpec((S, LRU_W), lambda b: (base + b, c))
    in_specs = [blk(C_LX // LRU_W), blk(C_LG // LRU_W), _const_spec((4, LRU_W)), _const_spec((1, LRU_W)),
                _const_spec((LRU_W, 4 * LRU_W)), _const_spec((1, 4 * LRU_W)), _const_spec((2, 1, LRU_W))]
    args = [big, big, lw["lru_conv_w"], lw["lru_conv_b"], lw["lru_wg"], lw["lru_bg"], lw["lru_lambda"]]
    if has_h0:
        in_specs += [pl.BlockSpec((1, 2, 1, LRU_W), lambda b: (b, 0, 0, 0))]
        args += [h0]
    return pl.pallas_call(
        functools.partial(_lru_body, S, has_h0),
        grid=(nb,),
        in_specs=in_specs,
        out_specs=[pl.BlockSpec((S, LRU_W), lambda b: (b, 0)),
                   pl.BlockSpec((1, 2, 1, LRU_W), lambda b: (b, 0, 0, 0))],
        out_shape=[jax.ShapeDtypeStruct((nb * S, LRU_W), F32), jax.ShapeDtypeStruct((nb, 2, 1, LRU_W), F32)],
        scratch_shapes=[pltpu.VMEM((2, S, LRU_W), F32), pltpu.VMEM((2, S, LRU_W), F32),
                        pltpu.VMEM((S, LRU_W), F32), pltpu.VMEM((S, LRU_W), F32)],
        compiler_params=_cp(48),
        name="lru_lat" if has_h0 else "lru_ctx",
    )(*args)


def _merge_body(x_ref, *refs):
    (mod_ref, wout_ref, g2_ref, rwh_ref, rwl_ref, rb_ref, x1_ref, h2_ref, idx_ref, gw_ref) = refs[8:]
    is_ctx = pl.program_id(0) < NT_CTX
    mix = 0.0
    for j in range(4):
        o = jnp.where(is_ctx, refs[2 * j][...], refs[2 * j + 1][...])
        mix = mix + _dot(o.astype(BF16), wout_ref[j * 256:(j + 1) * 256, :])
    m = mod_ref[0]
    g1, sh2, sc2 = m[:, 2 * D:3 * D], m[:, 3 * D:4 * D], m[:, 4 * D:5 * D]
    x1 = x_ref[...] + g1 * mix
    x1_ref[...] = x1
    h2 = _rms(x1) * g2_ref[...] * (1.0 + sc2) + sh2
    for s in range(SUB):
        h2_ref[pl.ds(s, TM, stride=SUB), :] = h2[:, s * LANES:(s + 1) * LANES]
    hi = h2.astype(BF16)
    lo = (h2 - hi.astype(F32)).astype(BF16)
    logits = _dot(hi, rwh_ref[...]) + _dot(hi, rwl_ref[...]) + _dot(lo, rwh_ref[...]) + rb_ref[...]
    lane = lax.broadcasted_iota(jnp.int32, (TM, LANES), 1)
    idx_out = jnp.zeros((TM, LANES), jnp.int32)
    val_out = jnp.zeros((TM, LANES), F32)
    top = None
    den = 0.0
    for kk in range(TOP_K):
        mval = logits.max(axis=-1, keepdims=True)
        midx = jnp.where(logits == mval, lane, LANES).min(axis=-1, keepdims=True)
        if kk == 0:
            top = mval
        e = jnp.exp(mval - top)
        den = den + e
        idx_out = jnp.where(lane == kk, midx, idx_out)
        val_out = jnp.where(lane == kk, e, val_out)
        logits = jnp.where(lane == midx, -jnp.inf, logits)
    idx_ref[...] = idx_out
    gw_ref[...] = val_out / den


def _merge(x, mix_parts, mods_l, lw):
    tok = lambda w: pl.BlockSpec((TM, w), lambda i: (i, 0))
    ctx = pl.BlockSpec((TM, 256), lambda i: (jnp.minimum(i, NT_CTX - 1), 0))
    lat = pl.BlockSpec((TM, 256), lambda i: (jnp.maximum(i - NT_CTX, 0), 0))
    return pl.pallas_call(
        _merge_body,
        grid=(NT_ALL,),
        in_specs=[tok(D)] + [ctx, lat] * 4 + [
                  pl.BlockSpec((1, 1, 6 * D), lambda i: (_mod_row(i), 0, 0)),
                  _const_spec((D, D)), _const_spec((1, D)), _const_spec((D, LANES)), _const_spec((D, LANES)),
                  _const_spec((1, LANES))],
        out_specs=[tok(D), pl.BlockSpec((TM * SUB, LANES), lambda i: (i, 0)), tok(LANES), tok(LANES)],
        out_shape=[jax.ShapeDtypeStruct((T_ALL, D), F32), jax.ShapeDtypeStruct((T_ALL * SUB, LANES), F32),
                   jax.ShapeDtypeStruct((T_ALL, LANES), jnp.int32), jax.ShapeDtypeStruct((T_ALL, LANES), F32)],
        compiler_params=_cp(40),
        name="merge",
    )(x, *mix_parts, mods_l, lw["w_out"], lw["norm2_g"], lw["rw_hi"], lw["rw_lo"], lw["rb"])


def _moe_body(sn_ref, te_ref, nv_ref, tb_ref, h2_hbm, wgu_ref, bgu_ref, wd_ref, bd_ref,
              y_hbm, xbuf, obuf, wgu_bf, wd_bf, gsem, ssem):
    i = pl.program_id(0)
    slot = i % 2
    cur = jnp.minimum(i, NT_MOE - 1)
    valid = (i < NT_MOE) & (nv_ref[cur] > 0)
    prv = jnp.maximum(i - 1, 0)
    nv_prev = jnp.where(i >= 1, nv_ref[prv], 0)
    base_prev = tb_ref[prv]
    base_next = tb_ref[jnp.minimum(i + 1, NT_MOE - 1)]

    def gather_row(base, sl, r):
        tok = jnp.bitwise_and(sn_ref[base + r], T_ALL - 1)
        pltpu.make_async_copy(h2_hbm.at[pl.ds(pl.multiple_of(tok * SUB, SUB), SUB)],
                              xbuf.at[sl, pl.ds(pl.multiple_of(r * SUB, SUB), SUB)], gsem.at[sl]).start()

    def scatter_row(sl, r):
        dst = jnp.where(r < nv_prev, sn_ref[base_prev + r], N_ROWS + sl * TM_MOE + r)
        pltpu.make_async_copy(obuf.at[sl, pl.ds(pl.multiple_of(r * SUB, SUB), SUB)],
                              y_hbm.at[pl.ds(pl.multiple_of(dst * SUB, SUB), SUB)], ssem.at[sl]).start()

    def for_other_slot(fn, lo=0, hi=TM_MOE):
        for other in range(2):
            @pl.when(slot == 1 - other)
            def _():
                lax.fori_loop(lo, hi, lambda r, c: fn(other, r) or c, 0, unroll=8)

    def gather_done(sl):
        pltpu.make_async_copy(h2_hbm.at[pl.ds(0, TM_MOE * SUB)], xbuf.at[sl], gsem.at[sl]).wait()

    def scatter_done(sl):
        pltpu.make_async_copy(obuf.at[sl], y_hbm.at[pl.ds(0, TM_MOE * SUB)], ssem.at[sl]).wait()

    @pl.when(i == 0)
    def _():
        obuf[1] = jnp.zeros((TM_MOE * SUB, LANES), F32)
        cp = pltpu.make_async_copy(obuf.at[1], y_hbm.at[pl.ds(N_ROWS * SUB, TM_MOE * SUB)], ssem.at[1])
        cp.start()
        cp.wait()

        def body(r, c):
            gather_row(tb_ref[0], 0, r)
            return c
        lax.fori_loop(0, TM_MOE, body, 0, unroll=8)

    e = te_ref[cur]
    prev_e = te_ref[prv]

    @pl.when(valid & ((i == 0) | (e != prev_e)))
    def _():
        wgu_bf[...] = wgu_ref[0].astype(BF16)
        wd_bf[...] = wd_ref[0].astype(BF16)

    @pl.when(valid)
    def _():
        def both(other, r):
            gather_row(base_next, other, r)
            scatter_row(other, r)
        for_other_slot(both)
        gather_done(slot)
        x = jnp.concatenate([xbuf[slot, pl.ds(s, TM_MOE, stride=SUB), :] for s in range(SUB)], axis=-1).astype(BF16)
        acc = jnp.zeros((TM_MOE, D), F32)
        for j in range(D_EXP // EXP_CHUNK):
            c0 = j * EXP_CHUNK
            xg = _dot(x, wgu_bf[:, c0:c0 + EXP_CHUNK]) + bgu_ref[0][:, c0:c0 + EXP_CHUNK]
            xl = _dot(x, wgu_bf[:, D_EXP + c0:D_EXP + c0 + EXP_CHUNK]) + bgu_ref[0][:, D_EXP + c0:D_EXP + c0 + EXP_CHUNK]
            xg = jnp.minimum(xg, SWIGLU_LIMIT)
            xl = jnp.clip(xl, -SWIGLU_LIMIT, SWIGLU_LIMIT)
            act = xg * jax.nn.sigmoid(SWIGLU_ALPHA * xg) * (xl + 1.0)
            acc = acc + _dot(act.astype(BF16), wd_bf[c0:c0 + EXP_CHUNK, :])
        out = acc + bd_ref[0]
        for s in range(SUB):
            obuf[slot, pl.ds(s, TM_MOE, stride=SUB), :] = out[:, s * LANES:(s + 1) * LANES]
        scatter_done(1 - slot)

    @pl.when(jnp.logical_not(valid) & (nv_prev > 0))
    def _():
        gather_done(slot)
        for_other_slot(scatter_row)
        scatter_done(1 - slot)


def _moe(sorted_n, tile_exp, tile_nvalid, tile_base, h2, lw):
    wmap = lambda i, sn, te, nv, tb: (te[jnp.minimum(i, NT_MOE - 1)], 0, 0)
    grid_spec = pltpu.PrefetchScalarGridSpec(
        num_scalar_prefetch=4,
        grid=(NT_MOE + 1,),
        in_specs=[
            pl.BlockSpec(memory_space=pl.ANY),
            pl.BlockSpec((1, D, 2 * D_EXP), wmap),
            pl.BlockSpec((1, 1, 2 * D_EXP), wmap),
            pl.BlockSpec((1, D_EXP, D), wmap),
            pl.BlockSpec((1, 1, D), wmap),
        ],
        out_specs=pl.BlockSpec(memory_space=pl.ANY),
        scratch_shapes=[pltpu.VMEM((2, TM_MOE * SUB, LANES), F32), pltpu.VMEM((2, TM_MOE * SUB, LANES), F32),
                        pltpu.VMEM((D, 2 * D_EXP), BF16), pltpu.VMEM((D_EXP, D), BF16),
                        pltpu.SemaphoreType.DMA((2,)), pltpu.SemaphoreType.DMA((2,))],
    )
    return pl.pallas_call(
        _moe_body,
        grid_spec=grid_spec,
        out_shape=jax.ShapeDtypeStruct(((N_ROWS + N_DUMP) * SUB, LANES), F32),
        compiler_params=pltpu.CompilerParams(vmem_limit_bytes=56 * 1024 * 1024, dimension_semantics=("arbitrary",),
                                             disable_bounds_checks=True),
        name="moe",
    )(sorted_n, tile_exp, tile_nvalid, tile_base, h2,
      lw["exp_w_gu"], lw["exp_b_gu"], lw["exp_w_down"], lw["exp_b_down"])


def _route(idx):
    flat = idx.T.reshape(-1)
    ids = jnp.arange(N_ROWS, dtype=jnp.int32)
    sorted_n = jnp.concatenate([jnp.sort(flat * N_ROWS + ids) % N_ROWS, jnp.zeros((TM_MOE,), jnp.int32)])
    experts = jnp.arange(N_EXP, dtype=jnp.int32)
    counts = jnp.sum((flat[None, :] == experts[:, None]).astype(jnp.int32), axis=1)
    padded = ((counts + TM_MOE - 1) // TM_MOE) * TM_MOE
    pad_ends = jnp.cumsum(padded)
    pad_starts = pad_ends - padded
    unp_starts = jnp.cumsum(counts) - counts
    tile_start = jnp.arange(NT_MOE, dtype=jnp.int32) * TM_MOE
    tile_exp = jnp.minimum(jnp.sum((pad_ends[None, :] <= tile_start[:, None]).astype(jnp.int32), axis=1), N_EXP - 1)
    onehot = (tile_exp[:, None] == experts[None, :]).astype(jnp.int32)
    pick = lambda v: jnp.sum(onehot * v[None, :], axis=1)
    offset = tile_start - pick(pad_starts)
    tile_nvalid = jnp.where(tile_start < pad_ends[-1], jnp.clip(pick(counts) - offset, 0, TM_MOE), 0)
    tile_base = jnp.clip(pick(unp_starts) + offset, 0, N_ROWS - 1)
    return sorted_n, tile_exp, tile_nvalid.astype(jnp.int32), tile_base.astype(jnp.int32)


def _final_body(x_ref, y0_ref, y1_ref, y2_ref, y3_ref, gw_ref, mod_ref, g_ref, o_ref):
    g2 = mod_ref[0][:, 5 * D:6 * D]
    o_ref[...] = _rms(x_ref[...] + g2 * _sum_slots((y0_ref, y1_ref, y2_ref, y3_ref), gw_ref)) * g_ref[...]


def _final(x1, moe, gw, mods_l, g, tile0, ntiles):
    tok = pl.BlockSpec((TM, D), lambda i: (tile0 + i, 0))
    return pl.pallas_call(
        _final_body,
        grid=(ntiles,),
        in_specs=[tok] + _slot_specs(tile0) + [
                  pl.BlockSpec((TM, LANES), lambda i: (tile0 + i, 0)),
                  pl.BlockSpec((1, 1, 6 * D), lambda i: (_mod_row(tile0 + i), 0, 0)), _const_spec((1, D))],
        out_specs=pl.BlockSpec((TM, D), lambda i: (i, 0)),
        out_shape=jax.ShapeDtypeStruct((ntiles * TM, D), F32),
        name="final_norm",
    )(x1, *([moe] * TOP_K), gw, mods_l, g)


def _rope_perm(w):
    parts = []
    for a in range(2):
        x1, x2 = w[..., a * 16:a * 16 + 8], w[..., a * 16 + 8:a * 16 + 16]
        parts += [-x2, x1]
    return jnp.concatenate(parts, axis=-1)


def _rope_tables():
    t = np.arange(S_LAT)
    freqs = ROPE_BASE ** (-np.arange(8, dtype=np.float64) / 8)
    cs, sn = [], []
    for pos in (t // GRID_W, t % GRID_W):
        ang = pos.astype(np.float64)[:, None] * freqs[None, :]
        cs += [np.cos(ang), np.cos(ang)]
        sn += [np.sin(ang), np.sin(ang)]
    cos = np.concatenate([np.ones((ID_ROWS, 32), np.float32), np.concatenate(cs, -1).astype(np.float32)], 0)
    sin = np.concatenate([np.zeros((ID_ROWS, 32), np.float32), np.concatenate(sn, -1).astype(np.float32)], 0)
    n = cos.shape[0]
    one, zero = np.ones((n, 1), np.float32), np.zeros((n, 1), np.float32)
    cq_h = np.concatenate([np.repeat(one, 64, 1), cos, np.repeat(zero, 32, 1)], 1)
    sq_h = np.concatenate([np.repeat(zero, 64, 1), sin, np.repeat(zero, 32, 1)], 1)
    cseg = np.concatenate([cos, np.repeat(one, 96, 1)], 1)
    sseg = np.concatenate([sin, np.repeat(zero, 96, 1)], 1)
    return dict(cq=jnp.asarray(np.tile(cq_h, (1, 4))), sq=jnp.asarray(np.tile(sq_h, (1, 4))),
                cs=jnp.asarray(cseg), ss=jnp.asarray(sseg))


def _na_bias_table(rpb):
    qc = np.arange(GRID_W)[:, None]
    kc = np.arange(GRID_W)[None, :]
    cs = np.clip(qc - NA_WIN_W // 2, 0, GRID_W - NA_WIN_W)
    valid = (kc >= cs) & (kc < cs + NA_WIN_W)
    pad = GRID_W - NA_WIN_W
    ext = jnp.pad(rpb, ((0, 0), (0, 0), (pad, pad)))
    toe = jnp.stack([ext[:, :, GRID_W - 1 - q:2 * GRID_W - 1 - q] for q in range(GRID_W)], axis=2)
    toe = jnp.where(valid[None, None], toe, NEG)
    tab = jnp.stack([toe[:, o:o + NA_WIN_H] for o in range(NA_WIN_H)], axis=0)
    return tab.transpose(0, 1, 3, 2, 4).reshape(NA_WIN_H, NA_HEADS, GRID_W, NK_LOC).astype(F32)


def _layer_weights(l, p):
    w_in = p["w_in"][l]
    z = lambda n: jnp.zeros((D, n), F32)
    kr = w_in[:, 320:352]
    w_perm = jnp.concatenate([
        w_in[:, 352:1120], w_in[:, 1120:1248], w_in[:, 1248:1376], w_in[:, 1376:1632], w_in[:, 1632:1888],
        w_in[:, 1920:2176], w_in[:, 2176:2432], w_in[:, 192:320], w_in[:, 0:192], z(64),
        kr, w_in[:, 1888:1920], z(64), _rope_perm(kr), z(96)], axis=1).astype(BF16)
    wuq = p["mla_w_uq"][l].reshape(MLA_Q_LORA, MLA_HEADS, MLA_NOPE + MLA_ROPE)
    zq = jnp.zeros((MLA_Q_LORA, MLA_HEADS, 32), F32)
    wq1 = jnp.concatenate([wuq, zq], axis=-1).reshape(MLA_Q_LORA, 512)
    wq2 = jnp.concatenate([jnp.zeros((MLA_Q_LORA, MLA_HEADS, MLA_NOPE), F32), _rope_perm(wuq[..., MLA_NOPE:]), zq],
                          axis=-1).reshape(MLA_Q_LORA, 512)
    wq = jnp.pad(jnp.concatenate([wq1, wq2], axis=1), ((0, 256 - MLA_Q_LORA), (0, 0))).astype(BF16)
    qg = jnp.pad(p["mla_q_norm_g"][l], (0, 256 - MLA_Q_LORA)).reshape(1, 256)
    wukv = p["mla_w_ukv"][l].reshape(MLA_KV_LORA, MLA_HEADS, MLA_NOPE + MLA_V)
    wk = jnp.concatenate([wukv[..., :MLA_NOPE], jnp.zeros((MLA_KV_LORA, MLA_HEADS, 64), F32)], -1).reshape(MLA_KV_LORA, 512)
    wv = wukv[..., MLA_NOPE:].reshape(MLA_KV_LORA, 256)
    eye = jnp.eye(MLA_ROPE, dtype=F32)[:, None, :]
    ek = jnp.concatenate([jnp.zeros((MLA_ROPE, MLA_HEADS, MLA_NOPE), F32),
                          jnp.broadcast_to(eye, (MLA_ROPE, MLA_HEADS, MLA_ROPE)),
                          jnp.zeros((MLA_ROPE, MLA_HEADS, 32), F32)], -1).reshape(MLA_ROPE, 512)
    wkv = jnp.concatenate([
        jnp.concatenate([wk, wv], axis=1),
        jnp.concatenate([ek, jnp.zeros((MLA_ROPE, 256), F32)], axis=1),
        jnp.zeros((LANES - MLA_ROPE, 768), F32)], axis=0).astype(BF16)
    wg = jnp.zeros((2, LANES, LANES), F32)
    for d in range(2):
        wg = wg.at[d, SEG_GA + d * GLA_RANK:SEG_GA + (d + 1) * GLA_RANK].set(p["gla_w_gate"][l, d])
    def bdiag(w):
        out = jnp.zeros((LRU_W, LRU_W), F32)
        nb = LRU_W // LRU_BLOCKS
        for n in range(LRU_BLOCKS):
            out = out.at[n * nb:(n + 1) * nb, n * nb:(n + 1) * nb].set(w[n])
        return out
    lru_wg = jnp.concatenate([bdiag(p["lru_w_a"][l, 0]), bdiag(p["lru_w_a"][l, 1]),
                              bdiag(p["lru_w_x"][l, 0]), bdiag(p["lru_w_x"][l, 1])], axis=1).astype(BF16)
    lru_bg = jnp.concatenate([p["lru_b_a"][l, 0], p["lru_b_a"][l, 1], p["lru_b_x"][l, 0], p["lru_b_x"][l, 1]]).reshape(1, 4 * LRU_W)
    rw = jnp.pad(p["router_w"][l], ((0, 0), (0, LANES - N_EXP)))
    rw_hi = rw.astype(BF16)
    rw_lo = (rw - rw_hi.astype(F32)).astype(BF16)
    rb = jnp.concatenate([p["router_b"][l], jnp.full((LANES - N_EXP,), -jnp.inf, F32)]).reshape(1, LANES)
    return dict(
        w_in=w_perm, norm1_g=p["norm1_g"][l].reshape(1, D), norm2_g=p["norm2_g"][l].reshape(1, D),
        qg=qg, wq=wq, kvg=p["mla_kv_norm_g"][l].reshape(1, LANES), wkv=wkv,
        wg=wg.astype(BF16), bg=p["gla_b_gate"][l].reshape(2, 1, LANES), gla_norm_g=p["gla_norm_g"][l].reshape(1, 256),
        lru_conv_w=p["lru_conv_w"][l], lru_conv_b=p["lru_conv_b"][l].reshape(1, LRU_W), lru_wg=lru_wg, lru_bg=lru_bg,
        lru_lambda=p["lru_lambda"][l].reshape(2, 1, LRU_W),
        w_out=p["w_out"][l].astype(BF16), rw_hi=rw_hi, rw_lo=rw_lo, rb=rb,
        exp_w_gu=p["exp_w_gu"][l], exp_b_gu=p["exp_b_gu"][l].reshape(N_EXP, 1, 2 * D_EXP),
        exp_w_down=p["exp_w_down"][l], exp_b_down=p["exp_b_down"][l].reshape(N_EXP, 1, D),
        na_bias=_na_bias_table(p["na_rpb"][l]),
    )


def _state_to_blockdiag(s):
    eye = jnp.eye(GLA_HEADS, dtype=s.dtype)
    return jnp.einsum("bdhkv,hg->bdhvgk", s, eye).reshape(s.shape[0], 2, 256, LANES)


def _blockdiag_to_state(st):
    b = st.shape[0]
    s6 = st.reshape(b, 2, GLA_HEADS, GLA_DV, GLA_HEADS, GLA_DK)
    diag = jnp.stack([s6[:, :, h, :, h, :] for h in range(GLA_HEADS)], axis=2)
    return diag.transpose(0, 1, 2, 4, 3)


def kernel(x_prompt, x_sample, cache_mla_ckv, cache_mla_krope, cache_na_k, cache_na_v, state_gla, state_lru, c, c_ctx, norm1_g, norm2_g, w_mod, b_mod, w_in, w_out, mla_q_norm_g, mla_w_uq, mla_kv_norm_g, mla_w_ukv, na_rpb, gla_w_gate, gla_b_gate, gla_norm_g, lru_conv_w, lru_conv_b, lru_w_a, lru_b_a, lru_w_x, lru_b_x, lru_lambda, router_w, router_b, exp_w_gu, exp_b_gu, exp_w_down, exp_b_down, final_norm_g):
    p = dict(norm1_g=norm1_g, norm2_g=norm2_g, w_in=w_in, w_out=w_out, mla_q_norm_g=mla_q_norm_g, mla_w_uq=mla_w_uq,
             mla_kv_norm_g=mla_kv_norm_g, mla_w_ukv=mla_w_ukv, na_rpb=na_rpb, gla_w_gate=gla_w_gate,
             gla_b_gate=gla_b_gate, gla_norm_g=gla_norm_g, lru_conv_w=lru_conv_w, lru_conv_b=lru_conv_b,
             lru_w_a=lru_w_a, lru_b_a=lru_b_a, lru_w_x=lru_w_x, lru_b_x=lru_b_x, lru_lambda=lru_lambda,
             router_w=router_w, router_b=router_b, exp_w_gu=exp_w_gu, exp_b_gu=exp_b_gu, exp_w_down=exp_w_down,
             exp_b_down=exp_b_down)
    x = jnp.concatenate([x_prompt.reshape(T_CTX, D), x_sample.reshape(T_LAT, D)], axis=0)
    crow = jnp.concatenate([c_ctx[None, :], c, jnp.zeros((8 - 1 - B_LAT, D), F32)], axis=0)
    mods = _modulation(crow, w_mod, b_mod).reshape(DEPTH, 8, 1, 6 * D)
    tabs = _rope_tables()

    col = [[] for _ in range(6)]
    moe = None
    x1 = None
    for l in range(DEPTH):
        lw = _layer_weights(l, p)
        if l == 0:
            x, big, seg, ckv, q, k, v = _proj(True, x, None, None, None, mods[l], lw, tabs)
        else:
            x, big, seg, ckv, q, k, v = _proj(False, x1, moe, gw, mods[l - 1], mods[l], lw, tabs)
        kr_c = jnp.pad(cache_mla_krope[:, l].reshape(B_LAT * PAST, MLA_ROPE), ((0, 0), (0, LANES - MLA_ROPE)))
        kc, vc = _cache_kv(cache_mla_ckv[:, l].reshape(B_LAT * PAST, MLA_KV_LORA), kr_c, lw["wkv"])
        oa_c, ob_c = _ctx_attn(q, k, v, big)
        oa_l = _lat_mla(q, k, v, kc, vc)
        ob_l = _lat_na(big, cache_na_k[:, l].reshape(B_LAT * PAST, 256), cache_na_v[:, l].reshape(B_LAT * PAST, 256),
                       lw["na_bias"])
        oc_c, sg = _gla(big, seg, lw, None)
        oc_l, _ = _gla(big, seg, lw, _state_to_blockdiag(state_gla[:, l]))
        od_c, sl = _lru(big, lw, None)
        od_l, _ = _lru(big, lw, state_lru[:, l].reshape(B_LAT, 2, 1, LRU_W))
        x1, h2, idx, gw = _merge(x, (oa_c, oa_l, ob_c, ob_l, oc_c, oc_l, od_c, od_l), mods[l], lw)
        moe = _moe(*_route(idx[:, :TOP_K]), h2, lw)
        col[0].append(ckv[:T_CTX].reshape(B_CTX, S_CTX, MLA_KV_LORA))
        col[1].append(seg[:T_CTX, :MLA_ROPE].reshape(B_CTX, S_CTX, MLA_ROPE))
        col[2].append(big[:T_CTX, C_NAK:C_NAK + 256].reshape(B_CTX, S_CTX, NA_HEADS, NA_DH))
        col[3].append(big[:T_CTX, C_NAV:C_NAV + 256].reshape(B_CTX, S_CTX, NA_HEADS, NA_DH))
        col[4].append(_blockdiag_to_state(sg))
        col[5].append(sl.reshape(B_CTX, 2, LRU_W))
    fg = final_norm_g.reshape(1, D)
    y_prompt = _final(x1, moe, gw, mods[DEPTH - 1], fg, 0, NT_CTX).reshape(B_CTX, S_CTX, D)
    y_sample = _final(x1, moe, gw, mods[DEPTH - 1], fg, NT_CTX, NT_ALL - NT_CTX).reshape(B_LAT, S_LAT, D)
    return (y_prompt, y_sample) + tuple(jnp.stack(cl, axis=1) for cl in col)
```

```python
import functools

import numpy as np
import jax
import jax.numpy as jnp
from jax import lax
from jax.experimental import pallas as pl
from jax.experimental.pallas import tpu as pltpu

F32 = jnp.float32
BF16 = jnp.bfloat16
HIGHEST = lax.Precision.HIGHEST

D = 1024
DEPTH = 2
B_CTX, S_CTX = 32, 256
B_LAT, S_LAT = 4, 2048
PAST = 512
T_CTX = B_CTX * S_CTX
T_LAT = B_LAT * S_LAT
T_ALL = T_CTX + T_LAT
GRID_W = 64
ROWS = S_LAT // GRID_W

MLA_HEADS, MLA_NOPE, MLA_ROPE, MLA_V = 4, 64, 32, 64
MLA_Q_LORA, MLA_KV_LORA = 192, 128
MLA_SCALE = (MLA_NOPE + MLA_ROPE) ** -0.5
NA_HEADS, NA_DH, NA_WIN_H, NA_WIN_W = 4, 64, 8, 16
NA_SCALE = NA_DH ** -0.5
GLA_HEADS, GLA_DK, GLA_DV, GLA_RANK, GLA_NORM, GLA_CHUNK = 4, 32, 64, 16, 16.0, 64
GLA_SUB = 16
LRU_W, LRU_BLOCKS, LRU_C = 256, 4, 8.0
N_EXP, TOP_K, D_EXP = 32, 4, 1024
SWIGLU_ALPHA, SWIGLU_LIMIT = 1.702, 7.0
ROPE_BASE = 10000.0
EPS = 1e-6
NEG = -1e30

LANES = 128
VMEM_BYTES = 64 * 1024 * 1024

TM = 256
NT_CTX = T_CTX // TM
NT_ALL = T_ALL // TM
TILES_PER_LAT = S_LAT // TM
ID_ROWS = 2 * TM

TM_MOE = 512
N_ROWS = T_ALL * TOP_K
R_MAX = N_ROWS + N_EXP * TM_MOE
NT_MOE = R_MAX // TM_MOE
EXP_CHUNK = 256
N_DUMP = 2 * TM_MOE
SUB = D // LANES
assert T_ALL & (T_ALL - 1) == 0

C_NAQ, C_NAK, C_NAV, C_GQ, C_GK, C_GV, C_GG, C_LX, C_LG = 0, 256, 512, 768, 896, 1024, 1280, 1536, 1792
BIG = 2048
C_PKV, C_PQ, C_SEGA, C_SEGB = 2048, 2176, 2432, 2560
NC_IN = 2688
SEG_GA = 32


def _cp(vmem_mb, sem=None):
    kw = dict(vmem_limit_bytes=vmem_mb * 1024 * 1024)
    if sem is not None:
        kw["dimension_semantics"] = sem
    return pltpu.CompilerParams(**kw)


def _rms(x):
    return x * lax.rsqrt(jnp.mean(x * x, axis=-1, keepdims=True) + EPS)


def _dot(a, b, **kw):
    return jnp.dot(a, b, preferred_element_type=F32, **kw)


def _dot_nt(a, b):
    return lax.dot_general(a, b, (((1,), (1,)), ((), ())), preferred_element_type=F32)


def _dot_tn(a, b):
    return lax.dot_general(a, b, (((0,), (0,)), ((), ())), preferred_element_type=F32)


def _const_spec(shape):
    n = len(shape)
    return pl.BlockSpec(shape, lambda *_: (0,) * n)


def _mod_body(c_ref, w_ref, b_ref, o_ref):
    c = c_ref[...]
    s = c * jax.nn.sigmoid(c)
    o_ref[0] = _dot(s.astype(BF16), w_ref[0].astype(BF16)) + b_ref[0]


def _modulation(crow, w_mod, b_mod):
    tn = 1536
    return pl.pallas_call(
        _mod_body,
        grid=(DEPTH, 6 * D // tn),
        in_specs=[
            pl.BlockSpec((8, D), lambda l, j: (0, 0)),
            pl.BlockSpec((1, D, tn), lambda l, j: (l, 0, j)),
            pl.BlockSpec((1, 1, tn), lambda l, j: (l, 0, j)),
        ],
        out_specs=pl.BlockSpec((1, 8, tn), lambda l, j: (l, 0, j)),
        out_shape=jax.ShapeDtypeStruct((DEPTH, 8, 6 * D), F32),
        compiler_params=_cp(40),
        name="modulation",
    )(crow, w_mod, b_mod.reshape(DEPTH, 1, 6 * D))


def _mod_row(i):
    return jnp.where(i < NT_CTX, 0, 1 + (i - NT_CTX) // TILES_PER_LAT)


def _pos_block(i):
    return jnp.where(i < NT_CTX, 0, ID_ROWS // TM + (i - NT_CTX) % TILES_PER_LAT)


def _sum_slots(y_refs, gw_ref):
    gw = gw_ref[...]
    cols = [gw[:, kk:kk + 1] for kk in range(TOP_K)]
    parts = []
    for s in range(SUB):
        acc = cols[0] * y_refs[0][pl.ds(s, TM, stride=SUB), :]
        for kk in range(1, TOP_K):
            acc = acc + cols[kk] * y_refs[kk][pl.ds(s, TM, stride=SUB), :]
        parts.append(acc)
    return jnp.concatenate(parts, axis=-1)


def _slot_specs(tile0):
    return [pl.BlockSpec((TM * SUB, LANES), functools.partial(lambda kk, i: (kk * NT_ALL + tile0 + i, 0), kk))
            for kk in range(TOP_K)]


def _proj_body(first, *refs):
    if first:
        (x_ref, mod_ref, g1_ref, win_ref, qg_ref, wq_ref, kvg_ref, wkv_ref, cq_ref, sq_ref, cs_ref, ss_ref,
         big_ref, seg_ref, ckv_ref, q_ref, k_ref, v_ref) = refs
        x = x_ref[...]
    else:
        (x_ref, y0_ref, y1_ref, y2_ref, y3_ref, gw_ref, modp_ref, mod_ref, g1_ref, win_ref, qg_ref, wq_ref, kvg_ref, wkv_ref, cq_ref, sq_ref,
         cs_ref, ss_ref, xo_ref, big_ref, seg_ref, ckv_ref, q_ref, k_ref, v_ref) = refs
        g2 = modp_ref[0][:, 5 * D:6 * D]
        x = x_ref[...] + g2 * _sum_slots((y0_ref, y1_ref, y2_ref, y3_ref), gw_ref)
        xo_ref[...] = x
    m = mod_ref[0]
    sh1, sc1 = m[:, 0:D], m[:, D:2 * D]
    h = _rms(x) * g1_ref[...] * (1.0 + sc1) + sh1
    p = _dot(h.astype(BF16), win_ref[...])
    big_ref[...] = p[:, :BIG]
    ckv = _rms(p[:, C_PKV:C_PKV + MLA_KV_LORA]) * kvg_ref[...]
    ckv_ref[...] = ckv
    pq = p[:, C_PQ:C_PQ + 256]
    qn = pq * lax.rsqrt(jnp.sum(pq * pq, axis=-1, keepdims=True) * (1.0 / MLA_Q_LORA) + EPS) * qg_ref[...]
    a = _dot(qn.astype(BF16), wq_ref[...])
    q_ref[...] = (a[:, :512] * cq_ref[...] + a[:, 512:] * sq_ref[...]).astype(BF16)
    seg = p[:, C_SEGA:C_SEGA + LANES] * cs_ref[...] + p[:, C_SEGB:C_SEGB + LANES] * ss_ref[...]
    seg_ref[...] = seg
    kv = _dot(jnp.concatenate([ckv, seg], axis=-1).astype(BF16), wkv_ref[...])
    k_ref[...] = kv[:, :512].astype(BF16)
    v_ref[...] = kv[:, 512:].astype(BF16)


def _proj(first, x, moe, gw, mods_prev, mods_l, lw, tabs):
    tok = lambda w: pl.BlockSpec((TM, w), lambda i: (i, 0))
    modspec = pl.BlockSpec((1, 1, 6 * D), lambda i: (_mod_row(i), 0, 0))
    pos = lambda w: pl.BlockSpec((TM, w), lambda i: (_pos_block(i), 0))
    in_specs = [tok(D)]
    args = [x]
    if not first:
        in_specs += _slot_specs(0) + [tok(LANES), modspec]
        args += [moe] * TOP_K + [gw, mods_prev]
    in_specs += [modspec, _const_spec((1, D)), _const_spec((D, NC_IN)), _const_spec((1, 256)),
                 _const_spec((256, 1024)), _const_spec((1, LANES)), _const_spec((256, 768)),
                 pos(512), pos(512), pos(LANES), pos(LANES)]
    args += [mods_l, lw["norm1_g"], lw["w_in"], lw["qg"], lw["wq"], lw["kvg"], lw["wkv"],
             tabs["cq"], tabs["sq"], tabs["cs"], tabs["ss"]]
    out_specs = [tok(BIG), tok(LANES), tok(LANES), tok(512), tok(512), tok(256)]
    out_shape = [jax.ShapeDtypeStruct((T_ALL, BIG), F32), jax.ShapeDtypeStruct((T_ALL, LANES), F32),
                 jax.ShapeDtypeStruct((T_ALL, LANES), F32), jax.ShapeDtypeStruct((T_ALL, 512), BF16),
                 jax.ShapeDtypeStruct((T_ALL, 512), BF16), jax.ShapeDtypeStruct((T_ALL, 256), BF16)]
    if not first:
        out_specs = [tok(D)] + out_specs
        out_shape = [jax.ShapeDtypeStruct((T_ALL, D), F32)] + out_shape
    outs = pl.pallas_call(
        functools.partial(_proj_body, first),
        grid=(NT_ALL,),
        in_specs=in_specs,
        out_specs=out_specs,
        out_shape=out_shape,
        compiler_params=_cp(48),
        name="proj",
    )(*args)
    if first:
        return (x,) + tuple(outs)
    return tuple(outs)


def _kvc_body(ckv_ref, kr_ref, wkv_ref, k_ref, v_ref):
    kv = _dot(jnp.concatenate([ckv_ref[...], kr_ref[...]], axis=-1).astype(BF16), wkv_ref[...])
    k_ref[...] = kv[:, :512].astype(BF16)
    v_ref[...] = kv[:, 512:].astype(BF16)


def _cache_kv(ckv_c, kr_c, wkv):
    n = B_LAT * PAST
    return pl.pallas_call(
        _kvc_body,
        grid=(B_LAT,),
        in_specs=[pl.BlockSpec((PAST, LANES), lambda b: (b, 0)), pl.BlockSpec((PAST, LANES), lambda b: (b, 0)),
                  _const_spec((256, 768))],
        out_specs=[pl.BlockSpec((PAST, 512), lambda b: (b, 0)), pl.BlockSpec((PAST, 256), lambda b: (b, 0))],
        out_shape=[jax.ShapeDtypeStruct((n, 512), BF16), jax.ShapeDtypeStruct((n, 256), BF16)],
        name="cache_kv",
    )(ckv_c, kr_c, wkv)


def _softmax_pv(scores, values):
    m = scores[0].max(axis=-1, keepdims=True)
    for s in scores[1:]:
        m = jnp.maximum(m, s.max(axis=-1, keepdims=True))
    den = 0.0
    acc = 0.0
    for s, v in zip(scores, values):
        e = jnp.exp(s - m)
        den = den + e.sum(axis=-1, keepdims=True)
        acc = acc + _dot(e.astype(BF16), v)
    return acc / den


def _lane_lo(shape):
    return lax.broadcasted_iota(jnp.int32, shape, len(shape) - 1) < 64


def _ctx_attn_body(q_ref, k_ref, v_ref, na_ref, oa_ref, ob_ref):
    lo = _lane_lo((S_CTX, LANES))
    for pp in range(2):
        vp = v_ref[:, pp * LANES:(pp + 1) * LANES]
        outs = []
        for h in (2 * pp, 2 * pp + 1):
            s = _dot_nt(q_ref[:, h * LANES:(h + 1) * LANES], k_ref[:, h * LANES:(h + 1) * LANES]) * MLA_SCALE
            outs.append(_softmax_pv([s], [vp]))
        oa_ref[:, pp * LANES:(pp + 1) * LANES] = jnp.where(lo, outs[0], outs[1])
    for pp in range(2):
        qp = na_ref[:, C_NAQ + pp * LANES:C_NAQ + (pp + 1) * LANES]
        kp = na_ref[:, C_NAK + pp * LANES:C_NAK + (pp + 1) * LANES].astype(BF16)
        vp = na_ref[:, C_NAV + pp * LANES:C_NAV + (pp + 1) * LANES].astype(BF16)
        outs = []
        for sel in (lo, ~lo):
            qh = jnp.where(sel, qp, 0.0).astype(BF16)
            outs.append(_softmax_pv([_dot_nt(qh, kp) * NA_SCALE], [vp]))
        ob_ref[:, pp * LANES:(pp + 1) * LANES] = jnp.where(lo, outs[0], outs[1])


def _ctx_attn(q, k, v, big):
    blk = lambda w: pl.BlockSpec((S_CTX, w), lambda b: (b, 0))
    return pl.pallas_call(
        _ctx_attn_body,
        grid=(B_CTX,),
        in_specs=[blk(512), blk(512), blk(256), blk(768)],
        out_specs=[blk(256), blk(256)],
        out_shape=[jax.ShapeDtypeStruct((T_CTX, 256), F32), jax.ShapeDtypeStruct((T_CTX, 256), F32)],
        name="ctx_attn",
    )(q, k, v, big)


TQ = 256


def _lat_mla_body(q_ref, kl_ref, vl_ref, kc_ref, vc_ref, o_ref):
    lo = _lane_lo((TQ, LANES))
    for pp in range(2):
        vlp = vl_ref[:, pp * LANES:(pp + 1) * LANES]
        vcp = vc_ref[:, pp * LANES:(pp + 1) * LANES]
        outs = []
        for h in (2 * pp, 2 * pp + 1):
            qh = q_ref[:, h * LANES:(h + 1) * LANES]
            s_l = _dot_nt(qh, kl_ref[:, h * LANES:(h + 1) * LANES]) * MLA_SCALE
            s_c = _dot_nt(qh, kc_ref[:, h * LANES:(h + 1) * LANES]) * MLA_SCALE
            outs.append(_softmax_pv([s_l, s_c], [vlp, vcp]))
        o_ref[:, pp * LANES:(pp + 1) * LANES] = jnp.where(lo, outs[0], outs[1])


def _lat_mla(q, k, v, kc, vc):
    nq = S_LAT // TQ
    qrow = lambda b, t: (T_CTX // TQ + b * nq + t, 0)
    lat = lambda b, t: (T_CTX // S_LAT + b, 0)
    return pl.pallas_call(
        _lat_mla_body,
        grid=(B_LAT, nq),
        in_specs=[pl.BlockSpec((TQ, 512), qrow), pl.BlockSpec((S_LAT, 512), lat), pl.BlockSpec((S_LAT, 256), lat),
                  pl.BlockSpec((PAST, 512), lambda b, t: (b, 0)), pl.BlockSpec((PAST, 256), lambda b, t: (b, 0))],
        out_specs=pl.BlockSpec((TQ, 256), lambda b, t: (b * nq + t, 0)),
        out_shape=jax.ShapeDtypeStruct((T_LAT, 256), F32),
        compiler_params=_cp(48),
        name="lat_mla",
    )(q, k, v, kc, vc)


NK_LOC = NA_WIN_H * GRID_W


def _na_row_start(r):
    return jnp.clip(r - NA_WIN_H // 2, 0, ROWS - NA_WIN_H)


def _lat_na_body(q_ref, k_ref, v_ref, kc_ref, vc_ref, bias_ref, o_ref):
    r = pl.program_id(1)
    k0 = pl.multiple_of(_na_row_start(r) * GRID_W, GRID_W)
    lo = _lane_lo((GRID_W, LANES))
    for pp in range(2):
        cols = slice(pp * LANES, (pp + 1) * LANES)
        qp = q_ref[:, cols]
        kl = k_ref[pl.ds(k0, NK_LOC), cols].astype(BF16)
        vl = v_ref[pl.ds(k0, NK_LOC), cols].astype(BF16)
        kc = kc_ref[:, cols].astype(BF16)
        vc = vc_ref[:, cols].astype(BF16)
        outs = []
        for j, sel in enumerate((lo, ~lo)):
            qh = jnp.where(sel, qp, 0.0).astype(BF16)
            s_loc = _dot_nt(qh, kl) * NA_SCALE + bias_ref[0, 2 * pp + j]
            s_ctx = _dot_nt(qh, kc) * NA_SCALE
            outs.append(_softmax_pv([s_loc, s_ctx], [vl, vc]))
        o_ref[:, cols] = jnp.where(lo, outs[0], outs[1])


def _lat_na(big, kc, vc, bias):
    qrow = lambda b, r: (T_CTX // GRID_W + b * ROWS + r, 0)
    boff = lambda b, r: (_na_row_start(r) - r + NA_WIN_H - 1, 0, 0, 0)
    lat = lambda c: (lambda b, r: (T_CTX // S_LAT + b, c))
    return pl.pallas_call(
        _lat_na_body,
        grid=(B_LAT, ROWS),
        in_specs=[pl.BlockSpec((GRID_W, 256), qrow), pl.BlockSpec((S_LAT, 256), lat(1)),
                  pl.BlockSpec((S_LAT, 256), lat(2)),
                  pl.BlockSpec((PAST, 256), lambda b, r: (b, 0)), pl.BlockSpec((PAST, 256), lambda b, r: (b, 0)),
                  pl.BlockSpec((1, NA_HEADS, GRID_W, NK_LOC), boff)],
        out_specs=pl.BlockSpec((GRID_W, 256), lambda b, r: (b * ROWS + r, 0)),
        out_shape=jax.ShapeDtypeStruct((T_LAT, 256), F32),
        compiler_params=_cp(48),
        name="lat_na",
    )(big, big, big, kc, vc, bias)


def _log_sigmoid(x):
    return jnp.minimum(x, 0.0) - jnp.log1p(jnp.exp(-jnp.abs(x)))


def _gla_chunk(rev, q, k, v, la, st):
    C, SB = GLA_CHUNK, GLA_SUB
    ri = lax.broadcasted_iota(jnp.int32, (C, C), 0)
    ci = lax.broadcasted_iota(jnp.int32, (C, C), 1)
    tri = ((ci >= ri) if rev else (ci <= ri)).astype(F32)
    cb = _dot(tri, la, precision=HIGHEST)
    cl = cb[0:1] if rev else cb[C - 1:C]
    o = _dot_nt((q * jnp.exp(cb)).astype(BF16), st.astype(BF16))
    kd = (k * jnp.exp(cl - cb)).astype(BF16)
    bd = (lax.broadcasted_iota(jnp.int32, (256, LANES), 0) // GLA_DV
          == lax.broadcasted_iota(jnp.int32, (256, LANES), 1) // GLA_DK)
    st_new = st * jnp.exp(cl) + jnp.where(bd, _dot_tn(v.astype(BF16), kd), 0.0)

    vb = v.astype(BF16)
    lane_h = lax.broadcasted_iota(jnp.int32, (SB, LANES), 1) // GLA_DK
    col_h = lax.broadcasted_iota(jnp.int32, (SB, 256), 1) // GLA_DV
    expand = (lax.broadcasted_iota(jnp.int32, (LANES, 256), 0) // GLA_DK
              == lax.broadcasted_iota(jnp.int32, (LANES, 256), 1) // GLA_DV).astype(BF16)
    row = lax.broadcasted_iota(jnp.int32, (C, LANES), 0)
    pj = lax.broadcasted_iota(jnp.int32, (SB * SB, LANES), 0) // SB
    pi = lax.broadcasted_iota(jnp.int32, (SB * SB, LANES), 0) % SB
    pmask = (pi <= pj) if rev else (pi >= pj)
    nsub = C // SB
    parts = []
    for i in range(nsub):
        sl = slice(i * SB, (i + 1) * SB)
        qi, ki, cbi, vi = q[sl], k[sl], cb[sl], v[sl]
        acc = o[sl]
        has_off = (i < nsub - 1) if rev else (i > 0)
        if has_off:
            if rev:
                ref = cb[(i + 1) * SB:(i + 1) * SB + 1]
                outside = row >= (i + 1) * SB
            else:
                ref = cb[i * SB - 1:i * SB]
                outside = row < i * SB
            a = qi * jnp.exp(cbi - ref)
            bm = jnp.where(outside, k * jnp.exp(jnp.minimum(ref - cb, 0.0)), 0.0).astype(BF16)
            a4 = jnp.concatenate([jnp.where(lane_h == h, a, 0.0) for h in range(GLA_HEADS)], axis=0)
            att = _dot_nt(a4.astype(BF16), bm)
            pv = _dot(att.astype(BF16), vb)
            for h in range(GLA_HEADS):
                acc = acc + jnp.where(col_h == h, pv[h * SB:(h + 1) * SB], 0.0)
        kj = jnp.broadcast_to(ki[:, None, :], (SB, SB, LANES)).reshape(SB * SB, LANES)
        cbj = jnp.broadcast_to(cbi[:, None, :], (SB, SB, LANES)).reshape(SB * SB, LANES)
        qq = jnp.broadcast_to(qi[None, :, :], (SB, SB, LANES)).reshape(SB * SB, LANES)
        cbq = jnp.broadcast_to(cbi[None, :, :], (SB, SB, LANES)).reshape(SB * SB, LANES)
        x = qq * kj * jnp.where(pmask, jnp.exp(jnp.where(pmask, cbq - cbj, 0.0)), 0.0)
        y = _dot(x.astype(BF16), expand).reshape(SB, SB, 256)
        acc = acc + jnp.sum(y * vi[:, None, :], axis=0)
        parts.append(acc)
    return jnp.concatenate(parts, axis=0), st_new


def _gla_body(S, has_s0, *refs):
    if has_s0:
        (gq_ref, gk_ref, gv_ref, gg_ref, seg_ref, wg_ref, bg_ref, gn_ref, s0_ref,
         o_ref, sfin_ref, la_scr, of_scr, ob_scr, st_scr) = refs
    else:
        (gq_ref, gk_ref, gv_ref, gg_ref, seg_ref, wg_ref, bg_ref, gn_ref,
         o_ref, sfin_ref, la_scr, of_scr, ob_scr, st_scr) = refs
    nc = S // GLA_CHUNK
    seg = seg_ref[...].astype(BF16)
    for d in range(2):
        la_scr[d] = _log_sigmoid(_dot(seg, wg_ref[d]) + bg_ref[d]) * (1.0 / GLA_NORM)
        if has_s0:
            st_scr[d] = s0_ref[0, d]
        else:
            st_scr[d] = jnp.zeros((256, LANES), F32)

    def body(c, carry):
        for d, out_scr in ((0, of_scr), (1, ob_scr)):
            r0 = pl.multiple_of((c if d == 0 else nc - 1 - c) * GLA_CHUNK, GLA_CHUNK)
            rows = pl.ds(r0, GLA_CHUNK)
            q = gq_ref[rows, :] * (GLA_DK ** -0.5)
            o, st = _gla_chunk(d == 1, q, gk_ref[rows, :], gv_ref[rows, :], la_scr[d, rows, :], st_scr[d])
            out_scr[rows, :] = o
            st_scr[d] = st
        return carry

    lax.fori_loop(0, nc, body, 0)
    sfin_ref[0] = st_scr[...]
    o = of_scr[...] + ob_scr[...]
    ones = (lax.broadcasted_iota(jnp.int32, (256, 256), 0) // GLA_DV
            == lax.broadcasted_iota(jnp.int32, (256, 256), 1) // GLA_DV).astype(F32)
    ms = _dot(o * o, ones, precision=HIGHEST) * (1.0 / GLA_DV)
    g = gg_ref[...]
    o_ref[...] = o * lax.rsqrt(ms + EPS) * gn_ref[...] * (g * jax.nn.sigmoid(g))


def _gla(big, seg, lw, s0):
    has_s0 = s0 is not None
    S, nb, base = (S_LAT, B_LAT, T_CTX // S_LAT) if has_s0 else (S_CTX, B_CTX, 0)
    blk = lambda w, c: pl.BlockSpec((S, w), lambda b: (base + b, c))
    in_specs = [blk(LANES, C_GQ // LANES), blk(LANES, C_GK // LANES), blk(256, C_GV // 256), blk(256, C_GG // 256),
                blk(LANES, 0), _const_spec((2, LANES, LANES)), _const_spec((2, 1, LANES)), _const_spec((1, 256))]
    args = [big, big, big, big, seg, lw["wg"], lw["bg"], lw["gla_norm_g"]]
    if has_s0:
        in_specs += [pl.BlockSpec((1, 2, 256, LANES), lambda b: (b, 0, 0, 0))]
        args += [s0]
    return pl.pallas_call(
        functools.partial(_gla_body, S, has_s0),
        grid=(nb,),
        in_specs=in_specs,
        out_specs=[pl.BlockSpec((S, 256), lambda b: (b, 0)), pl.BlockSpec((1, 2, 256, LANES), lambda b: (b, 0, 0, 0))],
        out_shape=[jax.ShapeDtypeStruct((nb * S, 256), F32), jax.ShapeDtypeStruct((nb, 2, 256, LANES), F32)],
        scratch_shapes=[pltpu.VMEM((2, S, LANES), F32), pltpu.VMEM((S, 256), F32), pltpu.VMEM((S, 256), F32),
                        pltpu.VMEM((2, 256, LANES), F32)],
        compiler_params=_cp(48),
        name="gla_lat" if has_s0 else "gla_ctx",
    )(*args)


def _softplus(x):
    return jnp.maximum(x, 0.0) + jnp.log1p(jnp.exp(-jnp.abs(x)))


def _lru_body(S, has_h0, *refs):
    if has_h0:
        (lx_ref, lg_ref, cw_ref, cb_ref, wg_ref, bgate_ref, lam_ref, h0_ref,
         o_ref, hfin_ref, a_scr, b_scr, hf_scr, hb_scr) = refs
    else:
        (lx_ref, lg_ref, cw_ref, cb_ref, wg_ref, bgate_ref, lam_ref,
         o_ref, hfin_ref, a_scr, b_scr, hf_scr, hb_scr) = refs
    x = lx_ref[...]
    row = lax.broadcasted_iota(jnp.int32, (S, LRU_W), 0)
    xm1 = jnp.where(row >= 1, pltpu.roll(x, 1, 0), 0.0)
    xp1 = jnp.where(row < S - 1, pltpu.roll(x, S - 1, 0), 0.0)
    xp2 = jnp.where(row < S - 2, pltpu.roll(x, S - 2, 0), 0.0)
    xc = cb_ref[...] + xm1 * cw_ref[0:1] + x * cw_ref[1:2] + xp1 * cw_ref[2:3] + xp2 * cw_ref[3:4]
    gates = _dot(xc.astype(BF16), wg_ref[...]) + bgate_ref[...]
    for d in range(2):
        r = jax.nn.sigmoid(gates[:, d * LRU_W:(d + 1) * LRU_W])
        ig = jax.nn.sigmoid(gates[:, (2 + d) * LRU_W:(3 + d) * LRU_W])
        log_a = -LRU_C * r * _softplus(-lam_ref[d])
        a_scr[d] = jnp.exp(log_a)
        b_scr[d] = jnp.sqrt(1.0 - jnp.exp(2.0 * log_a)) * (ig * xc)
    if has_h0:
        h_init = (h0_ref[0, 0], h0_ref[0, 1])
    else:
        h_init = (jnp.zeros((1, LRU_W), F32), jnp.zeros((1, LRU_W), F32))

    def body(i, carry):
        hf, hb = carry
        r0 = pl.multiple_of(i * 8, 8)
        rb = pl.multiple_of(S - 8 - i * 8, 8)
        af, bf = a_scr[0, pl.ds(r0, 8), :], b_scr[0, pl.ds(r0, 8), :]
        ab, bb = a_scr[1, pl.ds(rb, 8), :], b_scr[1, pl.ds(rb, 8), :]
        fs, bs = [], [None] * 8
        for t in range(8):
            hf = af[t:t + 1] * hf + bf[t:t + 1]
            fs.append(hf)
            u = 7 - t
            hb = ab[u:u + 1] * hb + bb[u:u + 1]
            bs[u] = hb
        hf_scr[pl.ds(r0, 8), :] = jnp.concatenate(fs, axis=0)
        hb_scr[pl.ds(rb, 8), :] = jnp.concatenate(bs, axis=0)
        return hf, hb

    hf, hb = lax.fori_loop(0, S // 8, body, h_init)
    hfin_ref[0, 0] = hf
    hfin_ref[0, 1] = hb
    g = lg_ref[...]
    gelu = g * (0.5 * (1.0 + jnp.tanh(np.sqrt(2.0 / np.pi).astype(np.float32) * (g + 0.044715 * (g * g * g)))))
    o_ref[...] = (hf_scr[...] + hb_scr[...]) * gelu


def _lru(big, lw, h0):
    has_h0 = h0 is not None
    S, nb, base = (S_LAT, B_LAT, T_CTX // S_LAT) if has_h0 else (S_CTX, B_CTX, 0)
    blk = lambda c: pl.BlockSpec((S, LRU_W), lambda b: (base + b, c))
    in_specs = [blk(C_LX // LRU_W), blk(C_LG // LRU_W), _const_spec((4, LRU_W)), _const_spec((1, LRU_W)),
                _const_spec((LRU_W, 4 * LRU_W)), _const_spec((1, 4 * LRU_W)), _const_spec((2, 1, LRU_W))]
    args = [big, big, lw["lru_conv_w"], lw["lru_conv_b"], lw["lru_wg"], lw["lru_bg"], lw["lru_lambda"]]
    if has_h0:
        in_specs += [pl.BlockSpec((1, 2, 1, LRU_W), lambda b: (b, 0, 0, 0))]
        args += [h0]
    return pl.pallas_call(
        functools.partial(_lru_body, S, has_h0),
        grid=(nb,),
        in_specs=in_specs,
        out_specs=[pl.BlockSpec((S, LRU_W), lambda b: (b, 0)),
                   pl.BlockSpec((1, 2, 1, LRU_W), lambda b: (b, 0, 0, 0))],
        out_shape=[jax.ShapeDtypeStruct((nb * S, LRU_W), F32), jax.ShapeDtypeStruct((nb, 2, 1, LRU_W), F32)],
        scratch_shapes=[pltpu.VMEM((2, S, LRU_W), F32), pltpu.VMEM((2, S, LRU_W), F32),
                        pltpu.VMEM((S, LRU_W), F32), pltpu.VMEM((S, LRU_W), F32)],
        compiler_params=_cp(48),
        name="lru_lat" if has_h0 else "lru_ctx",
    )(*args)


def _merge_body(x_ref, *refs):
    (mod_ref, wout_ref, g2_ref, rwh_ref, rwl_ref, rb_ref, x1_ref, h2_ref, idx_ref, gw_ref) = refs[8:]
    is_ctx = pl.program_id(0) < NT_CTX
    mix = 0.0
    for j in range(4):
        o = jnp.where(is_ctx, refs[2 * j][...], refs[2 * j + 1][...])
        mix = mix + _dot(o.astype(BF16), wout_ref[j * 256:(j + 1) * 256, :])
    m = mod_ref[0]
    g1, sh2, sc2 = m[:, 2 * D:3 * D], m[:, 3 * D:4 * D], m[:, 4 * D:5 * D]
    x1 = x_ref[...] + g1 * mix
    x1_ref[...] = x1
    h2 = _rms(x1) * g2_ref[...] * (1.0 + sc2) + sh2
    for s in range(SUB):
        h2_ref[pl.ds(s, TM, stride=SUB), :] = h2[:, s * LANES:(s + 1) * LANES]
    hi = h2.astype(BF16)
    lo = (h2 - hi.astype(F32)).astype(BF16)
    logits = _dot(hi, rwh_ref[...]) + _dot(hi, rwl_ref[...]) + _dot(lo, rwh_ref[...]) + rb_ref[...]
    lane = lax.broadcasted_iota(jnp.int32, (TM, LANES), 1).astype(F32)
    idx_out = jnp.zeros((TM, LANES), F32)
    val_out = jnp.zeros((TM, LANES), F32)
    top = None
    den = 0.0
    for kk in range(TOP_K):
        mval = logits.max(axis=-1, keepdims=True)
        midx = jnp.where(logits == mval, lane, float(LANES)).min(axis=-1, keepdims=True)
        if kk == 0:
            top = mval
        e = jnp.exp(mval - top)
        den = den + e
        idx_out = jnp.where(lane == kk, midx, idx_out)
        val_out = jnp.where(lane == kk, e, val_out)
        logits = jnp.where(lane == midx, -jnp.inf, logits)
    idx_ref[...] = idx_out.astype(jnp.int32)
    gw_ref[...] = val_out / den


def _merge(x, mix_parts, mods_l, lw):
    tok = lambda w: pl.BlockSpec((TM, w), lambda i: (i, 0))
    ctx = pl.BlockSpec((TM, 256), lambda i: (jnp.minimum(i, NT_CTX - 1), 0))
    lat = pl.BlockSpec((TM, 256), lambda i: (jnp.maximum(i - NT_CTX, 0), 0))
    return pl.pallas_call(
        _merge_body,
        grid=(NT_ALL,),
        in_specs=[tok(D)] + [ctx, lat] * 4 + [
                  pl.BlockSpec((1, 1, 6 * D), lambda i: (_mod_row(i), 0, 0)),
                  _const_spec((D, D)), _const_spec((1, D)), _const_spec((D, LANES)), _const_spec((D, LANES)),
                  _const_spec((1, LANES))],
        out_specs=[tok(D), pl.BlockSpec((TM * SUB, LANES), lambda i: (i, 0)), tok(LANES), tok(LANES)],
        out_shape=[jax.ShapeDtypeStruct((T_ALL, D), F32), jax.ShapeDtypeStruct((T_ALL * SUB, LANES), F32),
                   jax.ShapeDtypeStruct((T_ALL, LANES), jnp.int32), jax.ShapeDtypeStruct((T_ALL, LANES), F32)],
        compiler_params=_cp(40),
        name="merge",
    )(x, *mix_parts, mods_l, lw["w_out"], lw["norm2_g"], lw["rw_hi"], lw["rw_lo"], lw["rb"])


def _moe_body(sn_ref, te_ref, nv_ref, tb_ref, h2_hbm, wgu_ref, bgu_ref, wd_ref, bd_ref,
              y_hbm, xbuf, obuf, wgu_bf, wd_bf, gsem, ssem):
    i = pl.program_id(0)
    slot = i % 2
    cur = jnp.minimum(i, NT_MOE - 1)
    valid = (i < NT_MOE) & (nv_ref[cur] > 0)
    prv = jnp.maximum(i - 1, 0)
    nv_prev = jnp.where(i >= 1, nv_ref[prv], 0)
    base_prev = tb_ref[prv]
    base_next = tb_ref[jnp.minimum(i + 1, NT_MOE - 1)]

    def gather_row(base, sl, r):
        tok = jnp.bitwise_and(sn_ref[base + r], T_ALL - 1)
        pltpu.make_async_copy(h2_hbm.at[pl.ds(pl.multiple_of(tok * SUB, SUB), SUB)],
                              xbuf.at[sl, pl.ds(pl.multiple_of(r * SUB, SUB), SUB)], gsem.at[sl]).start()

    def scatter_row(sl, r):
        dst = jnp.where(r < nv_prev, sn_ref[base_prev + r], N_ROWS + sl * TM_MOE + r)
        pltpu.make_async_copy(obuf.at[sl, pl.ds(pl.multiple_of(r * SUB, SUB), SUB)],
                              y_hbm.at[pl.ds(pl.multiple_of(dst * SUB, SUB), SUB)], ssem.at[sl]).start()

    def for_other_slot(fn, lo=0, hi=TM_MOE):
        for other in range(2):
            @pl.when(slot == 1 - other)
            def _():
                lax.fori_loop(lo, hi, lambda r, c: fn(other, r) or c, 0, unroll=8)

    def gather_done(sl):
        pltpu.make_async_copy(h2_hbm.at[pl.ds(0, TM_MOE * SUB)], xbuf.at[sl], gsem.at[sl]).wait()

    def scatter_done(sl):
        pltpu.make_async_copy(obuf.at[sl], y_hbm.at[pl.ds(0, TM_MOE * SUB)], ssem.at[sl]).wait()

    @pl.when(i == 0)
    def _():
        obuf[1] = jnp.zeros((TM_MOE * SUB, LANES), F32)
        cp = pltpu.make_async_copy(obuf.at[1], y_hbm.at[pl.ds(N_ROWS * SUB, TM_MOE * SUB)], ssem.at[1])
        cp.start()
        cp.wait()

        def body(r, c):
            gather_row(tb_ref[0], 0, r)
            return c
        lax.fori_loop(0, TM_MOE, body, 0, unroll=8)

    e = te_ref[cur]
    prev_e = te_ref[prv]

    @pl.when(valid & ((i == 0) | (e != prev_e)))
    def _():
        wgu_bf[...] = wgu_ref[0, 0].astype(BF16)
        wd_bf[...] = wd_ref[0, 0].astype(BF16)

    @pl.when(valid)
    def _():
        def both(other, r):
            gather_row(base_next, other, r)
            scatter_row(other, r)
        for_other_slot(both)
        gather_done(slot)
        x = jnp.concatenate([xbuf[slot, pl.ds(s, TM_MOE, stride=SUB), :] for s in range(SUB)], axis=-1).astype(BF16)
        acc = jnp.zeros((TM_MOE, D), F32)
        for j in range(D_EXP // EXP_CHUNK):
            c0 = j * EXP_CHUNK
            xg = _dot(x, wgu_bf[:, c0:c0 + EXP_CHUNK]) + bgu_ref[0, 0][:, c0:c0 + EXP_CHUNK]
            xl = _dot(x, wgu_bf[:, D_EXP + c0:D_EXP + c0 + EXP_CHUNK]) + bgu_ref[0, 0][:, D_EXP + c0:D_EXP + c0 + EXP_CHUNK]
            xg = jnp.minimum(xg, SWIGLU_LIMIT)
            xl = jnp.clip(xl, -SWIGLU_LIMIT, SWIGLU_LIMIT)
            act = xg * jax.nn.sigmoid(SWIGLU_ALPHA * xg) * (xl + 1.0)
            acc = acc + _dot(act.astype(BF16), wd_bf[c0:c0 + EXP_CHUNK, :])
        out = acc + bd_ref[0, 0]
        for s in range(SUB):
            obuf[slot, pl.ds(s, TM_MOE, stride=SUB), :] = out[:, s * LANES:(s + 1) * LANES]
        scatter_done(1 - slot)

    @pl.when(jnp.logical_not(valid) & (nv_prev > 0))
    def _():
        gather_done(slot)
        for_other_slot(scatter_row)
        scatter_done(1 - slot)


def _moe(sorted_n, tile_exp, tile_nvalid, tile_base, h2, l, p):
    wmap = lambda i, sn, te, nv, tb: (l, te[jnp.minimum(i, NT_MOE - 1)], 0, 0)
    grid_spec = pltpu.PrefetchScalarGridSpec(
        num_scalar_prefetch=4,
        grid=(NT_MOE + 1,),
        in_specs=[
            pl.BlockSpec(memory_space=pl.ANY),
            pl.BlockSpec((1, 1, D, 2 * D_EXP), wmap),
            pl.BlockSpec((1, 1, 1, 2 * D_EXP), wmap),
            pl.BlockSpec((1, 1, D_EXP, D), wmap),
            pl.BlockSpec((1, 1, 1, D), wmap),
        ],
        out_specs=pl.BlockSpec(memory_space=pl.ANY),
        scratch_shapes=[pltpu.VMEM((2, TM_MOE * SUB, LANES), F32), pltpu.VMEM((2, TM_MOE * SUB, LANES), F32),
                        pltpu.VMEM((D, 2 * D_EXP), BF16), pltpu.VMEM((D_EXP, D), BF16),
                        pltpu.SemaphoreType.DMA((2,)), pltpu.SemaphoreType.DMA((2,))],
    )
    return pl.pallas_call(
        _moe_body,
        grid_spec=grid_spec,
        out_shape=jax.ShapeDtypeStruct(((N_ROWS + N_DUMP) * SUB, LANES), F32),
        compiler_params=pltpu.CompilerParams(vmem_limit_bytes=56 * 1024 * 1024, dimension_semantics=("arbitrary",),
                                             disable_bounds_checks=True),
        name="moe",
    )(sorted_n, tile_exp, tile_nvalid, tile_base, h2,
      p["exp_w_gu"], p["exp_b_gu"].reshape(DEPTH, N_EXP, 1, 2 * D_EXP),
      p["exp_w_down"], p["exp_b_down"].reshape(DEPTH, N_EXP, 1, D))


def _route(idx):
    flat = idx.T.reshape(-1)
    ids = jnp.arange(N_ROWS, dtype=jnp.int32)
    sorted_n = jnp.concatenate([jnp.sort(flat * N_ROWS + ids) % N_ROWS, jnp.zeros((TM_MOE,), jnp.int32)])
    experts = jnp.arange(N_EXP, dtype=jnp.int32)
    counts = jnp.sum((flat[None, :] == experts[:, None]).astype(jnp.int32), axis=1)
    padded = ((counts + TM_MOE - 1) // TM_MOE) * TM_MOE
    pad_ends = jnp.cumsum(padded)
    pad_starts = pad_ends - padded
    unp_starts = jnp.cumsum(counts) - counts
    tile_start = jnp.arange(NT_MOE, dtype=jnp.int32) * TM_MOE
    tile_exp = jnp.minimum(jnp.sum((pad_ends[None, :] <= tile_start[:, None]).astype(jnp.int32), axis=1), N_EXP - 1)
    onehot = (tile_exp[:, None] == experts[None, :]).astype(jnp.int32)
    pick = lambda v: jnp.sum(onehot * v[None, :], axis=1)
    offset = tile_start - pick(pad_starts)
    tile_nvalid = jnp.where(tile_start < pad_ends[-1], jnp.clip(pick(counts) - offset, 0, TM_MOE), 0)
    tile_base = jnp.clip(pick(unp_starts) + offset, 0, N_ROWS - 1)
    return sorted_n, tile_exp, tile_nvalid.astype(jnp.int32), tile_base.astype(jnp.int32)


def _final_body(x_ref, y0_ref, y1_ref, y2_ref, y3_ref, gw_ref, mod_ref, g_ref, o_ref):
    g2 = mod_ref[0][:, 5 * D:6 * D]
    o_ref[...] = _rms(x_ref[...] + g2 * _sum_slots((y0_ref, y1_ref, y2_ref, y3_ref), gw_ref)) * g_ref[...]


def _final(x1, moe, gw, mods_l, g, tile0, ntiles):
    tok = pl.BlockSpec((TM, D), lambda i: (tile0 + i, 0))
    return pl.pallas_call(
        _final_body,
        grid=(ntiles,),
        in_specs=[tok] + _slot_specs(tile0) + [
                  pl.BlockSpec((TM, LANES), lambda i: (tile0 + i, 0)),
                  pl.BlockSpec((1, 1, 6 * D), lambda i: (_mod_row(tile0 + i), 0, 0)), _const_spec((1, D))],
        out_specs=pl.BlockSpec((TM, D), lambda i: (i, 0)),
        out_shape=jax.ShapeDtypeStruct((ntiles * TM, D), F32),
        name="final_norm",
    )(x1, *([moe] * TOP_K), gw, mods_l, g)


def _rope_perm(w):
    parts = []
    for a in range(2):
        x1, x2 = w[..., a * 16:a * 16 + 8], w[..., a * 16 + 8:a * 16 + 16]
        parts += [-x2, x1]
    return jnp.concatenate(parts, axis=-1)


def _rope_tables():
    t = np.arange(S_LAT)
    freqs = ROPE_BASE ** (-np.arange(8, dtype=np.float64) / 8)
    cs, sn = [], []
    for pos in (t // GRID_W, t % GRID_W):
        ang = pos.astype(np.float64)[:, None] * freqs[None, :]
        cs += [np.cos(ang), np.cos(ang)]
        sn += [np.sin(ang), np.sin(ang)]
    cos = np.concatenate([np.ones((ID_ROWS, 32), np.float32), np.concatenate(cs, -1).astype(np.float32)], 0)
    sin = np.concatenate([np.zeros((ID_ROWS, 32), np.float32), np.concatenate(sn, -1).astype(np.float32)], 0)
    n = cos.shape[0]
    one, zero = np.ones((n, 1), np.float32), np.zeros((n, 1), np.float32)
    cq_h = np.concatenate([np.repeat(one, 64, 1), cos, np.repeat(zero, 32, 1)], 1)
    sq_h = np.concatenate([np.repeat(zero, 64, 1), sin, np.repeat(zero, 32, 1)], 1)
    cseg = np.concatenate([cos, np.repeat(one, 96, 1)], 1)
    sseg = np.concatenate([sin, np.repeat(zero, 96, 1)], 1)
    return dict(cq=jnp.asarray(np.tile(cq_h, (1, 4))), sq=jnp.asarray(np.tile(sq_h, (1, 4))),
                cs=jnp.asarray(cseg), ss=jnp.asarray(sseg))


def _na_bias_table(rpb):
    qc = np.arange(GRID_W)[:, None]
    kc = np.arange(GRID_W)[None, :]
    cs = np.clip(qc - NA_WIN_W // 2, 0, GRID_W - NA_WIN_W)
    valid = (kc >= cs) & (kc < cs + NA_WIN_W)
    pad = GRID_W - NA_WIN_W
    ext = jnp.pad(rpb, ((0, 0), (0, 0), (pad, pad)))
    toe = jnp.stack([ext[:, :, GRID_W - 1 - q:2 * GRID_W - 1 - q] for q in range(GRID_W)], axis=2)
    toe = jnp.where(valid[None, None], toe, NEG)
    tab = jnp.stack([toe[:, o:o + NA_WIN_H] for o in range(NA_WIN_H)], axis=0)
    return tab.transpose(0, 1, 3, 2, 4).reshape(NA_WIN_H, NA_HEADS, GRID_W, NK_LOC).astype(F32)


def _layer_weights(l, p):
    w_in = p["w_in"][l]
    z = lambda n: jnp.zeros((D, n), F32)
    kr = w_in[:, 320:352]
    w_perm = jnp.concatenate([
        w_in[:, 352:1120], w_in[:, 1120:1248], w_in[:, 1248:1376], w_in[:, 1376:1632], w_in[:, 1632:1888],
        w_in[:, 1920:2176], w_in[:, 2176:2432], w_in[:, 192:320], w_in[:, 0:192], z(64),
        kr, w_in[:, 1888:1920], z(64), _rope_perm(kr), z(96)], axis=1).astype(BF16)
    wuq = p["mla_w_uq"][l].reshape(MLA_Q_LORA, MLA_HEADS, MLA_NOPE + MLA_ROPE)
    zq = jnp.zeros((MLA_Q_LORA, MLA_HEADS, 32), F32)
    wq1 = jnp.concatenate([wuq, zq], axis=-1).reshape(MLA_Q_LORA, 512)
    wq2 = jnp.concatenate([jnp.zeros((MLA_Q_LORA, MLA_HEADS, MLA_NOPE), F32), _rope_perm(wuq[..., MLA_NOPE:]), zq],
                          axis=-1).reshape(MLA_Q_LORA, 512)
    wq = jnp.pad(jnp.concatenate([wq1, wq2], axis=1), ((0, 256 - MLA_Q_LORA), (0, 0))).astype(BF16)
    qg = jnp.pad(p["mla_q_norm_g"][l], (0, 256 - MLA_Q_LORA)).reshape(1, 256)
    wukv = p["mla_w_ukv"][l].reshape(MLA_KV_LORA, MLA_HEADS, MLA_NOPE + MLA_V)
    wk = jnp.concatenate([wukv[..., :MLA_NOPE], jnp.zeros((MLA_KV_LORA, MLA_HEADS, 64), F32)], -1).reshape(MLA_KV_LORA, 512)
    wv = wukv[..., MLA_NOPE:].reshape(MLA_KV_LORA, 256)
    eye = jnp.eye(MLA_ROPE, dtype=F32)[:, None, :]
    ek = jnp.concatenate([jnp.zeros((MLA_ROPE, MLA_HEADS, MLA_NOPE), F32),
                          jnp.broadcast_to(eye, (MLA_ROPE, MLA_HEADS, MLA_ROPE)),
                          jnp.zeros((MLA_ROPE, MLA_HEADS, 32), F32)], -1).reshape(MLA_ROPE, 512)
    wkv = jnp.concatenate([
        jnp.concatenate([wk, wv], axis=1),
        jnp.concatenate([ek, jnp.zeros((MLA_ROPE, 256), F32)], axis=1),
        jnp.zeros((LANES - MLA_ROPE, 768), F32)], axis=0).astype(BF16)
    wg = jnp.stack([jnp.pad(p["gla_w_gate"][l, d], ((SEG_GA + d * GLA_RANK, LANES - SEG_GA - (d + 1) * GLA_RANK), (0, 0)))
                    for d in range(2)])
    def bdiag(w):
        eye = jnp.eye(LRU_BLOCKS, dtype=F32)
        return (w[:, :, None, :] * eye[:, None, :, None]).reshape(LRU_W, LRU_W)
    lru_wg = jnp.concatenate([bdiag(p["lru_w_a"][l, 0]), bdiag(p["lru_w_a"][l, 1]),
                              bdiag(p["lru_w_x"][l, 0]), bdiag(p["lru_w_x"][l, 1])], axis=1).astype(BF16)
    lru_bg = jnp.concatenate([p["lru_b_a"][l, 0], p["lru_b_a"][l, 1], p["lru_b_x"][l, 0], p["lru_b_x"][l, 1]]).reshape(1, 4 * LRU_W)
    rw = jnp.pad(p["router_w"][l], ((0, 0), (0, LANES - N_EXP)))
    rw_hi = rw.astype(BF16)
    rw_lo = (rw - rw_hi.astype(F32)).astype(BF16)
    rb = jnp.concatenate([p["router_b"][l], jnp.full((LANES - N_EXP,), -jnp.inf, F32)]).reshape(1, LANES)
    return dict(
        w_in=w_perm, norm1_g=p["norm1_g"][l].reshape(1, D), norm2_g=p["norm2_g"][l].reshape(1, D),
        qg=qg, wq=wq, kvg=p["mla_kv_norm_g"][l].reshape(1, LANES), wkv=wkv,
        wg=wg.astype(BF16), bg=p["gla_b_gate"][l].reshape(2, 1, LANES), gla_norm_g=p["gla_norm_g"][l].reshape(1, 256),
        lru_conv_w=p["lru_conv_w"][l], lru_conv_b=p["lru_conv_b"][l].reshape(1, LRU_W), lru_wg=lru_wg, lru_bg=lru_bg,
        lru_lambda=p["lru_lambda"][l].reshape(2, 1, LRU_W),
        w_out=p["w_out"][l].astype(BF16), rw_hi=rw_hi, rw_lo=rw_lo, rb=rb,
        na_bias=_na_bias_table(p["na_rpb"][l]),
    )


def _state_to_blockdiag(s):
    eye = jnp.eye(GLA_HEADS, dtype=s.dtype)
    return jnp.einsum("bdhkv,hg->bdhvgk", s, eye).reshape(s.shape[0], 2, 256, LANES)


def _blockdiag_to_state(st):
    b = st.shape[0]
    s6 = st.reshape(b, 2, GLA_HEADS, GLA_DV, GLA_HEADS, GLA_DK)
    diag = jnp.stack([s6[:, :, h, :, h, :] for h in range(GLA_HEADS)], axis=2)
    return diag.transpose(0, 1, 2, 4, 3)


def kernel(x_prompt, x_sample, cache_mla_ckv, cache_mla_krope, cache_na_k, cache_na_v, state_gla, state_lru, c, c_ctx, norm1_g, norm2_g, w_mod, b_mod, w_in, w_out, mla_q_norm_g, mla_w_uq, mla_kv_norm_g, mla_w_ukv, na_rpb, gla_w_gate, gla_b_gate, gla_norm_g, lru_conv_w, lru_conv_b, lru_w_a, lru_b_a, lru_w_x, lru_b_x, lru_lambda, router_w, router_b, exp_w_gu, exp_b_gu, exp_w_down, exp_b_down, final_norm_g):
    p = dict(norm1_g=norm1_g, norm2_g=norm2_g, w_in=w_in, w_out=w_out, mla_q_norm_g=mla_q_norm_g, mla_w_uq=mla_w_uq,
             mla_kv_norm_g=mla_kv_norm_g, mla_w_ukv=mla_w_ukv, na_rpb=na_rpb, gla_w_gate=gla_w_gate,
             gla_b_gate=gla_b_gate, gla_norm_g=gla_norm_g, lru_conv_w=lru_conv_w, lru_conv_b=lru_conv_b,
             lru_w_a=lru_w_a, lru_b_a=lru_b_a, lru_w_x=lru_w_x, lru_b_x=lru_b_x, lru_lambda=lru_lambda,
             router_w=router_w, router_b=router_b, exp_w_gu=exp_w_gu, exp_b_gu=exp_b_gu, exp_w_down=exp_w_down,
             exp_b_down=exp_b_down)
    x = jnp.concatenate([x_prompt.reshape(T_CTX, D), x_sample.reshape(T_LAT, D)], axis=0)
    crow = jnp.concatenate([c_ctx[None, :], c, jnp.zeros((8 - 1 - B_LAT, D), F32)], axis=0)
    mods = _modulation(crow, w_mod, b_mod).reshape(DEPTH, 8, 1, 6 * D)
    tabs = _rope_tables()

    col = [[] for _ in range(6)]
    moe = None
    x1 = None
    for l in range(DEPTH):
        lw = _layer_weights(l, p)
        if l == 0:
            x, big, seg, ckv, q, k, v = _proj(True, x, None, None, None, mods[l], lw, tabs)
        else:
            x, big, seg, ckv, q, k, v = _proj(False, x1, moe, gw, mods[l - 1], mods[l], lw, tabs)
        kr_c = jnp.pad(cache_mla_krope[:, l].reshape(B_LAT * PAST, MLA_ROPE), ((0, 0), (0, LANES - MLA_ROPE)))
        kc, vc = _cache_kv(cache_mla_ckv[:, l].reshape(B_LAT * PAST, MLA_KV_LORA), kr_c, lw["wkv"])
        oa_c, ob_c = _ctx_attn(q, k, v, big)
        oa_l = _lat_mla(q, k, v, kc, vc)
        ob_l = _lat_na(big, cache_na_k[:, l].reshape(B_LAT * PAST, 256), cache_na_v[:, l].reshape(B_LAT * PAST, 256),
                       lw["na_bias"])
        oc_c, sg = _gla(big, seg, lw, None)
        oc_l, _ = _gla(big, seg, lw, _state_to_blockdiag(state_gla[:, l]))
        od_c, sl = _lru(big, lw, None)
        od_l, _ = _lru(big, lw, state_lru[:, l].reshape(B_LAT, 2, 1, LRU_W))
        x1, h2, idx, gw = _merge(x, (oa_c, oa_l, ob_c, ob_l, oc_c, oc_l, od_c, od_l), mods[l], lw)
        moe = _moe(*_route(idx[:, :TOP_K]), h2, l, p)
        col[0].append(ckv[:T_CTX].reshape(B_CTX, S_CTX, MLA_KV_LORA))
        col[1].append(seg[:T_CTX, :MLA_ROPE].reshape(B_CTX, S_CTX, MLA_ROPE))
        col[2].append(big[:T_CTX, C_NAK:C_NAK + 256].reshape(B_CTX, S_CTX, NA_HEADS, NA_DH))
        col[3].append(big[:T_CTX, C_NAV:C_NAV + 256].reshape(B_CTX, S_CTX, NA_HEADS, NA_DH))
        col[4].append(_blockdiag_to_state(sg))
        col[5].append(sl.reshape(B_CTX, 2, LRU_W))
    fg = final_norm_g.reshape(1, D)
    y_prompt = _final(x1, moe, gw, mods[DEPTH - 1], fg, 0, NT_CTX).reshape(B_CTX, S_CTX, D)
    y_sample = _final(x1, moe, gw, mods[DEPTH - 1], fg, NT_CTX, NT_ALL - NT_CTX).reshape(B_LAT, S_LAT, D)
    return (y_prompt, y_sample) + tuple(jnp.stack(cl, axis=1) for cl in col)
```

```python
import functools

import numpy as np
import jax
import jax.numpy as jnp
from jax import lax
from jax.experimental import pallas as pl
from jax.experimental.pallas import tpu as pltpu

F32 = jnp.float32
BF16 = jnp.bfloat16
HIGHEST = lax.Precision.HIGHEST

D = 1024
DEPTH = 2
B_CTX, S_CTX = 32, 256
B_LAT, S_LAT = 4, 2048
PAST = 512
T_CTX = B_CTX * S_CTX
T_LAT = B_LAT * S_LAT
T_ALL = T_CTX + T_LAT
GRID_W = 64
ROWS = S_LAT // GRID_W

MLA_HEADS, MLA_NOPE, MLA_ROPE, MLA_V = 4, 64, 32, 64
MLA_Q_LORA, MLA_KV_LORA = 192, 128
MLA_SCALE = (MLA_NOPE + MLA_ROPE) ** -0.5
NA_HEADS, NA_DH, NA_WIN_H, NA_WIN_W = 4, 64, 8, 16
NA_SCALE = NA_DH ** -0.5
GLA_HEADS, GLA_DK, GLA_DV, GLA_RANK, GLA_NORM, GLA_CHUNK = 4, 32, 64, 16, 16.0, 64
GLA_SUB = 16
LRU_W, LRU_BLOCKS, LRU_C = 256, 4, 8.0
N_EXP, TOP_K, D_EXP = 32, 4, 1024
SWIGLU_ALPHA, SWIGLU_LIMIT = 1.702, 7.0
ROPE_BASE = 10000.0
EPS = 1e-6
NEG = -1e30

LANES = 128
VMEM_BYTES = 64 * 1024 * 1024

TM = 256
NT_CTX = T_CTX // TM
NT_ALL = T_ALL // TM
TILES_PER_LAT = S_LAT // TM
ID_ROWS = 2 * TM

TM_MOE = 512
N_ROWS = T_ALL * TOP_K
R_MAX = N_ROWS + N_EXP * TM_MOE
NT_MOE = R_MAX // TM_MOE
EXP_CHUNK = 256
N_DUMP = 2 * TM_MOE
SUB = D // LANES
assert T_ALL & (T_ALL - 1) == 0

C_NAQ, C_NAK, C_NAV, C_GQ, C_GK, C_GV, C_GG, C_LX, C_LG = 0, 256, 512, 768, 896, 1024, 1280, 1536, 1792
BIG = 2048
C_PKV, C_PQ, C_SEGA, C_SEGB = 2048, 2176, 2432, 2560
NC_IN = 2688
SEG_GA = 32


def _cp(vmem_mb, sem=None):
    kw = dict(vmem_limit_bytes=vmem_mb * 1024 * 1024)
    if sem is not None:
        kw["dimension_semantics"] = sem
    return pltpu.CompilerParams(**kw)


def _rms(x):
    return x * lax.rsqrt(jnp.mean(x * x, axis=-1, keepdims=True) + EPS)


def _dot(a, b, **kw):
    return jnp.dot(a, b, preferred_element_type=F32, **kw)


def _dot_nt(a, b):
    return lax.dot_general(a, b, (((1,), (1,)), ((), ())), preferred_element_type=F32)


def _dot_tn(a, b):
    return lax.dot_general(a, b, (((0,), (0,)), ((), ())), preferred_element_type=F32)


def _const_spec(shape):
    n = len(shape)
    return pl.BlockSpec(shape, lambda *_: (0,) * n)


def _mod_body(c_ref, w_ref, b_ref, o_ref):
    c = c_ref[...]
    s = c * jax.nn.sigmoid(c)
    o_ref[0] = _dot(s.astype(BF16), w_ref[0].astype(BF16)) + b_ref[0]


def _modulation(crow, w_mod, b_mod):
    tn = 1536
    return pl.pallas_call(
        _mod_body,
        grid=(DEPTH, 6 * D // tn),
        in_specs=[
            pl.BlockSpec((8, D), lambda l, j: (0, 0)),
            pl.BlockSpec((1, D, tn), lambda l, j: (l, 0, j)),
            pl.BlockSpec((1, 1, tn), lambda l, j: (l, 0, j)),
        ],
        out_specs=pl.BlockSpec((1, 8, tn), lambda l, j: (l, 0, j)),
        out_shape=jax.ShapeDtypeStruct((DEPTH, 8, 6 * D), F32),
        compiler_params=_cp(40),
        name="modulation",
    )(crow, w_mod, b_mod.reshape(DEPTH, 1, 6 * D))


def _mod_row(i):
    return jnp.where(i < NT_CTX, 0, 1 + (i - NT_CTX) // TILES_PER_LAT)


def _pos_block(i):
    return jnp.where(i < NT_CTX, 0, ID_ROWS // TM + (i - NT_CTX) % TILES_PER_LAT)


def _sum_slots(y_refs, gw_ref):
    gw = gw_ref[...]
    cols = [gw[:, kk:kk + 1] for kk in range(TOP_K)]
    parts = []
    for s in range(SUB):
        acc = cols[0] * y_refs[0][pl.ds(s, TM, stride=SUB), :]
        for kk in range(1, TOP_K):
            acc = acc + cols[kk] * y_refs[kk][pl.ds(s, TM, stride=SUB), :]
        parts.append(acc)
    return jnp.concatenate(parts, axis=-1)


def _slot_specs(tile0):
    return [pl.BlockSpec((TM * SUB, LANES), functools.partial(lambda kk, i: (kk * NT_ALL + tile0 + i, 0), kk))
            for kk in range(TOP_K)]


def _proj_body(first, *refs):
    if first:
        (x_ref, mod_ref, g1_ref, win_ref, qg_ref, wq_ref, kvg_ref, wkv_ref, cq_ref, sq_ref, cs_ref, ss_ref,
         big_ref, seg_ref, ckv_ref, q_ref, k_ref, v_ref) = refs
        x = x_ref[...]
    else:
        (x_ref, y0_ref, y1_ref, y2_ref, y3_ref, gw_ref, modp_ref, mod_ref, g1_ref, win_ref, qg_ref, wq_ref, kvg_ref, wkv_ref, cq_ref, sq_ref,
         cs_ref, ss_ref, xo_ref, big_ref, seg_ref, ckv_ref, q_ref, k_ref, v_ref) = refs
        g2 = modp_ref[0][:, 5 * D:6 * D]
        x = x_ref[...] + g2 * _sum_slots((y0_ref, y1_ref, y2_ref, y3_ref), gw_ref)
        xo_ref[...] = x
    m = mod_ref[0]
    sh1, sc1 = m[:, 0:D], m[:, D:2 * D]
    h = _rms(x) * g1_ref[...] * (1.0 + sc1) + sh1
    p = _dot(h.astype(BF16), win_ref[...])
    big_ref[...] = p[:, :BIG]
    ckv = _rms(p[:, C_PKV:C_PKV + MLA_KV_LORA]) * kvg_ref[...]
    ckv_ref[...] = ckv
    pq = p[:, C_PQ:C_PQ + 256]
    qn = pq * lax.rsqrt(jnp.sum(pq * pq, axis=-1, keepdims=True) * (1.0 / MLA_Q_LORA) + EPS) * qg_ref[...]
    a = _dot(qn.astype(BF16), wq_ref[...])
    q_ref[...] = (a[:, :512] * cq_ref[...] + a[:, 512:] * sq_ref[...]).astype(BF16)
    seg = p[:, C_SEGA:C_SEGA + LANES] * cs_ref[...] + p[:, C_SEGB:C_SEGB + LANES] * ss_ref[...]
    seg_ref[...] = seg
    kv = _dot(jnp.concatenate([ckv, seg], axis=-1).astype(BF16), wkv_ref[...])
    k_ref[...] = kv[:, :512].astype(BF16)
    v_ref[...] = kv[:, 512:].astype(BF16)


def _proj(first, x, moe, gw, mods_prev, mods_l, lw, tabs):
    tok = lambda w: pl.BlockSpec((TM, w), lambda i: (i, 0))
    modspec = pl.BlockSpec((1, 1, 6 * D), lambda i: (_mod_row(i), 0, 0))
    pos = lambda w: pl.BlockSpec((TM, w), lambda i: (_pos_block(i), 0))
    in_specs = [tok(D)]
    args = [x]
    if not first:
        in_specs += _slot_specs(0) + [tok(LANES), modspec]
        args += [moe] * TOP_K + [gw, mods_prev]
    in_specs += [modspec, _const_spec((1, D)), _const_spec((D, NC_IN)), _const_spec((1, 256)),
                 _const_spec((256, 1024)), _const_spec((1, LANES)), _const_spec((256, 768)),
                 pos(512), pos(512), pos(LANES), pos(LANES)]
    args += [mods_l, lw["norm1_g"], lw["w_in"], lw["qg"], lw["wq"], lw["kvg"], lw["wkv"],
             tabs["cq"], tabs["sq"], tabs["cs"], tabs["ss"]]
    out_specs = [tok(BIG), tok(LANES), tok(LANES), tok(512), tok(512), tok(256)]
    out_shape = [jax.ShapeDtypeStruct((T_ALL, BIG), F32), jax.ShapeDtypeStruct((T_ALL, LANES), F32),
                 jax.ShapeDtypeStruct((T_ALL, LANES), F32), jax.ShapeDtypeStruct((T_ALL, 512), BF16),
                 jax.ShapeDtypeStruct((T_ALL, 512), BF16), jax.ShapeDtypeStruct((T_ALL, 256), BF16)]
    if not first:
        out_specs = [tok(D)] + out_specs
        out_shape = [jax.ShapeDtypeStruct((T_ALL, D), F32)] + out_shape
    outs = pl.pallas_call(
        functools.partial(_proj_body, first),
        grid=(NT_ALL,),
        in_specs=in_specs,
        out_specs=out_specs,
        out_shape=out_shape,
        compiler_params=_cp(48),
        name="proj",
    )(*args)
    if first:
        return (x,) + tuple(outs)
    return tuple(outs)


def _kvc_body(ckv_ref, kr_ref, wkv_ref, k_ref, v_ref):
    kv = _dot(jnp.concatenate([ckv_ref[...], kr_ref[...]], axis=-1).astype(BF16), wkv_ref[...])
    k_ref[...] = kv[:, :512].astype(BF16)
    v_ref[...] = kv[:, 512:].astype(BF16)


def _cache_kv(ckv_c, kr_c, wkv):
    n = B_LAT * PAST
    return pl.pallas_call(
        _kvc_body,
        grid=(B_LAT,),
        in_specs=[pl.BlockSpec((PAST, LANES), lambda b: (b, 0)), pl.BlockSpec((PAST, LANES), lambda b: (b, 0)),
                  _const_spec((256, 768))],
        out_specs=[pl.BlockSpec((PAST, 512), lambda b: (b, 0)), pl.BlockSpec((PAST, 256), lambda b: (b, 0))],
        out_shape=[jax.ShapeDtypeStruct((n, 512), BF16), jax.ShapeDtypeStruct((n, 256), BF16)],
        name="cache_kv",
    )(ckv_c, kr_c, wkv)


def _softmax_pv(scores, values):
    m = scores[0].max(axis=-1, keepdims=True)
    for s in scores[1:]:
        m = jnp.maximum(m, s.max(axis=-1, keepdims=True))
    den = 0.0
    acc = 0.0
    for s, v in zip(scores, values):
        e = jnp.exp(s - m)
        den = den + e.sum(axis=-1, keepdims=True)
        acc = acc + _dot(e.astype(BF16), v)
    return acc / den


def _lane_lo(shape):
    return lax.broadcasted_iota(jnp.int32, shape, len(shape) - 1) < 64


def _ctx_attn_body(q_ref, k_ref, v_ref, na_ref, oa_ref, ob_ref):
    lo = _lane_lo((S_CTX, LANES))
    for pp in range(2):
        vp = v_ref[:, pp * LANES:(pp + 1) * LANES]
        outs = []
        for h in (2 * pp, 2 * pp + 1):
            s = _dot_nt(q_ref[:, h * LANES:(h + 1) * LANES], k_ref[:, h * LANES:(h + 1) * LANES]) * MLA_SCALE
            outs.append(_softmax_pv([s], [vp]))
        oa_ref[:, pp * LANES:(pp + 1) * LANES] = jnp.where(lo, outs[0], outs[1])
    for pp in range(2):
        qp = na_ref[:, C_NAQ + pp * LANES:C_NAQ + (pp + 1) * LANES]
        kp = na_ref[:, C_NAK + pp * LANES:C_NAK + (pp + 1) * LANES].astype(BF16)
        vp = na_ref[:, C_NAV + pp * LANES:C_NAV + (pp + 1) * LANES].astype(BF16)
        outs = []
        for sel in (lo, ~lo):
            qh = jnp.where(sel, qp, 0.0).astype(BF16)
            outs.append(_softmax_pv([_dot_nt(qh, kp) * NA_SCALE], [vp]))
        ob_ref[:, pp * LANES:(pp + 1) * LANES] = jnp.where(lo, outs[0], outs[1])


def _ctx_attn(q, k, v, big):
    blk = lambda w: pl.BlockSpec((S_CTX, w), lambda b: (b, 0))
    return pl.pallas_call(
        _ctx_attn_body,
        grid=(B_CTX,),
        in_specs=[blk(512), blk(512), blk(256), blk(768)],
        out_specs=[blk(256), blk(256)],
        out_shape=[jax.ShapeDtypeStruct((T_CTX, 256), F32), jax.ShapeDtypeStruct((T_CTX, 256), F32)],
        name="ctx_attn",
    )(q, k, v, big)


TQ = 256


def _lat_mla_body(q_ref, kl_ref, vl_ref, kc_ref, vc_ref, o_ref):
    lo = _lane_lo((TQ, LANES))
    for pp in range(2):
        vlp = vl_ref[:, pp * LANES:(pp + 1) * LANES]
        vcp = vc_ref[:, pp * LANES:(pp + 1) * LANES]
        outs = []
        for h in (2 * pp, 2 * pp + 1):
            qh = q_ref[:, h * LANES:(h + 1) * LANES]
            s_l = _dot_nt(qh, kl_ref[:, h * LANES:(h + 1) * LANES]) * MLA_SCALE
            s_c = _dot_nt(qh, kc_ref[:, h * LANES:(h + 1) * LANES]) * MLA_SCALE
            outs.append(_softmax_pv([s_l, s_c], [vlp, vcp]))
        o_ref[:, pp * LANES:(pp + 1) * LANES] = jnp.where(lo, outs[0], outs[1])


def _lat_mla(q, k, v, kc, vc):
    nq = S_LAT // TQ
    qrow = lambda b, t: (T_CTX // TQ + b * nq + t, 0)
    lat = lambda b, t: (T_CTX // S_LAT + b, 0)
    return pl.pallas_call(
        _lat_mla_body,
        grid=(B_LAT, nq),
        in_specs=[pl.BlockSpec((TQ, 512), qrow), pl.BlockSpec((S_LAT, 512), lat), pl.BlockSpec((S_LAT, 256), lat),
                  pl.BlockSpec((PAST, 512), lambda b, t: (b, 0)), pl.BlockSpec((PAST, 256), lambda b, t: (b, 0))],
        out_specs=pl.BlockSpec((TQ, 256), lambda b, t: (b * nq + t, 0)),
        out_shape=jax.ShapeDtypeStruct((T_LAT, 256), F32),
        compiler_params=_cp(48),
        name="lat_mla",
    )(q, k, v, kc, vc)


NK_LOC = NA_WIN_H * GRID_W


def _na_row_start(r):
    return jnp.clip(r - NA_WIN_H // 2, 0, ROWS - NA_WIN_H)


def _lat_na_body(q_ref, k_ref, v_ref, kc_ref, vc_ref, bias_ref, o_ref):
    r = pl.program_id(1)
    k0 = pl.multiple_of(_na_row_start(r) * GRID_W, GRID_W)
    lo = _lane_lo((GRID_W, LANES))
    for pp in range(2):
        cols = slice(pp * LANES, (pp + 1) * LANES)
        qp = q_ref[:, cols]
        kl = k_ref[pl.ds(k0, NK_LOC), cols].astype(BF16)
        vl = v_ref[pl.ds(k0, NK_LOC), cols].astype(BF16)
        kc = kc_ref[:, cols].astype(BF16)
        vc = vc_ref[:, cols].astype(BF16)
        outs = []
        for j, sel in enumerate((lo, ~lo)):
            qh = jnp.where(sel, qp, 0.0).astype(BF16)
            s_loc = _dot_nt(qh, kl) * NA_SCALE + bias_ref[0, 2 * pp + j]
            s_ctx = _dot_nt(qh, kc) * NA_SCALE
            outs.append(_softmax_pv([s_loc, s_ctx], [vl, vc]))
        o_ref[:, cols] = jnp.where(lo, outs[0], outs[1])


def _lat_na(big, kc, vc, bias):
    qrow = lambda b, r: (T_CTX // GRID_W + b * ROWS + r, 0)
    boff = lambda b, r: (_na_row_start(r) - r + NA_WIN_H - 1, 0, 0, 0)
    lat = lambda c: (lambda b, r: (T_CTX // S_LAT + b, c))
    return pl.pallas_call(
        _lat_na_body,
        grid=(B_LAT, ROWS),
        in_specs=[pl.BlockSpec((GRID_W, 256), qrow), pl.BlockSpec((S_LAT, 256), lat(1)),
                  pl.BlockSpec((S_LAT, 256), lat(2)),
                  pl.BlockSpec((PAST, 256), lambda b, r: (b, 0)), pl.BlockSpec((PAST, 256), lambda b, r: (b, 0)),
                  pl.BlockSpec((1, NA_HEADS, GRID_W, NK_LOC), boff)],
        out_specs=pl.BlockSpec((GRID_W, 256), lambda b, r: (b * ROWS + r, 0)),
        out_shape=jax.ShapeDtypeStruct((T_LAT, 256), F32),
        compiler_params=_cp(48),
        name="lat_na",
    )(big, big, big, kc, vc, bias)


def _log_sigmoid(x):
    return jnp.minimum(x, 0.0) - jnp.log1p(jnp.exp(-jnp.abs(x)))


def _gla_chunk(rev, q, k, v, la, st):
    C, SB = GLA_CHUNK, GLA_SUB
    ri = lax.broadcasted_iota(jnp.int32, (C, C), 0)
    ci = lax.broadcasted_iota(jnp.int32, (C, C), 1)
    tri = ((ci >= ri) if rev else (ci <= ri)).astype(F32)
    cb = _dot(tri, la, precision=HIGHEST)
    cl = cb[0:1] if rev else cb[C - 1:C]
    o = _dot_nt((q * jnp.exp(cb)).astype(BF16), st.astype(BF16))
    kd = (k * jnp.exp(cl - cb)).astype(BF16)
    bd = (lax.broadcasted_iota(jnp.int32, (256, LANES), 0) // GLA_DV
          == lax.broadcasted_iota(jnp.int32, (256, LANES), 1) // GLA_DK)
    st_new = st * jnp.exp(cl) + jnp.where(bd, _dot_tn(v.astype(BF16), kd), 0.0)

    vb = v.astype(BF16)
    lane_h = lax.broadcasted_iota(jnp.int32, (SB, LANES), 1) // GLA_DK
    col_h = lax.broadcasted_iota(jnp.int32, (SB, 256), 1) // GLA_DV
    expand = (lax.broadcasted_iota(jnp.int32, (LANES, 256), 0) // GLA_DK
              == lax.broadcasted_iota(jnp.int32, (LANES, 256), 1) // GLA_DV).astype(BF16)
    row = lax.broadcasted_iota(jnp.int32, (C, LANES), 0)
    pj = lax.broadcasted_iota(jnp.int32, (SB * SB, LANES), 0) // SB
    pi = lax.broadcasted_iota(jnp.int32, (SB * SB, LANES), 0) % SB
    pmask = (pi <= pj) if rev else (pi >= pj)
    nsub = C // SB
    parts = []
    for i in range(nsub):
        sl = slice(i * SB, (i + 1) * SB)
        qi, ki, cbi, vi = q[sl], k[sl], cb[sl], v[sl]
        acc = o[sl]
        has_off = (i < nsub - 1) if rev else (i > 0)
        if has_off:
            if rev:
                ref = cb[(i + 1) * SB:(i + 1) * SB + 1]
                outside = row >= (i + 1) * SB
            else:
                ref = cb[i * SB - 1:i * SB]
                outside = row < i * SB
            a = qi * jnp.exp(cbi - ref)
            bm = jnp.where(outside, k * jnp.exp(jnp.minimum(ref - cb, 0.0)), 0.0).astype(BF16)
            a4 = jnp.concatenate([jnp.where(lane_h == h, a, 0.0) for h in range(GLA_HEADS)], axis=0)
            att = _dot_nt(a4.astype(BF16), bm)
            pv = _dot(att.astype(BF16), vb)
            for h in range(GLA_HEADS):
                acc = acc + jnp.where(col_h == h, pv[h * SB:(h + 1) * SB], 0.0)
        kj = jnp.broadcast_to(ki[:, None, :], (SB, SB, LANES)).reshape(SB * SB, LANES)
        cbj = jnp.broadcast_to(cbi[:, None, :], (SB, SB, LANES)).reshape(SB * SB, LANES)
        qq = jnp.broadcast_to(qi[None, :, :], (SB, SB, LANES)).reshape(SB * SB, LANES)
        cbq = jnp.broadcast_to(cbi[None, :, :], (SB, SB, LANES)).reshape(SB * SB, LANES)
        x = qq * kj * jnp.where(pmask, jnp.exp(jnp.where(pmask, cbq - cbj, 0.0)), 0.0)
        y = _dot(x.astype(BF16), expand).reshape(SB, SB, 256)
        acc = acc + jnp.sum(y * vi[:, None, :], axis=0)
        parts.append(acc)
    return jnp.concatenate(parts, axis=0), st_new


def _gla_body(S, has_s0, *refs):
    if has_s0:
        (gq_ref, gk_ref, gv_ref, gg_ref, seg_ref, wg_ref, bg_ref, gn_ref, s0_ref,
         o_ref, sfin_ref, la_scr, of_scr, ob_scr, st_scr) = refs
    else:
        (gq_ref, gk_ref, gv_ref, gg_ref, seg_ref, wg_ref, bg_ref, gn_ref,
         o_ref, sfin_ref, la_scr, of_scr, ob_scr, st_scr) = refs
    nc = S // GLA_CHUNK
    seg = seg_ref[...].astype(BF16)
    for d in range(2):
        la_scr[d] = _log_sigmoid(_dot(seg, wg_ref[d]) + bg_ref[d]) * (1.0 / GLA_NORM)
        if has_s0:
            st_scr[d] = s0_ref[0, d]
        else:
            st_scr[d] = jnp.zeros((256, LANES), F32)

    def body(c, carry):
        for d, out_scr in ((0, of_scr), (1, ob_scr)):
            r0 = pl.multiple_of((c if d == 0 else nc - 1 - c) * GLA_CHUNK, GLA_CHUNK)
            rows = pl.ds(r0, GLA_CHUNK)
            q = gq_ref[rows, :] * (GLA_DK ** -0.5)
            o, st = _gla_chunk(d == 1, q, gk_ref[rows, :], gv_ref[rows, :], la_scr[d, rows, :], st_scr[d])
            out_scr[rows, :] = o
            st_scr[d] = st
        return carry

    lax.fori_loop(0, nc, body, 0)
    sfin_ref[0] = st_scr[...]
    o = of_scr[...] + ob_scr[...]
    ones = (lax.broadcasted_iota(jnp.int32, (256, 256), 0) // GLA_DV
            == lax.broadcasted_iota(jnp.int32, (256, 256), 1) // GLA_DV).astype(F32)
    ms = _dot(o * o, ones, precision=HIGHEST) * (1.0 / GLA_DV)
    g = gg_ref[...]
    o_ref[...] = o * lax.rsqrt(ms + EPS) * gn_ref[...] * (g * jax.nn.sigmoid(g))


def _gla(big, seg, lw, s0):
    has_s0 = s0 is not None
    S, nb, base = (S_LAT, B_LAT, T_CTX // S_LAT) if has_s0 else (S_CTX, B_CTX, 0)
    blk = lambda w, c: pl.BlockSpec((S, w), lambda b: (base + b, c))
    in_specs = [blk(LANES, C_GQ // LANES), blk(LANES, C_GK // LANES), blk(256, C_GV // 256), blk(256, C_GG // 256),
                blk(LANES, 0), _const_spec((2, LANES, LANES)), _const_spec((2, 1, LANES)), _const_spec((1, 256))]
    args = [big, big, big, big, seg, lw["wg"], lw["bg"], lw["gla_norm_g"]]
    if has_s0:
        in_specs += [pl.BlockSpec((1, 2, 256, LANES), lambda b: (b, 0, 0, 0))]
        args += [s0]
    return pl.pallas_call(
        functools.partial(_gla_body, S, has_s0),
        grid=(nb,),
        in_specs=in_specs,
        out_specs=[pl.BlockSpec((S, 256), lambda b: (b, 0)), pl.BlockSpec((1, 2, 256, LANES), lambda b: (b, 0, 0, 0))],
        out_shape=[jax.ShapeDtypeStruct((nb * S, 256), F32), jax.ShapeDtypeStruct((nb, 2, 256, LANES), F32)],
        scratch_shapes=[pltpu.VMEM((2, S, LANES), F32), pltpu.VMEM((S, 256), F32), pltpu.VMEM((S, 256), F32),
                        pltpu.VMEM((2, 256, LANES), F32)],
        compiler_params=_cp(48),
        name="gla_lat" if has_s0 else "gla_ctx",
    )(*args)


def _softplus(x):
    return jnp.maximum(x, 0.0) + jnp.log1p(jnp.exp(-jnp.abs(x)))


def _lru_body(S, has_h0, *refs):
    if has_h0:
        (lx_ref, lg_ref, cw_ref, cb_ref, wg_ref, bgate_ref, lam_ref, h0_ref,
         o_ref, hfin_ref, a_scr, b_scr, hf_scr, hb_scr) = refs
    else:
        (lx_ref, lg_ref, cw_ref, cb_ref, wg_ref, bgate_ref, lam_ref,
         o_ref, hfin_ref, a_scr, b_scr, hf_scr, hb_scr) = refs
    x = lx_ref[...]
    row = lax.broadcasted_iota(jnp.int32, (S, LRU_W), 0)
    xm1 = jnp.where(row >= 1, pltpu.roll(x, 1, 0), 0.0)
    xp1 = jnp.where(row < S - 1, pltpu.roll(x, S - 1, 0), 0.0)
    xp2 = jnp.where(row < S - 2, pltpu.roll(x, S - 2, 0), 0.0)
    xc = cb_ref[...] + xm1 * cw_ref[0:1] + x * cw_ref[1:2] + xp1 * cw_ref[2:3] + xp2 * cw_ref[3:4]
    gates = _dot(xc.astype(BF16), wg_ref[...]) + bgate_ref[...]
    for d in range(2):
        r = jax.nn.sigmoid(gates[:, d * LRU_W:(d + 1) * LRU_W])
        ig = jax.nn.sigmoid(gates[:, (2 + d) * LRU_W:(3 + d) * LRU_W])
        log_a = -LRU_C * r * _softplus(-lam_ref[d])
        a_scr[d] = jnp.exp(log_a)
        b_scr[d] = jnp.sqrt(1.0 - jnp.exp(2.0 * log_a)) * (ig * xc)
    if has_h0:
        h_init = (h0_ref[0, 0], h0_ref[0, 1])
    else:
        h_init = (jnp.zeros((1, LRU_W), F32), jnp.zeros((1, LRU_W), F32))

    def body(i, carry):
        hf, hb = carry
        r0 = pl.multiple_of(i * 8, 8)
        rb = pl.multiple_of(S - 8 - i * 8, 8)
        af, bf = a_scr[0, pl.ds(r0, 8), :], b_scr[0, pl.ds(r0, 8), :]
        ab, bb = a_scr[1, pl.ds(rb, 8), :], b_scr[1, pl.ds(rb, 8), :]
        fs, bs = [], [None] * 8
        for t in range(8):
            hf = af[t:t + 1] * hf + bf[t:t + 1]
            fs.append(hf)
            u = 7 - t
            hb = ab[u:u + 1] * hb + bb[u:u + 1]
            bs[u] = hb
        hf_scr[pl.ds(r0, 8), :] = jnp.concatenate(fs, axis=0)
        hb_scr[pl.ds(rb, 8), :] = jnp.concatenate(bs, axis=0)
        return hf, hb

    hf, hb = lax.fori_loop(0, S // 8, body, h_init)
    hfin_ref[0, 0] = hf
    hfin_ref[0, 1] = hb
    g = lg_ref[...]
    gelu = g * (0.5 * (1.0 + jnp.tanh(np.sqrt(2.0 / np.pi).astype(np.float32) * (g + 0.044715 * (g * g * g)))))
    o_ref[...] = (hf_scr[...] + hb_scr[...]) * gelu


def _lru(big, lw, h0):
    has_h0 = h0 is not None
    S, nb, base = (S_LAT, B_LAT, T_CTX // S_LAT) if has_h0 else (S_CTX, B_CTX, 0)
    blk = lambda c: pl.BlockSpec((S, LRU_W), lambda b: (base + b, c))
    in_specs = [blk(C_LX // LRU_W), blk(C_LG // LRU_W), _const_spec((4, LRU_W)), _const_spec((1, LRU_W)),
                _const_spec((LRU_W, 4 * LRU_W)), _const_spec((1, 4 * LRU_W)), _const_spec((2, 1, LRU_W))]
    args = [big, big, lw["lru_conv_w"], lw["lru_conv_b"], lw["lru_wg"], lw["lru_bg"], lw["lru_lambda"]]
    if has_h0:
        in_specs += [pl.BlockSpec((1, 2, 1, LRU_W), lambda b: (b, 0, 0, 0))]
        args += [h0]
    return pl.pallas_call(
        functools.partial(_lru_body, S, has_h0),
        grid=(nb,),
        in_specs=in_specs,
        out_specs=[pl.BlockSpec((S, LRU_W), lambda b: (b, 0)),
                   pl.BlockSpec((1, 2, 1, LRU_W), lambda b: (b, 0, 0, 0))],
        out_shape=[jax.ShapeDtypeStruct((nb * S, LRU_W), F32), jax.ShapeDtypeStruct((nb, 2, 1, LRU_W), F32)],
        scratch_shapes=[pltpu.VMEM((2, S, LRU_W), F32), pltpu.VMEM((2, S, LRU_W), F32),
                        pltpu.VMEM((S, LRU_W), F32), pltpu.VMEM((S, LRU_W), F32)],
        compiler_params=_cp(48),
        name="lru_lat" if has_h0 else "lru_ctx",
    )(*args)


def _merge_body(x_ref, *refs):
    (mod_ref, wout_ref, g2_ref, rwh_ref, rwl_ref, rb_ref, x1_ref, h2_ref, idx_ref, gw_ref) = refs[8:]
    is_ctx = pl.program_id(0) < NT_CTX
    mix = 0.0
    for j in range(4):
        o = jnp.where(is_ctx, refs[2 * j][...], refs[2 * j + 1][...])
        mix = mix + _dot(o.astype(BF16), wout_ref[j * 256:(j + 1) * 256, :])
    m = mod_ref[0]
    g1, sh2, sc2 = m[:, 2 * D:3 * D], m[:, 3 * D:4 * D], m[:, 4 * D:5 * D]
    x1 = x_ref[...] + g1 * mix
    x1_ref[...] = x1
    h2 = _rms(x1) * g2_ref[...] * (1.0 + sc2) + sh2
    for s in range(SUB):
        h2_ref[pl.ds(s, TM, stride=SUB), :] = h2[:, s * LANES:(s + 1) * LANES]
    hi = h2.astype(BF16)
    lo = (h2 - hi.astype(F32)).astype(BF16)
    logits = _dot(hi, rwh_ref[...]) + _dot(hi, rwl_ref[...]) + _dot(lo, rwh_ref[...]) + rb_ref[...]
    lane = lax.broadcasted_iota(jnp.int32, (TM, LANES), 1).astype(F32)
    idx_out = jnp.zeros((TM, LANES), F32)
    val_out = jnp.zeros((TM, LANES), F32)
    top = None
    den = 0.0
    for kk in range(TOP_K):
        mval = logits.max(axis=-1, keepdims=True)
        midx = jnp.where(logits == mval, lane, float(LANES)).min(axis=-1, keepdims=True)
        if kk == 0:
            top = mval
        e = jnp.exp(mval - top)
        den = den + e
        idx_out = jnp.where(lane == kk, midx, idx_out)
        val_out = jnp.where(lane == kk, e, val_out)
        logits = jnp.where(lane == midx, -jnp.inf, logits)
    idx_ref[...] = idx_out.astype(jnp.int32)
    gw_ref[...] = val_out / den


def _merge(x, mix_parts, mods_l, lw):
    tok = lambda w: pl.BlockSpec((TM, w), lambda i: (i, 0))
    ctx = pl.BlockSpec((TM, 256), lambda i: (jnp.minimum(i, NT_CTX - 1), 0))
    lat = pl.BlockSpec((TM, 256), lambda i: (jnp.maximum(i - NT_CTX, 0), 0))
    return pl.pallas_call(
        _merge_body,
        grid=(NT_ALL,),
        in_specs=[tok(D)] + [ctx, lat] * 4 + [
                  pl.BlockSpec((1, 1, 6 * D), lambda i: (_mod_row(i), 0, 0)),
                  _const_spec((D, D)), _const_spec((1, D)), _const_spec((D, LANES)), _const_spec((D, LANES)),
                  _const_spec((1, LANES))],
        out_specs=[tok(D), pl.BlockSpec((TM * SUB, LANES), lambda i: (i, 0)), tok(LANES), tok(LANES)],
        out_shape=[jax.ShapeDtypeStruct((T_ALL, D), F32), jax.ShapeDtypeStruct((T_ALL * SUB, LANES), F32),
                   jax.ShapeDtypeStruct((T_ALL, LANES), jnp.int32), jax.ShapeDtypeStruct((T_ALL, LANES), F32)],
        compiler_params=_cp(40),
        name="merge",
    )(x, *mix_parts, mods_l, lw["w_out"], lw["norm2_g"], lw["rw_hi"], lw["rw_lo"], lw["rb"])


def _moe_body(sn_ref, te_ref, nv_ref, tb_ref, h2_hbm, wgu_ref, bgu_ref, wd_ref, bd_ref,
              y_hbm, xbuf, obuf, wgu_bf, wd_bf, gsem, ssem):
    i = pl.program_id(0)
    slot = i % 2
    cur = jnp.minimum(i, NT_MOE - 1)
    valid = (i < NT_MOE) & (nv_ref[cur] > 0)
    prv = jnp.maximum(i - 1, 0)
    nv_prev = jnp.where(i >= 1, nv_ref[prv], 0)
    base_prev = tb_ref[prv]
    base_next = tb_ref[jnp.minimum(i + 1, NT_MOE - 1)]

    def gather_row(base, sl, r):
        tok = jnp.bitwise_and(sn_ref[base + r], T_ALL - 1)
        pltpu.make_async_copy(h2_hbm.at[pl.ds(pl.multiple_of(tok * SUB, SUB), SUB)],
                              xbuf.at[sl, pl.ds(pl.multiple_of(r * SUB, SUB), SUB)], gsem.at[sl]).start()

    def scatter_row(sl, r):
        dst = jnp.where(r < nv_prev, sn_ref[base_prev + r], N_ROWS + sl * TM_MOE + r)
        pltpu.make_async_copy(obuf.at[sl, pl.ds(pl.multiple_of(r * SUB, SUB), SUB)],
                              y_hbm.at[pl.ds(pl.multiple_of(dst * SUB, SUB), SUB)], ssem.at[sl]).start()

    def for_other_slot(fn, lo=0, hi=TM_MOE):
        for other in range(2):
            @pl.when(slot == 1 - other)
            def _():
                lax.fori_loop(lo, hi, lambda r, c: fn(other, r) or c, 0, unroll=8)

    def gather_done(sl):
        pltpu.make_async_copy(h2_hbm.at[pl.ds(0, TM_MOE * SUB)], xbuf.at[sl], gsem.at[sl]).wait()

    def scatter_done(sl):
        pltpu.make_async_copy(obuf.at[sl], y_hbm.at[pl.ds(0, TM_MOE * SUB)], ssem.at[sl]).wait()

    @pl.when(i == 0)
    def _():
        obuf[1] = jnp.zeros((TM_MOE * SUB, LANES), F32)
        cp = pltpu.make_async_copy(obuf.at[1], y_hbm.at[pl.ds(N_ROWS * SUB, TM_MOE * SUB)], ssem.at[1])
        cp.start()
        cp.wait()

        def body(r, c):
            gather_row(tb_ref[0], 0, r)
            return c
        lax.fori_loop(0, TM_MOE, body, 0, unroll=8)

    e = te_ref[cur]
    prev_e = te_ref[prv]

    @pl.when(valid & ((i == 0) | (e != prev_e)))
    def _():
        wgu_bf[...] = wgu_ref[0, 0].astype(BF16)
        wd_bf[...] = wd_ref[0, 0].astype(BF16)

    def compute(this):
        other = 1 - this
        gather_done(this)
        x = jnp.concatenate([xbuf[this, pl.ds(s, TM_MOE, stride=SUB), :] for s in range(SUB)], axis=-1).astype(BF16)
        acc = jnp.zeros((TM_MOE, D), F32)
        n_chunk = D_EXP // EXP_CHUNK
        part = TM_MOE // n_chunk
        for j in range(n_chunk):
            c0 = j * EXP_CHUNK
            for r in range(j * part, (j + 1) * part):
                gather_row(base_next, other, r)
                scatter_row(other, r)
            xg = _dot(x, wgu_bf[:, c0:c0 + EXP_CHUNK]) + bgu_ref[0, 0][:, c0:c0 + EXP_CHUNK]
            xl = _dot(x, wgu_bf[:, D_EXP + c0:D_EXP + c0 + EXP_CHUNK]) + bgu_ref[0, 0][:, D_EXP + c0:D_EXP + c0 + EXP_CHUNK]
            xg = jnp.minimum(xg, SWIGLU_LIMIT)
            xl = jnp.clip(xl, -SWIGLU_LIMIT, SWIGLU_LIMIT)
            act = xg * jax.nn.sigmoid(SWIGLU_ALPHA * xg) * (xl + 1.0)
            acc = acc + _dot(act.astype(BF16), wd_bf[c0:c0 + EXP_CHUNK, :])
        out = acc + bd_ref[0, 0]
        for s in range(SUB):
            obuf[this, pl.ds(s, TM_MOE, stride=SUB), :] = out[:, s * LANES:(s + 1) * LANES]
        scatter_done(other)

    for parity in range(2):
        pl.when(valid & (slot == parity))(functools.partial(compute, parity))

    @pl.when(jnp.logical_not(valid) & (nv_prev > 0))
    def _():
        gather_done(slot)
        for_other_slot(scatter_row)
        scatter_done(1 - slot)


def _moe(sorted_n, tile_exp, tile_nvalid, tile_base, h2, l, p):
    wmap = lambda i, sn, te, nv, tb: (l, te[jnp.minimum(i, NT_MOE - 1)], 0, 0)
    grid_spec = pltpu.PrefetchScalarGridSpec(
        num_scalar_prefetch=4,
        grid=(NT_MOE + 1,),
        in_specs=[
            pl.BlockSpec(memory_space=pl.ANY),
            pl.BlockSpec((1, 1, D, 2 * D_EXP), wmap),
            pl.BlockSpec((1, 1, 1, 2 * D_EXP), wmap),
            pl.BlockSpec((1, 1, D_EXP, D), wmap),
            pl.BlockSpec((1, 1, 1, D), wmap),
        ],
        out_specs=pl.BlockSpec(memory_space=pl.ANY),
        scratch_shapes=[pltpu.VMEM((2, TM_MOE * SUB, LANES), F32), pltpu.VMEM((2, TM_MOE * SUB, LANES), F32),
                        pltpu.VMEM((D, 2 * D_EXP), BF16), pltpu.VMEM((D_EXP, D), BF16),
                        pltpu.SemaphoreType.DMA((2,)), pltpu.SemaphoreType.DMA((2,))],
    )
    return pl.pallas_call(
        _moe_body,
        grid_spec=grid_spec,
        out_shape=jax.ShapeDtypeStruct(((N_ROWS + N_DUMP) * SUB, LANES), F32),
        compiler_params=pltpu.CompilerParams(vmem_limit_bytes=56 * 1024 * 1024, dimension_semantics=("arbitrary",),
                                             disable_bounds_checks=True),
        name="moe",
    )(sorted_n, tile_exp, tile_nvalid, tile_base, h2,
      p["exp_w_gu"], p["exp_b_gu"].reshape(DEPTH, N_EXP, 1, 2 * D_EXP),
      p["exp_w_down"], p["exp_b_down"].reshape(DEPTH, N_EXP, 1, D))


def _route(idx):
    flat = idx.T.reshape(-1)
    ids = jnp.arange(N_ROWS, dtype=jnp.int32)
    sorted_n = jnp.concatenate([jnp.sort(flat * N_ROWS + ids) % N_ROWS, jnp.zeros((TM_MOE,), jnp.int32)])
    experts = jnp.arange(N_EXP, dtype=jnp.int32)
    counts = jnp.sum((flat[None, :] == experts[:, None]).astype(jnp.int32), axis=1)
    padded = ((counts + TM_MOE - 1) // TM_MOE) * TM_MOE
    pad_ends = jnp.cumsum(padded)
    pad_starts = pad_ends - padded
    unp_starts = jnp.cumsum(counts) - counts
    tile_start = jnp.arange(NT_MOE, dtype=jnp.int32) * TM_MOE
    tile_exp = jnp.minimum(jnp.sum((pad_ends[None, :] <= tile_start[:, None]).astype(jnp.int32), axis=1), N_EXP - 1)
    onehot = (tile_exp[:, None] == experts[None, :]).astype(jnp.int32)
    pick = lambda v: jnp.sum(onehot * v[None, :], axis=1)
    offset = tile_start - pick(pad_starts)
    tile_nvalid = jnp.where(tile_start < pad_ends[-1], jnp.clip(pick(counts) - offset, 0, TM_MOE), 0)
    tile_base = jnp.clip(pick(unp_starts) + offset, 0, N_ROWS - 1)
    return sorted_n, tile_exp, tile_nvalid.astype(jnp.int32), tile_base.astype(jnp.int32)


def _final_body(x_ref, y0_ref, y1_ref, y2_ref, y3_ref, gw_ref, mod_ref, g_ref, o_ref):
    g2 = mod_ref[0][:, 5 * D:6 * D]
    o_ref[...] = _rms(x_ref[...] + g2 * _sum_slots((y0_ref, y1_ref, y2_ref, y3_ref), gw_ref)) * g_ref[...]


def _final(x1, moe, gw, mods_l, g, tile0, ntiles):
    tok = pl.BlockSpec((TM, D), lambda i: (tile0 + i, 0))
    return pl.pallas_call(
        _final_body,
        grid=(ntiles,),
        in_specs=[tok] + _slot_specs(tile0) + [
                  pl.BlockSpec((TM, LANES), lambda i: (tile0 + i, 0)),
                  pl.BlockSpec((1, 1, 6 * D), lambda i: (_mod_row(tile0 + i), 0, 0)), _const_spec((1, D))],
        out_specs=pl.BlockSpec((TM, D), lambda i: (i, 0)),
        out_shape=jax.ShapeDtypeStruct((ntiles * TM, D), F32),
        name="final_norm",
    )(x1, *([moe] * TOP_K), gw, mods_l, g)


def _rope_perm(w):
    parts = []
    for a in range(2):
        x1, x2 = w[..., a * 16:a * 16 + 8], w[..., a * 16 + 8:a * 16 + 16]
        parts += [-x2, x1]
    return jnp.concatenate(parts, axis=-1)


def _rope_tables():
    t = np.arange(S_LAT)
    freqs = ROPE_BASE ** (-np.arange(8, dtype=np.float64) / 8)
    cs, sn = [], []
    for pos in (t // GRID_W, t % GRID_W):
        ang = pos.astype(np.float64)[:, None] * freqs[None, :]
        cs += [np.cos(ang), np.cos(ang)]
        sn += [np.sin(ang), np.sin(ang)]
    cos = np.concatenate([np.ones((ID_ROWS, 32), np.float32), np.concatenate(cs, -1).astype(np.float32)], 0)
    sin = np.concatenate([np.zeros((ID_ROWS, 32), np.float32), np.concatenate(sn, -1).astype(np.float32)], 0)
    n = cos.shape[0]
    one, zero = np.ones((n, 1), np.float32), np.zeros((n, 1), np.float32)
    cq_h = np.concatenate([np.repeat(one, 64, 1), cos, np.repeat(zero, 32, 1)], 1)
    sq_h = np.concatenate([np.repeat(zero, 64, 1), sin, np.repeat(zero, 32, 1)], 1)
    cseg = np.concatenate([cos, np.repeat(one, 96, 1)], 1)
    sseg = np.concatenate([sin, np.repeat(zero, 96, 1)], 1)
    return dict(cq=jnp.asarray(np.tile(cq_h, (1, 4))), sq=jnp.asarray(np.tile(sq_h, (1, 4))),
                cs=jnp.asarray(cseg), ss=jnp.asarray(sseg))


def _na_bias_table(rpb):
    qc = np.arange(GRID_W)[:, None]
    kc = np.arange(GRID_W)[None, :]
    cs = np.clip(qc - NA_WIN_W // 2, 0, GRID_W - NA_WIN_W)
    valid = (kc >= cs) & (kc < cs + NA_WIN_W)
    pad = GRID_W - NA_WIN_W
    ext = jnp.pad(rpb, ((0, 0), (0, 0), (pad, pad)))
    toe = jnp.stack([ext[:, :, GRID_W - 1 - q:2 * GRID_W - 1 - q] for q in range(GRID_W)], axis=2)
    toe = jnp.where(valid[None, None], toe, NEG)
    tab = jnp.stack([toe[:, o:o + NA_WIN_H] for o in range(NA_WIN_H)], axis=0)
    return tab.transpose(0, 1, 3, 2, 4).reshape(NA_WIN_H, NA_HEADS, GRID_W, NK_LOC).astype(F32)


def _layer_weights(l, p):
    w_in = p["w_in"][l]
    z = lambda n: jnp.zeros((D, n), F32)
    kr = w_in[:, 320:352]
    w_perm = jnp.concatenate([
        w_in[:, 352:1120], w_in[:, 1120:1248], w_in[:, 1248:1376], w_in[:, 1376:1632], w_in[:, 1632:1888],
        w_in[:, 1920:2176], w_in[:, 2176:2432], w_in[:, 192:320], w_in[:, 0:192], z(64),
        kr, w_in[:, 1888:1920], z(64), _rope_perm(kr), z(96)], axis=1).astype(BF16)
    wuq = p["mla_w_uq"][l].reshape(MLA_Q_LORA, MLA_HEADS, MLA_NOPE + MLA_ROPE)
    zq = jnp.zeros((MLA_Q_LORA, MLA_HEADS, 32), F32)
    wq1 = jnp.concatenate([wuq, zq], axis=-1).reshape(MLA_Q_LORA, 512)
    wq2 = jnp.concatenate([jnp.zeros((MLA_Q_LORA, MLA_HEADS, MLA_NOPE), F32), _rope_perm(wuq[..., MLA_NOPE:]), zq],
                          axis=-1).reshape(MLA_Q_LORA, 512)
    wq = jnp.pad(jnp.concatenate([wq1, wq2], axis=1), ((0, 256 - MLA_Q_LORA), (0, 0))).astype(BF16)
    qg = jnp.pad(p["mla_q_norm_g"][l], (0, 256 - MLA_Q_LORA)).reshape(1, 256)
    wukv = p["mla_w_ukv"][l].reshape(MLA_KV_LORA, MLA_HEADS, MLA_NOPE + MLA_V)
    wk = jnp.concatenate([wukv[..., :MLA_NOPE], jnp.zeros((MLA_KV_LORA, MLA_HEADS, 64), F32)], -1).reshape(MLA_KV_LORA, 512)
    wv = wukv[..., MLA_NOPE:].reshape(MLA_KV_LORA, 256)
    eye = jnp.eye(MLA_ROPE, dtype=F32)[:, None, :]
    ek = jnp.concatenate([jnp.zeros((MLA_ROPE, MLA_HEADS, MLA_NOPE), F32),
                          jnp.broadcast_to(eye, (MLA_ROPE, MLA_HEADS, MLA_ROPE)),
                          jnp.zeros((MLA_ROPE, MLA_HEADS, 32), F32)], -1).reshape(MLA_ROPE, 512)
    wkv = jnp.concatenate([
        jnp.concatenate([wk, wv], axis=1),
        jnp.concatenate([ek, jnp.zeros((MLA_ROPE, 256), F32)], axis=1),
        jnp.zeros((LANES - MLA_ROPE, 768), F32)], axis=0).astype(BF16)
    wg = jnp.stack([jnp.pad(p["gla_w_gate"][l, d], ((SEG_GA + d * GLA_RANK, LANES - SEG_GA - (d + 1) * GLA_RANK), (0, 0)))
                    for d in range(2)])
    def bdiag(w):
        eye = jnp.eye(LRU_BLOCKS, dtype=F32)
        return (w[:, :, None, :] * eye[:, None, :, None]).reshape(LRU_W, LRU_W)
    lru_wg = jnp.concatenate([bdiag(p["lru_w_a"][l, 0]), bdiag(p["lru_w_a"][l, 1]),
                              bdiag(p["lru_w_x"][l, 0]), bdiag(p["lru_w_x"][l, 1])], axis=1).astype(BF16)
    lru_bg = jnp.concatenate([p["lru_b_a"][l, 0], p["lru_b_a"][l, 1], p["lru_b_x"][l, 0], p["lru_b_x"][l, 1]]).reshape(1, 4 * LRU_W)
    rw = jnp.pad(p["router_w"][l], ((0, 0), (0, LANES - N_EXP)))
    rw_hi = rw.astype(BF16)
    rw_lo = (rw - rw_hi.astype(F32)).astype(BF16)
    rb = jnp.concatenate([p["router_b"][l], jnp.full((LANES - N_EXP,), -jnp.inf, F32)]).reshape(1, LANES)
    return dict(
        w_in=w_perm, norm1_g=p["norm1_g"][l].reshape(1, D), norm2_g=p["norm2_g"][l].reshape(1, D),
        qg=qg, wq=wq, kvg=p["mla_kv_norm_g"][l].reshape(1, LANES), wkv=wkv,
        wg=wg.astype(BF16), bg=p["gla_b_gate"][l].reshape(2, 1, LANES), gla_norm_g=p["gla_norm_g"][l].reshape(1, 256),
        lru_conv_w=p["lru_conv_w"][l], lru_conv_b=p["lru_conv_b"][l].reshape(1, LRU_W), lru_wg=lru_wg, lru_bg=lru_bg,
        lru_lambda=p["lru_lambda"][l].reshape(2, 1, LRU_W),
        w_out=p["w_out"][l].astype(BF16), rw_hi=rw_hi, rw_lo=rw_lo, rb=rb,
        na_bias=_na_bias_table(p["na_rpb"][l]),
    )


def _state_to_blockdiag(s):
    eye = jnp.eye(GLA_HEADS, dtype=s.dtype)
    return jnp.einsum("bdhkv,hg->bdhvgk", s, eye).reshape(s.shape[0], 2, 256, LANES)


def _blockdiag_to_state(st):
    b = st.shape[0]
    s6 = st.reshape(b, 2, GLA_HEADS, GLA_DV, GLA_HEADS, GLA_DK)
    diag = jnp.stack([s6[:, :, h, :, h, :] for h in range(GLA_HEADS)], axis=2)
    return diag.transpose(0, 1, 2, 4, 3)


def kernel(x_prompt, x_sample, cache_mla_ckv, cache_mla_krope, cache_na_k, cache_na_v, state_gla, state_lru, c, c_ctx, norm1_g, norm2_g, w_mod, b_mod, w_in, w_out, mla_q_norm_g, mla_w_uq, mla_kv_norm_g, mla_w_ukv, na_rpb, gla_w_gate, gla_b_gate, gla_norm_g, lru_conv_w, lru_conv_b, lru_w_a, lru_b_a, lru_w_x, lru_b_x, lru_lambda, router_w, router_b, exp_w_gu, exp_b_gu, exp_w_down, exp_b_down, final_norm_g):
    p = dict(norm1_g=norm1_g, norm2_g=norm2_g, w_in=w_in, w_out=w_out, mla_q_norm_g=mla_q_norm_g, mla_w_uq=mla_w_uq,
             mla_kv_norm_g=mla_kv_norm_g, mla_w_ukv=mla_w_ukv, na_rpb=na_rpb, gla_w_gate=gla_w_gate,
             gla_b_gate=gla_b_gate, gla_norm_g=gla_norm_g, lru_conv_w=lru_conv_w, lru_conv_b=lru_conv_b,
             lru_w_a=lru_w_a, lru_b_a=lru_b_a, lru_w_x=lru_w_x, lru_b_x=lru_b_x, lru_lambda=lru_lambda,
             router_w=router_w, router_b=router_b, exp_w_gu=exp_w_gu, exp_b_gu=exp_b_gu, exp_w_down=exp_w_down,
             exp_b_down=exp_b_down)
    x = jnp.concatenate([x_prompt.reshape(T_CTX, D), x_sample.reshape(T_LAT, D)], axis=0)
    crow = jnp.concatenate([c_ctx[None, :], c, jnp.zeros((8 - 1 - B_LAT, D), F32)], axis=0)
    mods = _modulation(crow, w_mod, b_mod).reshape(DEPTH, 8, 1, 6 * D)
    tabs = _rope_tables()

    col = [[] for _ in range(6)]
    moe = None
    x1 = None
    for l in range(DEPTH):
        lw = _layer_weights(l, p)
        if l == 0:
            x, big, seg, ckv, q, k, v = _proj(True, x, None, None, None, mods[l], lw, tabs)
        else:
            x, big, seg, ckv, q, k, v = _proj(False, x1, moe, gw, mods[l - 1], mods[l], lw, tabs)
        kr_c = jnp.pad(cache_mla_krope[:, l].reshape(B_LAT * PAST, MLA_ROPE), ((0, 0), (0, LANES - MLA_ROPE)))
        kc, vc = _cache_kv(cache_mla_ckv[:, l].reshape(B_LAT * PAST, MLA_KV_LORA), kr_c, lw["wkv"])
        oa_c, ob_c = _ctx_attn(q, k, v, big)
        oa_l = _lat_mla(q, k, v, kc, vc)
        ob_l = _lat_na(big, cache_na_k[:, l].reshape(B_LAT * PAST, 256), cache_na_v[:, l].reshape(B_LAT * PAST, 256),
                       lw["na_bias"])
        oc_c, sg = _gla(big, seg, lw, None)
        oc_l, _ = _gla(big, seg, lw, _state_to_blockdiag(state_gla[:, l]))
        od_c, sl = _lru(big, lw, None)
        od_l, _ = _lru(big, lw, state_lru[:, l].reshape(B_LAT, 2, 1, LRU_W))
        x1, h2, idx, gw = _merge(x, (oa_c, oa_l, ob_c, ob_l, oc_c, oc_l, od_c, od_l), mods[l], lw)
        moe = _moe(*_route(idx[:, :TOP_K]), h2, l, p)
        col[0].append(ckv[:T_CTX].reshape(B_CTX, S_CTX, MLA_KV_LORA))
        col[1].append(seg[:T_CTX, :MLA_ROPE].reshape(B_CTX, S_CTX, MLA_ROPE))
        col[2].append(big[:T_CTX, C_NAK:C_NAK + 256].reshape(B_CTX, S_CTX, NA_HEADS, NA_DH))
        col[3].append(big[:T_CTX, C_NAV:C_NAV + 256].reshape(B_CTX, S_CTX, NA_HEADS, NA_DH))
        col[4].append(_blockdiag_to_state(sg))
        col[5].append(sl.reshape(B_CTX, 2, LRU_W))
    fg = final_norm_g.reshape(1, D)
    y_prompt = _final(x1, moe, gw, mods[DEPTH - 1], fg, 0, NT_CTX).reshape(B_CTX, S_CTX, D)
    y_sample = _final(x1, moe, gw, mods[DEPTH - 1], fg, NT_CTX, NT_ALL - NT_CTX).reshape(B_LAT, S_LAT, D)
    return (y_prompt, y_sample) + tuple(jnp.stack(cl, axis=1) for cl in col)
```

```python
import functools

import numpy as np
import jax
import jax.numpy as jnp
from jax import lax
from jax.experimental import pallas as pl
from jax.experimental.pallas import tpu as pltpu

F32 = jnp.float32
BF16 = jnp.bfloat16
HIGHEST = lax.Precision.HIGHEST

D = 1024
DEPTH = 2
B_CTX, S_CTX = 32, 256
B_LAT, S_LAT = 4, 2048
PAST = 512
T_CTX = B_CTX * S_CTX
T_LAT = B_LAT * S_LAT
T_ALL = T_CTX + T_LAT
GRID_W = 64
ROWS = S_LAT // GRID_W

MLA_HEADS, MLA_NOPE, MLA_ROPE, MLA_V = 4, 64, 32, 64
MLA_Q_LORA, MLA_KV_LORA = 192, 128
MLA_SCALE = (MLA_NOPE + MLA_ROPE) ** -0.5
NA_HEADS, NA_DH, NA_WIN_H, NA_WIN_W = 4, 64, 8, 16
NA_SCALE = NA_DH ** -0.5
GLA_HEADS, GLA_DK, GLA_DV, GLA_RANK, GLA_NORM, GLA_CHUNK = 4, 32, 64, 16, 16.0, 64
GLA_SUB = 16
LRU_W, LRU_BLOCKS, LRU_C = 256, 4, 8.0
N_EXP, TOP_K, D_EXP = 32, 4, 1024
SWIGLU_ALPHA, SWIGLU_LIMIT = 1.702, 7.0
ROPE_BASE = 10000.0
EPS = 1e-6
NEG = -1e30

LANES = 128
VMEM_BYTES = 64 * 1024 * 1024

TM = 256
NT_CTX = T_CTX // TM
NT_ALL = T_ALL // TM
TILES_PER_LAT = S_LAT // TM
ID_ROWS = 2 * TM

TM_MOE = 512
N_ROWS = T_ALL * TOP_K
R_MAX = N_ROWS + N_EXP * TM_MOE
NT_MOE = R_MAX // TM_MOE
EXP_CHUNK = 256
SUB = D // LANES
assert T_ALL & (T_ALL - 1) == 0

C_NAQ, C_NAK, C_NAV, C_GQ, C_GK, C_GV, C_GG, C_LX, C_LG = 0, 256, 512, 768, 896, 1024, 1280, 1536, 1792
BIG = 2048
C_PKV, C_PQ, C_SEGA, C_SEGB = 2048, 2176, 2432, 2560
NC_IN = 2688
SEG_GA = 32


def _cp(vmem_mb, sem=None):
    kw = dict(vmem_limit_bytes=vmem_mb * 1024 * 1024)
    if sem is not None:
        kw["dimension_semantics"] = sem
    return pltpu.CompilerParams(**kw)


def _rms(x):
    return x * lax.rsqrt(jnp.mean(x * x, axis=-1, keepdims=True) + EPS)


def _dot(a, b, **kw):
    return jnp.dot(a, b, preferred_element_type=F32, **kw)


def _dot_nt(a, b):
    return lax.dot_general(a, b, (((1,), (1,)), ((), ())), preferred_element_type=F32)


def _dot_tn(a, b):
    return lax.dot_general(a, b, (((0,), (0,)), ((), ())), preferred_element_type=F32)


def _const_spec(shape):
    n = len(shape)
    return pl.BlockSpec(shape, lambda *_: (0,) * n)


def _mod_body(c_ref, w_ref, b_ref, o_ref):
    c = c_ref[...]
    s = c * jax.nn.sigmoid(c)
    o_ref[0] = _dot(s.astype(BF16), w_ref[0].astype(BF16)) + b_ref[0]


def _modulation(crow, w_mod, b_mod):
    tn = 1536
    return pl.pallas_call(
        _mod_body,
        grid=(DEPTH, 6 * D // tn),
        in_specs=[
            pl.BlockSpec((8, D), lambda l, j: (0, 0)),
            pl.BlockSpec((1, D, tn), lambda l, j: (l, 0, j)),
            pl.BlockSpec((1, 1, tn), lambda l, j: (l, 0, j)),
        ],
        out_specs=pl.BlockSpec((1, 8, tn), lambda l, j: (l, 0, j)),
        out_shape=jax.ShapeDtypeStruct((DEPTH, 8, 6 * D), F32),
        compiler_params=_cp(40),
        name="modulation",
    )(crow, w_mod, b_mod.reshape(DEPTH, 1, 6 * D))


def _mod_row(i):
    return jnp.where(i < NT_CTX, 0, 1 + (i - NT_CTX) // TILES_PER_LAT)


def _pos_block(i):
    return jnp.where(i < NT_CTX, 0, ID_ROWS // TM + (i - NT_CTX) % TILES_PER_LAT)


def _sum_slots(y_refs, gw_ref):
    gw = gw_ref[...]
    cols = [gw[:, kk:kk + 1] for kk in range(TOP_K)]
    parts = []
    for s in range(SUB):
        acc = cols[0] * y_refs[0][pl.ds(s, TM, stride=SUB), :]
        for kk in range(1, TOP_K):
            acc = acc + cols[kk] * y_refs[kk][pl.ds(s, TM, stride=SUB), :]
        parts.append(acc)
    return jnp.concatenate(parts, axis=-1)


def _slot_specs(tile0):
    return [pl.BlockSpec((TM * SUB, LANES), functools.partial(lambda kk, i: (kk * NT_ALL + tile0 + i, 0), kk))
            for kk in range(TOP_K)]


def _proj_body(first, *refs):
    if first:
        (x_ref, mod_ref, g1_ref, win_ref, qg_ref, wq_ref, kvg_ref, wkv_ref, cq_ref, sq_ref, cs_ref, ss_ref,
         big_ref, seg_ref, ckv_ref, q_ref, k_ref, v_ref) = refs
        x = x_ref[...]
    else:
        (x_ref, y0_ref, y1_ref, y2_ref, y3_ref, gw_ref, modp_ref, mod_ref, g1_ref, win_ref, qg_ref, wq_ref, kvg_ref, wkv_ref, cq_ref, sq_ref,
         cs_ref, ss_ref, xo_ref, big_ref, seg_ref, ckv_ref, q_ref, k_ref, v_ref) = refs
        g2 = modp_ref[0][:, 5 * D:6 * D]
        x = x_ref[...] + g2 * _sum_slots((y0_ref, y1_ref, y2_ref, y3_ref), gw_ref)
        xo_ref[...] = x
    m = mod_ref[0]
    sh1, sc1 = m[:, 0:D], m[:, D:2 * D]
    h = _rms(x) * g1_ref[...] * (1.0 + sc1) + sh1
    p = _dot(h.astype(BF16), win_ref[...])
    big_ref[...] = p[:, :BIG]
    ckv = _rms(p[:, C_PKV:C_PKV + MLA_KV_LORA]) * kvg_ref[...]
    ckv_ref[...] = ckv
    pq = p[:, C_PQ:C_PQ + 256]
    qn = pq * lax.rsqrt(jnp.sum(pq * pq, axis=-1, keepdims=True) * (1.0 / MLA_Q_LORA) + EPS) * qg_ref[...]
    a = _dot(qn.astype(BF16), wq_ref[...])
    q_ref[...] = (a[:, :512] * cq_ref[...] + a[:, 512:] * sq_ref[...]).astype(BF16)
    seg = p[:, C_SEGA:C_SEGA + LANES] * cs_ref[...] + p[:, C_SEGB:C_SEGB + LANES] * ss_ref[...]
    seg_ref[...] = seg
    kv = _dot(jnp.concatenate([ckv, seg], axis=-1).astype(BF16), wkv_ref[...])
    k_ref[...] = kv[:, :512].astype(BF16)
    v_ref[...] = kv[:, 512:].astype(BF16)


def _proj(first, x, moe, gw, mods_prev, mods_l, lw, tabs):
    tok = lambda w: pl.BlockSpec((TM, w), lambda i: (i, 0))
    modspec = pl.BlockSpec((1, 1, 6 * D), lambda i: (_mod_row(i), 0, 0))
    pos = lambda w: pl.BlockSpec((TM, w), lambda i: (_pos_block(i), 0))
    in_specs = [tok(D)]
    args = [x]
    if not first:
        in_specs += _slot_specs(0) + [tok(LANES), modspec]
        args += [moe] * TOP_K + [gw, mods_prev]
    in_specs += [modspec, _const_spec((1, D)), _const_spec((D, NC_IN)), _const_spec((1, 256)),
                 _const_spec((256, 1024)), _const_spec((1, LANES)), _const_spec((256, 768)),
                 pos(512), pos(512), pos(LANES), pos(LANES)]
    args += [mods_l, lw["norm1_g"], lw["w_in"], lw["qg"], lw["wq"], lw["kvg"], lw["wkv"],
             tabs["cq"], tabs["sq"], tabs["cs"], tabs["ss"]]
    out_specs = [tok(BIG), tok(LANES), tok(LANES), tok(512), tok(512), tok(256)]
    out_shape = [jax.ShapeDtypeStruct((T_ALL, BIG), F32), jax.ShapeDtypeStruct((T_ALL, LANES), F32),
                 jax.ShapeDtypeStruct((T_ALL, LANES), F32), jax.ShapeDtypeStruct((T_ALL, 512), BF16),
                 jax.ShapeDtypeStruct((T_ALL, 512), BF16), jax.ShapeDtypeStruct((T_ALL, 256), BF16)]
    if not first:
        out_specs = [tok(D)] + out_specs
        out_shape = [jax.ShapeDtypeStruct((T_ALL, D), F32)] + out_shape
    outs = pl.pallas_call(
        functools.partial(_proj_body, first),
        grid=(NT_ALL,),
        in_specs=in_specs,
        out_specs=out_specs,
        out_shape=out_shape,
        compiler_params=_cp(48),
        name="proj",
    )(*args)
    if first:
        return (x,) + tuple(outs)
    return tuple(outs)


def _kvc_body(ckv_ref, kr_ref, wkv_ref, k_ref, v_ref):
    kv = _dot(jnp.concatenate([ckv_ref[...], kr_ref[...]], axis=-1).astype(BF16), wkv_ref[...])
    k_ref[...] = kv[:, :512].astype(BF16)
    v_ref[...] = kv[:, 512:].astype(BF16)


def _cache_kv(ckv_c, kr_c, wkv):
    n = B_LAT * PAST
    return pl.pallas_call(
        _kvc_body,
        grid=(B_LAT,),
        in_specs=[pl.BlockSpec((PAST, LANES), lambda b: (b, 0)), pl.BlockSpec((PAST, LANES), lambda b: (b, 0)),
                  _const_spec((256, 768))],
        out_specs=[pl.BlockSpec((PAST, 512), lambda b: (b, 0)), pl.BlockSpec((PAST, 256), lambda b: (b, 0))],
        out_shape=[jax.ShapeDtypeStruct((n, 512), BF16), jax.ShapeDtypeStruct((n, 256), BF16)],
        name="cache_kv",
    )(ckv_c, kr_c, wkv)


def _softmax_pv(scores, values):
    m = scores[0].max(axis=-1, keepdims=True)
    for s in scores[1:]:
        m = jnp.maximum(m, s.max(axis=-1, keepdims=True))
    den = 0.0
    acc = 0.0
    for s, v in zip(scores, values):
        e = jnp.exp(s - m)
        den = den + e.sum(axis=-1, keepdims=True)
        acc = acc + _dot(e.astype(BF16), v)
    return acc / den


def _lane_lo(shape):
    return lax.broadcasted_iota(jnp.int32, shape, len(shape) - 1) < 64


def _ctx_attn_body(q_ref, k_ref, v_ref, na_ref, oa_ref, ob_ref):
    lo = _lane_lo((S_CTX, LANES))
    for pp in range(2):
        vp = v_ref[:, pp * LANES:(pp + 1) * LANES]
        outs = []
        for h in (2 * pp, 2 * pp + 1):
            s = _dot_nt(q_ref[:, h * LANES:(h + 1) * LANES], k_ref[:, h * LANES:(h + 1) * LANES]) * MLA_SCALE
            outs.append(_softmax_pv([s], [vp]))
        oa_ref[:, pp * LANES:(pp + 1) * LANES] = jnp.where(lo, outs[0], outs[1])
    for pp in range(2):
        qp = na_ref[:, C_NAQ + pp * LANES:C_NAQ + (pp + 1) * LANES]
        kp = na_ref[:, C_NAK + pp * LANES:C_NAK + (pp + 1) * LANES].astype(BF16)
        vp = na_ref[:, C_NAV + pp * LANES:C_NAV + (pp + 1) * LANES].astype(BF16)
        outs = []
        for sel in (lo, ~lo):
            qh = jnp.where(sel, qp, 0.0).astype(BF16)
            outs.append(_softmax_pv([_dot_nt(qh, kp) * NA_SCALE], [vp]))
        ob_ref[:, pp * LANES:(pp + 1) * LANES] = jnp.where(lo, outs[0], outs[1])


def _ctx_attn(q, k, v, big):
    blk = lambda w: pl.BlockSpec((S_CTX, w), lambda b: (b, 0))
    return pl.pallas_call(
        _ctx_attn_body,
        grid=(B_CTX,),
        in_specs=[blk(512), blk(512), blk(256), blk(768)],
        out_specs=[blk(256), blk(256)],
        out_shape=[jax.ShapeDtypeStruct((T_CTX, 256), F32), jax.ShapeDtypeStruct((T_CTX, 256), F32)],
        name="ctx_attn",
    )(q, k, v, big)


TQ = 256


def _lat_mla_body(q_ref, kl_ref, vl_ref, kc_ref, vc_ref, o_ref):
    lo = _lane_lo((TQ, LANES))
    for pp in range(2):
        vlp = vl_ref[:, pp * LANES:(pp + 1) * LANES]
        vcp = vc_ref[:, pp * LANES:(pp + 1) * LANES]
        outs = []
        for h in (2 * pp, 2 * pp + 1):
            qh = q_ref[:, h * LANES:(h + 1) * LANES]
            s_l = _dot_nt(qh, kl_ref[:, h * LANES:(h + 1) * LANES]) * MLA_SCALE
            s_c = _dot_nt(qh, kc_ref[:, h * LANES:(h + 1) * LANES]) * MLA_SCALE
            outs.append(_softmax_pv([s_l, s_c], [vlp, vcp]))
        o_ref[:, pp * LANES:(pp + 1) * LANES] = jnp.where(lo, outs[0], outs[1])


def _lat_mla(q, k, v, kc, vc):
    nq = S_LAT // TQ
    qrow = lambda b, t: (T_CTX // TQ + b * nq + t, 0)
    lat = lambda b, t: (T_CTX // S_LAT + b, 0)
    return pl.pallas_call(
        _lat_mla_body,
        grid=(B_LAT, nq),
        in_specs=[pl.BlockSpec((TQ, 512), qrow), pl.BlockSpec((S_LAT, 512), lat), pl.BlockSpec((S_LAT, 256), lat),
                  pl.BlockSpec((PAST, 512), lambda b, t: (b, 0)), pl.BlockSpec((PAST, 256), lambda b, t: (b, 0))],
        out_specs=pl.BlockSpec((TQ, 256), lambda b, t: (b * nq + t, 0)),
        out_shape=jax.ShapeDtypeStruct((T_LAT, 256), F32),
        compiler_params=_cp(48),
        name="lat_mla",
    )(q, k, v, kc, vc)


NK_LOC = NA_WIN_H * GRID_W


def _na_row_start(r):
    return jnp.clip(r - NA_WIN_H // 2, 0, ROWS - NA_WIN_H)


def _lat_na_body(q_ref, k_ref, v_ref, kc_ref, vc_ref, bias_ref, o_ref):
    r = pl.program_id(1)
    k0 = pl.multiple_of(_na_row_start(r) * GRID_W, GRID_W)
    lo = _lane_lo((GRID_W, LANES))
    for pp in range(2):
        cols = slice(pp * LANES, (pp + 1) * LANES)
        qp = q_ref[:, cols]
        kl = k_ref[pl.ds(k0, NK_LOC), cols].astype(BF16)
        vl = v_ref[pl.ds(k0, NK_LOC), cols].astype(BF16)
        kc = kc_ref[:, cols].astype(BF16)
        vc = vc_ref[:, cols].astype(BF16)
        outs = []
        for j, sel in enumerate((lo, ~lo)):
            qh = jnp.where(sel, qp, 0.0).astype(BF16)
            s_loc = _dot_nt(qh, kl) * NA_SCALE + bias_ref[0, 2 * pp + j]
            s_ctx = _dot_nt(qh, kc) * NA_SCALE
            outs.append(_softmax_pv([s_loc, s_ctx], [vl, vc]))
        o_ref[:, cols] = jnp.where(lo, outs[0], outs[1])


def _lat_na(big, kc, vc, bias):
    qrow = lambda b, r: (T_CTX // GRID_W + b * ROWS + r, 0)
    boff = lambda b, r: (_na_row_start(r) - r + NA_WIN_H - 1, 0, 0, 0)
    lat = lambda c: (lambda b, r: (T_CTX // S_LAT + b, c))
    return pl.pallas_call(
        _lat_na_body,
        grid=(B_LAT, ROWS),
        in_specs=[pl.BlockSpec((GRID_W, 256), qrow), pl.BlockSpec((S_LAT, 256), lat(1)),
                  pl.BlockSpec((S_LAT, 256), lat(2)),
                  pl.BlockSpec((PAST, 256), lambda b, r: (b, 0)), pl.BlockSpec((PAST, 256), lambda b, r: (b, 0)),
                  pl.BlockSpec((1, NA_HEADS, GRID_W, NK_LOC), boff)],
        out_specs=pl.BlockSpec((GRID_W, 256), lambda b, r: (b * ROWS + r, 0)),
        out_shape=jax.ShapeDtypeStruct((T_LAT, 256), F32),
        compiler_params=_cp(48),
        name="lat_na",
    )(big, big, big, kc, vc, bias)


def _log_sigmoid(x):
    return jnp.minimum(x, 0.0) - jnp.log1p(jnp.exp(-jnp.abs(x)))


def _gla_chunk(rev, q, k, v, la, st):
    C, SB = GLA_CHUNK, GLA_SUB
    ri = lax.broadcasted_iota(jnp.int32, (C, C), 0)
    ci = lax.broadcasted_iota(jnp.int32, (C, C), 1)
    tri = ((ci >= ri) if rev else (ci <= ri)).astype(F32)
    cb = _dot(tri, la, precision=HIGHEST)
    cl = cb[0:1] if rev else cb[C - 1:C]
    o = _dot_nt((q * jnp.exp(cb)).astype(BF16), st.astype(BF16))
    kd = (k * jnp.exp(cl - cb)).astype(BF16)
    bd = (lax.broadcasted_iota(jnp.int32, (256, LANES), 0) // GLA_DV
          == lax.broadcasted_iota(jnp.int32, (256, LANES), 1) // GLA_DK)
    st_new = st * jnp.exp(cl) + jnp.where(bd, _dot_tn(v.astype(BF16), kd), 0.0)

    vb = v.astype(BF16)
    lane_h = lax.broadcasted_iota(jnp.int32, (SB, LANES), 1) // GLA_DK
    col_h = lax.broadcasted_iota(jnp.int32, (SB, 256), 1) // GLA_DV
    expand = (lax.broadcasted_iota(jnp.int32, (LANES, 256), 0) // GLA_DK
              == lax.broadcasted_iota(jnp.int32, (LANES, 256), 1) // GLA_DV).astype(BF16)
    row = lax.broadcasted_iota(jnp.int32, (C, LANES), 0)
    pj = lax.broadcasted_iota(jnp.int32, (SB * SB, LANES), 0) // SB
    pi = lax.broadcasted_iota(jnp.int32, (SB * SB, LANES), 0) % SB
    pmask = (pi <= pj) if rev else (pi >= pj)
    nsub = C // SB
    parts = []
    for i in range(nsub):
        sl = slice(i * SB, (i + 1) * SB)
        qi, ki, cbi, vi = q[sl], k[sl], cb[sl], v[sl]
        acc = o[sl]
        has_off = (i < nsub - 1) if rev else (i > 0)
        if has_off:
            if rev:
                ref = cb[(i + 1) * SB:(i + 1) * SB + 1]
                outside = row >= (i + 1) * SB
            else:
                ref = cb[i * SB - 1:i * SB]
                outside = row < i * SB
            a = qi * jnp.exp(cbi - ref)
            bm = jnp.where(outside, k * jnp.exp(jnp.minimum(ref - cb, 0.0)), 0.0).astype(BF16)
            a4 = jnp.concatenate([jnp.where(lane_h == h, a, 0.0) for h in range(GLA_HEADS)], axis=0)
            att = _dot_nt(a4.astype(BF16), bm)
            pv = _dot(att.astype(BF16), vb)
            for h in range(GLA_HEADS):
                acc = acc + jnp.where(col_h == h, pv[h * SB:(h + 1) * SB], 0.0)
        kj = jnp.broadcast_to(ki[:, None, :], (SB, SB, LANES)).reshape(SB * SB, LANES)
        cbj = jnp.broadcast_to(cbi[:, None, :], (SB, SB, LANES)).reshape(SB * SB, LANES)
        qq = jnp.broadcast_to(qi[None, :, :], (SB, SB, LANES)).reshape(SB * SB, LANES)
        cbq = jnp.broadcast_to(cbi[None, :, :], (SB, SB, LANES)).reshape(SB * SB, LANES)
        x = qq * kj * jnp.where(pmask, jnp.exp(jnp.where(pmask, cbq - cbj, 0.0)), 0.0)
        y = _dot(x.astype(BF16), expand).reshape(SB, SB, 256)
        acc = acc + jnp.sum(y * vi[:, None, :], axis=0)
        parts.append(acc)
    return jnp.concatenate(parts, axis=0), st_new


def _gla_body(S, has_s0, *refs):
    if has_s0:
        (gq_ref, gk_ref, gv_ref, gg_ref, seg_ref, wg_ref, bg_ref, gn_ref, s0_ref,
         o_ref, sfin_ref, la_scr, of_scr, ob_scr, st_scr) = refs
    else:
        (gq_ref, gk_ref, gv_ref, gg_ref, seg_ref, wg_ref, bg_ref, gn_ref,
         o_ref, sfin_ref, la_scr, of_scr, ob_scr, st_scr) = refs
    nc = S // GLA_CHUNK
    seg = seg_ref[...].astype(BF16)
    for d in range(2):
        la_scr[d] = _log_sigmoid(_dot(seg, wg_ref[d]) + bg_ref[d]) * (1.0 / GLA_NORM)
        if has_s0:
            st_scr[d] = s0_ref[0, d]
        else:
            st_scr[d] = jnp.zeros((256, LANES), F32)

    def body(c, carry):
        for d, out_scr in ((0, of_scr), (1, ob_scr)):
            r0 = pl.multiple_of((c if d == 0 else nc - 1 - c) * GLA_CHUNK, GLA_CHUNK)
            rows = pl.ds(r0, GLA_CHUNK)
            q = gq_ref[rows, :] * (GLA_DK ** -0.5)
            o, st = _gla_chunk(d == 1, q, gk_ref[rows, :], gv_ref[rows, :], la_scr[d, rows, :], st_scr[d])
            out_scr[rows, :] = o
            st_scr[d] = st
        return carry

    lax.fori_loop(0, nc, body, 0)
    sfin_ref[0] = st_scr[...]
    o = of_scr[...] + ob_scr[...]
    ones = (lax.broadcasted_iota(jnp.int32, (256, 256), 0) // GLA_DV
            == lax.broadcasted_iota(jnp.int32, (256, 256), 1) // GLA_DV).astype(F32)
    ms = _dot(o * o, ones, precision=HIGHEST) * (1.0 / GLA_DV)
    g = gg_ref[...]
    o_ref[...] = o * lax.rsqrt(ms + EPS) * gn_ref[...] * (g * jax.nn.sigmoid(g))


def _gla(big, seg, lw, s0):
    has_s0 = s0 is not None
    S, nb, base = (S_LAT, B_LAT, T_CTX // S_LAT) if has_s0 else (S_CTX, B_CTX, 0)
    blk = lambda w, c: pl.BlockSpec((S, w), lambda b: (base + b, c))
    in_specs = [blk(LANES, C_GQ // LANES), blk(LANES, C_GK // LANES), blk(256, C_GV // 256), blk(256, C_GG // 256),
                blk(LANES, 0), _const_spec((2, LANES, LANES)), _const_spec((2, 1, LANES)), _const_spec((1, 256))]
    args = [big, big, big, big, seg, lw["wg"], lw["bg"], lw["gla_norm_g"]]
    if has_s0:
        in_specs += [pl.BlockSpec((1, 2, 256, LANES), lambda b: (b, 0, 0, 0))]
        args += [s0]
    return pl.pallas_call(
        functools.partial(_gla_body, S, has_s0),
        grid=(nb,),
        in_specs=in_specs,
        out_specs=[pl.BlockSpec((S, 256), lambda b: (b, 0)), pl.BlockSpec((1, 2, 256, LANES), lambda b: (b, 0, 0, 0))],
        out_shape=[jax.ShapeDtypeStruct((nb * S, 256), F32), jax.ShapeDtypeStruct((nb, 2, 256, LANES), F32)],
        scratch_shapes=[pltpu.VMEM((2, S, LANES), F32), pltpu.VMEM((S, 256), F32), pltpu.VMEM((S, 256), F32),
                        pltpu.VMEM((2, 256, LANES), F32)],
        compiler_params=_cp(48),
        name="gla_lat" if has_s0 else "gla_ctx",
    )(*args)


def _softplus(x):
    return jnp.maximum(x, 0.0) + jnp.log1p(jnp.exp(-jnp.abs(x)))


def _lru_body(S, has_h0, *refs):
    if has_h0:
        (lx_ref, lg_ref, cw_ref, cb_ref, wg_ref, bgate_ref, lam_ref, h0_ref,
         o_ref, hfin_ref, a_scr, b_scr, hf_scr, hb_scr) = refs
    else:
        (lx_ref, lg_ref, cw_ref, cb_ref, wg_ref, bgate_ref, lam_ref,
         o_ref, hfin_ref, a_scr, b_scr, hf_scr, hb_scr) = refs
    x = lx_ref[...]
    row = lax.broadcasted_iota(jnp.int32, (S, LRU_W), 0)
    xm1 = jnp.where(row >= 1, pltpu.roll(x, 1, 0), 0.0)
    xp1 = jnp.where(row < S - 1, pltpu.roll(x, S - 1, 0), 0.0)
    xp2 = jnp.where(row < S - 2, pltpu.roll(x, S - 2, 0), 0.0)
    xc = cb_ref[...] + xm1 * cw_ref[0:1] + x * cw_ref[1:2] + xp1 * cw_ref[2:3] + xp2 * cw_ref[3:4]
    gates = _dot(xc.astype(BF16), wg_ref[...]) + bgate_ref[...]
    for d in range(2):
        r = jax.nn.sigmoid(gates[:, d * LRU_W:(d + 1) * LRU_W])
        ig = jax.nn.sigmoid(gates[:, (2 + d) * LRU_W:(3 + d) * LRU_W])
        log_a = -LRU_C * r * _softplus(-lam_ref[d])
        a_scr[d] = jnp.exp(log_a)
        b_scr[d] = jnp.sqrt(1.0 - jnp.exp(2.0 * log_a)) * (ig * xc)
    if has_h0:
        h_init = (h0_ref[0, 0], h0_ref[0, 1])
    else:
        h_init = (jnp.zeros((1, LRU_W), F32), jnp.zeros((1, LRU_W), F32))

    def body(i, carry):
        hf, hb = carry
        r0 = pl.multiple_of(i * 8, 8)
        rb = pl.multiple_of(S - 8 - i * 8, 8)
        af, bf = a_scr[0, pl.ds(r0, 8), :], b_scr[0, pl.ds(r0, 8), :]
        ab, bb = a_scr[1, pl.ds(rb, 8), :], b_scr[1, pl.ds(rb, 8), :]
        fs, bs = [], [None] * 8
        for t in range(8):
            hf = af[t:t + 1] * hf + bf[t:t + 1]
            fs.append(hf)
            u = 7 - t
            hb = ab[u:u + 1] * hb + bb[u:u + 1]
            bs[u] = hb
        hf_scr[pl.ds(r0, 8), :] = jnp.concatenate(fs, axis=0)
        hb_scr[pl.ds(rb, 8), :] = jnp.concatenate(bs, axis=0)
        return hf, hb

    hf, hb = lax.fori_loop(0, S // 8, body, h_init)
    hfin_ref[0, 0] = hf
    hfin_ref[0, 1] = hb
    g = lg_ref[...]
    gelu = g * (0.5 * (1.0 + jnp.tanh(np.sqrt(2.0 / np.pi).astype(np.float32) * (g + 0.044715 * (g * g * g)))))
    o_ref[...] = (hf_scr[...] + hb_scr[...]) * gelu


def _lru(big, lw, h0):
    has_h0 = h0 is not None
    S, nb, base = (S_LAT, B_LAT, T_CTX // S_LAT) if has_h0 else (S_CTX, B_CTX, 0)
    blk = lambda c: pl.BlockSpec((S, LRU_W), lambda b: (base + b, c))
    in_specs = [blk(C_LX // LRU_W), blk(C_LG // LRU_W), _const_spec((4, LRU_W)), _const_spec((1, LRU_W)),
                _const_spec((LRU_W, 4 * LRU_W)), _const_spec((1, 4 * LRU_W)), _const_spec((2, 1, LRU_W))]
    args = [big, big, lw["lru_conv_w"], lw["lru_conv_b"], lw["lru_wg"], lw["lru_bg"], lw["lru_lambda"]]
    if has_h0:
        in_specs += [pl.BlockSpec((1, 2, 1, LRU_W), lambda b: (b, 0, 0, 0))]
        args += [h0]
    return pl.pallas_call(
        functools.partial(_lru_body, S, has_h0),
        grid=(nb,),
        in_specs=in_specs,
        out_specs=[pl.BlockSpec((S, LRU_W), lambda b: (b, 0)),
                   pl.BlockSpec((1, 2, 1, LRU_W), lambda b: (b, 0, 0, 0))],
        out_shape=[jax.ShapeDtypeStruct((nb * S, LRU_W), F32), jax.ShapeDtypeStruct((nb, 2, 1, LRU_W), F32)],
        scratch_shapes=[pltpu.VMEM((2, S, LRU_W), F32), pltpu.VMEM((2, S, LRU_W), F32),
                        pltpu.VMEM((S, LRU_W), F32), pltpu.VMEM((S, LRU_W), F32)],
        compiler_params=_cp(48),
        name="lru_lat" if has_h0 else "lru_ctx",
    )(*args)


def _merge_body(x_ref, *refs):
    (mod_ref, wout_ref, g2_ref, rwh_ref, rwl_ref, rb_ref, x1_ref, h2_ref, idx_ref, gw_ref) = refs[8:]
    is_ctx = pl.program_id(0) < NT_CTX
    mix = 0.0
    for j in range(4):
        o = jnp.where(is_ctx, refs[2 * j][...], refs[2 * j + 1][...])
        mix = mix + _dot(o.astype(BF16), wout_ref[j * 256:(j + 1) * 256, :])
    m = mod_ref[0]
    g1, sh2, sc2 = m[:, 2 * D:3 * D], m[:, 3 * D:4 * D], m[:, 4 * D:5 * D]
    x1 = x_ref[...] + g1 * mix
    x1_ref[...] = x1
    h2 = _rms(x1) * g2_ref[...] * (1.0 + sc2) + sh2
    for s in range(SUB):
        h2_ref[pl.ds(s, TM, stride=SUB), :] = h2[:, s * LANES:(s + 1) * LANES]
    hi = h2.astype(BF16)
    lo = (h2 - hi.astype(F32)).astype(BF16)
    logits = _dot(hi, rwh_ref[...]) + _dot(hi, rwl_ref[...]) + _dot(lo, rwh_ref[...]) + rb_ref[...]
    lane = lax.broadcasted_iota(jnp.int32, (TM, LANES), 1).astype(F32)
    idx_out = jnp.zeros((TM, LANES), F32)
    val_out = jnp.zeros((TM, LANES), F32)
    top = None
    den = 0.0
    for kk in range(TOP_K):
        mval = logits.max(axis=-1, keepdims=True)
        midx = jnp.where(logits == mval, lane, float(LANES)).min(axis=-1, keepdims=True)
        if kk == 0:
            top = mval
        e = jnp.exp(mval - top)
        den = den + e
        idx_out = jnp.where(lane == kk, midx, idx_out)
        val_out = jnp.where(lane == kk, e, val_out)
        logits = jnp.where(lane == midx, -jnp.inf, logits)
    idx_ref[...] = idx_out.astype(jnp.int32)
    gw_ref[...] = val_out / den


def _merge(x, mix_parts, mods_l, lw):
    tok = lambda w: pl.BlockSpec((TM, w), lambda i: (i, 0))
    ctx = pl.BlockSpec((TM, 256), lambda i: (jnp.minimum(i, NT_CTX - 1), 0))
    lat = pl.BlockSpec((TM, 256), lambda i: (jnp.maximum(i - NT_CTX, 0), 0))
    return pl.pallas_call(
        _merge_body,
        grid=(NT_ALL,),
        in_specs=[tok(D)] + [ctx, lat] * 4 + [
                  pl.BlockSpec((1, 1, 6 * D), lambda i: (_mod_row(i), 0, 0)),
                  _const_spec((D, D)), _const_spec((1, D)), _const_spec((D, LANES)), _const_spec((D, LANES)),
                  _const_spec((1, LANES))],
        out_specs=[tok(D), pl.BlockSpec((TM * SUB, LANES), lambda i: (i, 0)), tok(LANES), tok(LANES)],
        out_shape=[jax.ShapeDtypeStruct((T_ALL, D), F32), jax.ShapeDtypeStruct((T_ALL * SUB, LANES), F32),
                   jax.ShapeDtypeStruct((T_ALL, LANES), jnp.int32), jax.ShapeDtypeStruct((T_ALL, LANES), F32)],
        compiler_params=_cp(40),
        name="merge",
    )(x, *mix_parts, mods_l, lw["w_out"], lw["norm2_g"], lw["rw_hi"], lw["rw_lo"], lw["rb"])


ROW_GROUP = 8
WAIT_BLOCKS = (64, 8, 1)


def _moe_body(sn_ref, te_ref, nv_ref, tb_ref, h2_hbm, wgu_ref, bgu_ref, wd_ref, bd_ref,
              y_hbm, xbuf, obuf, wgu_bf, wd_bf, gsem, ssem):
    i = pl.program_id(0)
    slot = i % 2
    other = 1 - slot
    cur = jnp.minimum(i, NT_MOE - 1)
    nv_cur = jnp.where(i < NT_MOE, nv_ref[cur], 0)
    valid = nv_cur > 0
    prv = jnp.maximum(i - 1, 0)
    nv_prev = jnp.where(i >= 1, nv_ref[prv], 0)
    base_prev = tb_ref[prv]
    nxt = jnp.minimum(i + 1, NT_MOE - 1)
    nv_next = jnp.where(i + 1 < NT_MOE, nv_ref[nxt], 0)
    base_next = tb_ref[nxt]

    def rows(sl, r, n):
        return pl.ds(pl.multiple_of(r * SUB, SUB), n * SUB)

    def gather_row(base, sl, r):
        tok = jnp.bitwise_and(sn_ref[base + r], T_ALL - 1)
        pltpu.make_async_copy(h2_hbm.at[pl.ds(pl.multiple_of(tok * SUB, SUB), SUB)],
                              xbuf.at[sl, rows(sl, r, 1)], gsem.at[sl]).start()

    def scatter_row(sl, r):
        dst = sn_ref[base_prev + r]
        pltpu.make_async_copy(obuf.at[sl, rows(sl, r, 1)],
                              y_hbm.at[pl.ds(pl.multiple_of(dst * SUB, SUB), SUB)], ssem.at[sl]).start()

    def gather_wait(sl, r, n):
        pltpu.make_async_copy(h2_hbm.at[pl.ds(0, n * SUB)], xbuf.at[sl, rows(sl, r, n)], gsem.at[sl]).wait()

    def scatter_wait(sl, r, n):
        pltpu.make_async_copy(obuf.at[sl, rows(sl, r, n)], y_hbm.at[pl.ds(0, n * SUB)], ssem.at[sl]).wait()

    def for_rows(n, one):
        ng = lax.shift_right_logical(n, ROW_GROUP.bit_length() - 1)

        def body(g, c):
            for u in range(ROW_GROUP):
                one(g * ROW_GROUP + u)
            return c
        lax.fori_loop(0, ng, body, 0)
        lax.fori_loop(ng * ROW_GROUP, n, lambda r, c: one(r) or c, 0)

    def for_row_blocks(n, block):
        r0 = 0
        for size in WAIT_BLOCKS:
            cnt = lax.shift_right_logical(n - r0, size.bit_length() - 1)
            lax.fori_loop(0, cnt, lambda g, c, r0=r0, size=size: block(r0 + g * size, size) or c, 0)
            r0 = r0 + cnt * size

    def start_gather(base, sl, n):
        for_rows(n, functools.partial(gather_row, base, sl))

    def start_scatter(sl, n):
        for_rows(n, functools.partial(scatter_row, sl))

    def finish_gather(sl, n):
        for_row_blocks(n, functools.partial(gather_wait, sl))

    def finish_scatter(sl, n):
        for_row_blocks(n, functools.partial(scatter_wait, sl))

    @pl.when(i == 0)
    def _():
        xbuf[...] = jnp.zeros(xbuf.shape, F32)
        start_gather(tb_ref[0], 0, nv_cur)

    e = te_ref[cur]
    prev_e = te_ref[prv]

    @pl.when(valid & ((i == 0) | (e != prev_e)))
    def _():
        wgu_bf[...] = wgu_ref[0, 0].astype(BF16)
        wd_bf[...] = wd_ref[0, 0].astype(BF16)

    @pl.when(valid)
    def _():
        start_gather(base_next, other, nv_next)
        start_scatter(other, nv_prev)
        finish_gather(slot, nv_cur)
        x = jnp.concatenate([xbuf[slot, pl.ds(s, TM_MOE, stride=SUB), :] for s in range(SUB)], axis=-1).astype(BF16)
        acc = jnp.zeros((TM_MOE, D), F32)
        for j in range(D_EXP // EXP_CHUNK):
            c0 = j * EXP_CHUNK
            xg = _dot(x, wgu_bf[:, c0:c0 + EXP_CHUNK]) + bgu_ref[0, 0][:, c0:c0 + EXP_CHUNK]
            xl = _dot(x, wgu_bf[:, D_EXP + c0:D_EXP + c0 + EXP_CHUNK]) + bgu_ref[0, 0][:, D_EXP + c0:D_EXP + c0 + EXP_CHUNK]
            xg = jnp.minimum(xg, SWIGLU_LIMIT)
            xl = jnp.clip(xl, -SWIGLU_LIMIT, SWIGLU_LIMIT)
            act = xg * jax.nn.sigmoid(SWIGLU_ALPHA * xg) * (xl + 1.0)
            acc = acc + _dot(act.astype(BF16), wd_bf[c0:c0 + EXP_CHUNK, :])
        out = acc + bd_ref[0, 0]
        for s in range(SUB):
            obuf[slot, pl.ds(s, TM_MOE, stride=SUB), :] = out[:, s * LANES:(s + 1) * LANES]
        finish_scatter(other, nv_prev)

    @pl.when(jnp.logical_not(valid) & (nv_prev > 0))
    def _():
        start_scatter(other, nv_prev)
        finish_scatter(other, nv_prev)


def _moe(sorted_n, tile_exp, tile_nvalid, tile_base, h2, l, p):
    wmap = lambda i, sn, te, nv, tb: (l, te[jnp.minimum(i, NT_MOE - 1)], 0, 0)
    grid_spec = pltpu.PrefetchScalarGridSpec(
        num_scalar_prefetch=4,
        grid=(NT_MOE + 1,),
        in_specs=[
            pl.BlockSpec(memory_space=pl.ANY),
            pl.BlockSpec((1, 1, D, 2 * D_EXP), wmap),
            pl.BlockSpec((1, 1, 1, 2 * D_EXP), wmap),
            pl.BlockSpec((1, 1, D_EXP, D), wmap),
            pl.BlockSpec((1, 1, 1, D), wmap),
        ],
        out_specs=pl.BlockSpec(memory_space=pl.ANY),
        scratch_shapes=[pltpu.VMEM((2, TM_MOE * SUB, LANES), F32), pltpu.VMEM((2, TM_MOE * SUB, LANES), F32),
                        pltpu.VMEM((D, 2 * D_EXP), BF16), pltpu.VMEM((D_EXP, D), BF16),
                        pltpu.SemaphoreType.DMA((2,)), pltpu.SemaphoreType.DMA((2,))],
    )
    return pl.pallas_call(
        _moe_body,
        grid_spec=grid_spec,
        out_shape=jax.ShapeDtypeStruct((N_ROWS * SUB, LANES), F32),
        compiler_params=pltpu.CompilerParams(vmem_limit_bytes=56 * 1024 * 1024, dimension_semantics=("arbitrary",),
                                             disable_bounds_checks=True),
        name="moe",
    )(sorted_n, tile_exp, tile_nvalid, tile_base, h2,
      p["exp_w_gu"], p["exp_b_gu"].reshape(DEPTH, N_EXP, 1, 2 * D_EXP),
      p["exp_w_down"], p["exp_b_down"].reshape(DEPTH, N_EXP, 1, D))


def _route(idx):
    flat = idx.T.reshape(-1)
    ids = jnp.arange(N_ROWS, dtype=jnp.int32)
    sorted_n = jnp.sort(flat * N_ROWS + ids) % N_ROWS
    experts = jnp.arange(N_EXP, dtype=jnp.int32)
    counts = jnp.sum((flat[None, :] == experts[:, None]).astype(jnp.int32), axis=1)
    padded = ((counts + TM_MOE - 1) // TM_MOE) * TM_MOE
    pad_ends = jnp.cumsum(padded)
    pad_starts = pad_ends - padded
    unp_starts = jnp.cumsum(counts) - counts
    tile_start = jnp.arange(NT_MOE, dtype=jnp.int32) * TM_MOE
    tile_exp = jnp.minimum(jnp.sum((pad_ends[None, :] <= tile_start[:, None]).astype(jnp.int32), axis=1), N_EXP - 1)
    onehot = (tile_exp[:, None] == experts[None, :]).astype(jnp.int32)
    pick = lambda v: jnp.sum(onehot * v[None, :], axis=1)
    offset = tile_start - pick(pad_starts)
    tile_nvalid = jnp.where(tile_start < pad_ends[-1], jnp.clip(pick(counts) - offset, 0, TM_MOE), 0)
    tile_base = jnp.clip(pick(unp_starts) + offset, 0, N_ROWS - 1)
    return sorted_n, tile_exp, tile_nvalid.astype(jnp.int32), tile_base.astype(jnp.int32)


def _final_body(x_ref, y0_ref, y1_ref, y2_ref, y3_ref, gw_ref, mod_ref, g_ref, o_ref):
    g2 = mod_ref[0][:, 5 * D:6 * D]
    o_ref[...] = _rms(x_ref[...] + g2 * _sum_slots((y0_ref, y1_ref, y2_ref, y3_ref), gw_ref)) * g_ref[...]


def _final(x1, moe, gw, mods_l, g, tile0, ntiles):
    tok = pl.BlockSpec((TM, D), lambda i: (tile0 + i, 0))
    return pl.pallas_call(
        _final_body,
        grid=(ntiles,),
        in_specs=[tok] + _slot_specs(tile0) + [
                  pl.BlockSpec((TM, LANES), lambda i: (tile0 + i, 0)),
                  pl.BlockSpec((1, 1, 6 * D), lambda i: (_mod_row(tile0 + i), 0, 0)), _const_spec((1, D))],
        out_specs=pl.BlockSpec((TM, D), lambda i: (i, 0)),
        out_shape=jax.ShapeDtypeStruct((ntiles * TM, D), F32),
        name="final_norm",
    )(x1, *([moe] * TOP_K), gw, mods_l, g)


def _rope_perm(w):
    parts = []
    for a in range(2):
        x1, x2 = w[..., a * 16:a * 16 + 8], w[..., a * 16 + 8:a * 16 + 16]
        parts += [-x2, x1]
    return jnp.concatenate(parts, axis=-1)


def _rope_tables():
    t = np.arange(S_LAT)
    freqs = ROPE_BASE ** (-np.arange(8, dtype=np.float64) / 8)
    cs, sn = [], []
    for pos in (t // GRID_W, t % GRID_W):
        ang = pos.astype(np.float64)[:, None] * freqs[None, :]
        cs += [np.cos(ang), np.cos(ang)]
        sn += [np.sin(ang), np.sin(ang)]
    cos = np.concatenate([np.ones((ID_ROWS, 32), np.float32), np.concatenate(cs, -1).astype(np.float32)], 0)
    sin = np.concatenate([np.zeros((ID_ROWS, 32), np.float32), np.concatenate(sn, -1).astype(np.float32)], 0)
    n = cos.shape[0]
    one, zero = np.ones((n, 1), np.float32), np.zeros((n, 1), np.float32)
    cq_h = np.concatenate([np.repeat(one, 64, 1), cos, np.repeat(zero, 32, 1)], 1)
    sq_h = np.concatenate([np.repeat(zero, 64, 1), sin, np.repeat(zero, 32, 1)], 1)
    cseg = np.concatenate([cos, np.repeat(one, 96, 1)], 1)
    sseg = np.concatenate([sin, np.repeat(zero, 96, 1)], 1)
    return dict(cq=jnp.asarray(np.tile(cq_h, (1, 4))), sq=jnp.asarray(np.tile(sq_h, (1, 4))),
                cs=jnp.asarray(cseg), ss=jnp.asarray(sseg))


def _na_bias_table(rpb):
    qc = np.arange(GRID_W)[:, None]
    kc = np.arange(GRID_W)[None, :]
    cs = np.clip(qc - NA_WIN_W // 2, 0, GRID_W - NA_WIN_W)
    valid = (kc >= cs) & (kc < cs + NA_WIN_W)
    pad = GRID_W - NA_WIN_W
    ext = jnp.pad(rpb, ((0, 0), (0, 0), (pad, pad)))
    toe = jnp.stack([ext[:, :, GRID_W - 1 - q:2 * GRID_W - 1 - q] for q in range(GRID_W)], axis=2)
    toe = jnp.where(valid[None, None], toe, NEG)
    tab = jnp.stack([toe[:, o:o + NA_WIN_H] for o in range(NA_WIN_H)], axis=0)
    return tab.transpose(0, 1, 3, 2, 4).reshape(NA_WIN_H, NA_HEADS, GRID_W, NK_LOC).astype(F32)


def _layer_weights(l, p):
    w_in = p["w_in"][l]
    z = lambda n: jnp.zeros((D, n), F32)
    kr = w_in[:, 320:352]
    w_perm = jnp.concatenate([
        w_in[:, 352:1120], w_in[:, 1120:1248], w_in[:, 1248:1376], w_in[:, 1376:1632], w_in[:, 1632:1888],
        w_in[:, 1920:2176], w_in[:, 2176:2432], w_in[:, 192:320], w_in[:, 0:192], z(64),
        kr, w_in[:, 1888:1920], z(64), _rope_perm(kr), z(96)], axis=1).astype(BF16)
    wuq = p["mla_w_uq"][l].reshape(MLA_Q_LORA, MLA_HEADS, MLA_NOPE + MLA_ROPE)
    zq = jnp.zeros((MLA_Q_LORA, MLA_HEADS, 32), F32)
    wq1 = jnp.concatenate([wuq, zq], axis=-1).reshape(MLA_Q_LORA, 512)
    wq2 = jnp.concatenate([jnp.zeros((MLA_Q_LORA, MLA_HEADS, MLA_NOPE), F32), _rope_perm(wuq[..., MLA_NOPE:]), zq],
                          axis=-1).reshape(MLA_Q_LORA, 512)
    wq = jnp.pad(jnp.concatenate([wq1, wq2], axis=1), ((0, 256 - MLA_Q_LORA), (0, 0))).astype(BF16)
    qg = jnp.pad(p["mla_q_norm_g"][l], (0, 256 - MLA_Q_LORA)).reshape(1, 256)
    wukv = p["mla_w_ukv"][l].reshape(MLA_KV_LORA, MLA_HEADS, MLA_NOPE + MLA_V)
    wk = jnp.concatenate([wukv[..., :MLA_NOPE], jnp.zeros((MLA_KV_LORA, MLA_HEADS, 64), F32)], -1).reshape(MLA_KV_LORA, 512)
    wv = wukv[..., MLA_NOPE:].reshape(MLA_KV_LORA, 256)
    eye = jnp.eye(MLA_ROPE, dtype=F32)[:, None, :]
    ek = jnp.concatenate([jnp.zeros((MLA_ROPE, MLA_HEADS, MLA_NOPE), F32),
                          jnp.broadcast_to(eye, (MLA_ROPE, MLA_HEADS, MLA_ROPE)),
                          jnp.zeros((MLA_ROPE, MLA_HEADS, 32), F32)], -1).reshape(MLA_ROPE, 512)
    wkv = jnp.concatenate([
        jnp.concatenate([wk, wv], axis=1),
        jnp.concatenate([ek, jnp.zeros((MLA_ROPE, 256), F32)], axis=1),
        jnp.zeros((LANES - MLA_ROPE, 768), F32)], axis=0).astype(BF16)
    wg = jnp.stack([jnp.pad(p["gla_w_gate"][l, d], ((SEG_GA + d * GLA_RANK, LANES - SEG_GA - (d + 1) * GLA_RANK), (0, 0)))
                    for d in range(2)])
    def bdiag(w):
        eye = jnp.eye(LRU_BLOCKS, dtype=F32)
        return (w[:, :, None, :] * eye[:, None, :, None]).reshape(LRU_W, LRU_W)
    lru_wg = jnp.concatenate([bdiag(p["lru_w_a"][l, 0]), bdiag(p["lru_w_a"][l, 1]),
                              bdiag(p["lru_w_x"][l, 0]), bdiag(p["lru_w_x"][l, 1])], axis=1).astype(BF16)
    lru_bg = jnp.concatenate([p["lru_b_a"][l, 0], p["lru_b_a"][l, 1], p["lru_b_x"][l, 0], p["lru_b_x"][l, 1]]).reshape(1, 4 * LRU_W)
    rw = jnp.pad(p["router_w"][l], ((0, 0), (0, LANES - N_EXP)))
    rw_hi = rw.astype(BF16)
    rw_lo = (rw - rw_hi.astype(F32)).astype(BF16)
    rb = jnp.concatenate([p["router_b"][l], jnp.full((LANES - N_EXP,), -jnp.inf, F32)]).reshape(1, LANES)
    return dict(
        w_in=w_perm, norm1_g=p["norm1_g"][l].reshape(1, D), norm2_g=p["norm2_g"][l].reshape(1, D),
        qg=qg, wq=wq, kvg=p["mla_kv_norm_g"][l].reshape(1, LANES), wkv=wkv,
        wg=wg.astype(BF16), bg=p["gla_b_gate"][l].reshape(2, 1, LANES), gla_norm_g=p["gla_norm_g"][l].reshape(1, 256),
        lru_conv_w=p["lru_conv_w"][l], lru_conv_b=p["lru_conv_b"][l].reshape(1, LRU_W), lru_wg=lru_wg, lru_bg=lru_bg,
        lru_lambda=p["lru_lambda"][l].reshape(2, 1, LRU_W),
        w_out=p["w_out"][l].astype(BF16), rw_hi=rw_hi, rw_lo=rw_lo, rb=rb,
        na_bias=_na_bias_table(p["na_rpb"][l]),
    )


def _state_to_blockdiag(s):
    eye = jnp.eye(GLA_HEADS, dtype=s.dtype)
    return jnp.einsum("bdhkv,hg->bdhvgk", s, eye).reshape(s.shape[0], 2, 256, LANES)


def _blockdiag_to_state(st):
    b = st.shape[0]
    s6 = st.reshape(b, 2, GLA_HEADS, GLA_DV, GLA_HEADS, GLA_DK)
    diag = jnp.stack([s6[:, :, h, :, h, :] for h in range(GLA_HEADS)], axis=2)
    return diag.transpose(0, 1, 2, 4, 3)


def kernel(x_prompt, x_sample, cache_mla_ckv, cache_mla_krope, cache_na_k, cache_na_v, state_gla, state_lru, c, c_ctx, norm1_g, norm2_g, w_mod, b_mod, w_in, w_out, mla_q_norm_g, mla_w_uq, mla_kv_norm_g, mla_w_ukv, na_rpb, gla_w_gate, gla_b_gate, gla_norm_g, lru_conv_w, lru_conv_b, lru_w_a, lru_b_a, lru_w_x, lru_b_x, lru_lambda, router_w, router_b, exp_w_gu, exp_b_gu, exp_w_down, exp_b_down, final_norm_g):
    p = dict(norm1_g=norm1_g, norm2_g=norm2_g, w_in=w_in, w_out=w_out, mla_q_norm_g=mla_q_norm_g, mla_w_uq=mla_w_uq,
             mla_kv_norm_g=mla_kv_norm_g, mla_w_ukv=mla_w_ukv, na_rpb=na_rpb, gla_w_gate=gla_w_gate,
             gla_b_gate=gla_b_gate, gla_norm_g=gla_norm_g, lru_conv_w=lru_conv_w, lru_conv_b=lru_conv_b,
             lru_w_a=lru_w_a, lru_b_a=lru_b_a, lru_w_x=lru_w_x, lru_b_x=lru_b_x, lru_lambda=lru_lambda,
             router_w=router_w, router_b=router_b, exp_w_gu=exp_w_gu, exp_b_gu=exp_b_gu, exp_w_down=exp_w_down,
             exp_b_down=exp_b_down)
    x = jnp.concatenate([x_prompt.reshape(T_CTX, D), x_sample.reshape(T_LAT, D)], axis=0)
    crow = jnp.concatenate([c_ctx[None, :], c, jnp.zeros((8 - 1 - B_LAT, D), F32)], axis=0)
    mods = _modulation(crow, w_mod, b_mod).reshape(DEPTH, 8, 1, 6 * D)
    tabs = _rope_tables()

    col = [[] for _ in range(6)]
    moe = None
    x1 = None
    for l in range(DEPTH):
        lw = _layer_weights(l, p)
        if l == 0:
            x, big, seg, ckv, q, k, v = _proj(True, x, None, None, None, mods[l], lw, tabs)
        else:
            x, big, seg, ckv, q, k, v = _proj(False, x1, moe, gw, mods[l - 1], mods[l], lw, tabs)
        kr_c = jnp.pad(cache_mla_krope[:, l].reshape(B_LAT * PAST, MLA_ROPE), ((0, 0), (0, LANES - MLA_ROPE)))
        kc, vc = _cache_kv(cache_mla_ckv[:, l].reshape(B_LAT * PAST, MLA_KV_LORA), kr_c, lw["wkv"])
        oa_c, ob_c = _ctx_attn(q, k, v, big)
        oa_l = _lat_mla(q, k, v, kc, vc)
        ob_l = _lat_na(big, cache_na_k[:, l].reshape(B_LAT * PAST, 256), cache_na_v[:, l].reshape(B_LAT * PAST, 256),
                       lw["na_bias"])
        oc_c, sg = _gla(big, seg, lw, None)
        oc_l, _ = _gla(big, seg, lw, _state_to_blockdiag(state_gla[:, l]))
        od_c, sl = _lru(big, lw, None)
        od_l, _ = _lru(big, lw, state_lru[:, l].reshape(B_LAT, 2, 1, LRU_W))
        x1, h2, idx, gw = _merge(x, (oa_c, oa_l, ob_c, ob_l, oc_c, oc_l, od_c, od_l), mods[l], lw)
        moe = _moe(*_route(idx[:, :TOP_K]), h2, l, p)
        col[0].append(ckv[:T_CTX].reshape(B_CTX, S_CTX, MLA_KV_LORA))
        col[1].append(seg[:T_CTX, :MLA_ROPE].reshape(B_CTX, S_CTX, MLA_ROPE))
        col[2].append(big[:T_CTX, C_NAK:C_NAK + 256].reshape(B_CTX, S_CTX, NA_HEADS, NA_DH))
        col[3].append(big[:T_CTX, C_NAV:C_NAV + 256].reshape(B_CTX, S_CTX, NA_HEADS, NA_DH))
        col[4].append(_blockdiag_to_state(sg))
        col[5].append(sl.reshape(B_CTX, 2, LRU_W))
    fg = final_norm_g.reshape(1, D)
    y_prompt = _final(x1, moe, gw, mods[DEPTH - 1], fg, 0, NT_CTX).reshape(B_CTX, S_CTX, D)
    y_sample = _final(x1, moe, gw, mods[DEPTH - 1], fg, NT_CTX, NT_ALL - NT_CTX).reshape(B_LAT, S_LAT, D)
    return (y_prompt, y_sample) + tuple(jnp.stack(cl, axis=1) for cl in col)
```

```python
import functools

import numpy as np
import jax
import jax.numpy as jnp
from jax import lax
from jax.experimental import pallas as pl
from jax.experimental.pallas import tpu as pltpu

F32 = jnp.float32
BF16 = jnp.bfloat16
HIGHEST = lax.Precision.HIGHEST

D = 1024
DEPTH = 2
B_CTX, S_CTX = 32, 256
B_LAT, S_LAT = 4, 2048
PAST = 512
T_CTX = B_CTX * S_CTX
T_LAT = B_LAT * S_LAT
T_ALL = T_CTX + T_LAT
GRID_W = 64
ROWS = S_LAT // GRID_W

MLA_HEADS, MLA_NOPE, MLA_ROPE, MLA_V = 4, 64, 32, 64
MLA_Q_LORA, MLA_KV_LORA = 192, 128
MLA_SCALE = (MLA_NOPE + MLA_ROPE) ** -0.5
NA_HEADS, NA_DH, NA_WIN_H, NA_WIN_W = 4, 64, 8, 16
NA_SCALE = NA_DH ** -0.5
GLA_HEADS, GLA_DK, GLA_DV, GLA_RANK, GLA_NORM, GLA_CHUNK = 4, 32, 64, 16, 16.0, 64
GLA_SUB = 16
LRU_W, LRU_BLOCKS, LRU_C = 256, 4, 8.0
N_EXP, TOP_K, D_EXP = 32, 4, 1024
SWIGLU_ALPHA, SWIGLU_LIMIT = 1.702, 7.0
ROPE_BASE = 10000.0
EPS = 1e-6
NEG = -1e30

LANES = 128
VMEM_BYTES = 64 * 1024 * 1024

TM = 256
NT_CTX = T_CTX // TM
NT_ALL = T_ALL // TM
TILES_PER_LAT = S_LAT // TM
ID_ROWS = 2 * TM

TM_MOE = 256
N_ROWS = T_ALL * TOP_K
R_MAX = N_ROWS + N_EXP * TM_MOE
NT_MOE = R_MAX // TM_MOE
EXP_CHUNK = 256
N_DUMP = 2 * TM_MOE
SUB = D // LANES
assert T_ALL & (T_ALL - 1) == 0

C_NAQ, C_NAK, C_NAV, C_GQ, C_GK, C_GV, C_GG, C_LX, C_LG = 0, 256, 512, 768, 896, 1024, 1280, 1536, 1792
BIG = 2048
C_PKV, C_PQ, C_SEGA, C_SEGB = 2048, 2176, 2432, 2560
NC_IN = 2688
SEG_GA = 32


def _cp(vmem_mb, sem=None):
    kw = dict(vmem_limit_bytes=vmem_mb * 1024 * 1024)
    if sem is not None:
        kw["dimension_semantics"] = sem
    return pltpu.CompilerParams(**kw)


def _rms(x):
    return x * lax.rsqrt(jnp.mean(x * x, axis=-1, keepdims=True) + EPS)


def _dot(a, b, **kw):
    return jnp.dot(a, b, preferred_element_type=F32, **kw)


def _dot_nt(a, b):
    return lax.dot_general(a, b, (((1,), (1,)), ((), ())), preferred_element_type=F32)


def _dot_tn(a, b):
    return lax.dot_general(a, b, (((0,), (0,)), ((), ())), preferred_element_type=F32)


def _const_spec(shape):
    n = len(shape)
    return pl.BlockSpec(shape, lambda *_: (0,) * n)


def _mod_body(c_ref, w_ref, b_ref, o_ref):
    c = c_ref[...]
    s = c * jax.nn.sigmoid(c)
    o_ref[0] = _dot(s.astype(BF16), w_ref[0].astype(BF16)) + b_ref[0]


def _modulation(crow, w_mod, b_mod):
    tn = 1536
    return pl.pallas_call(
        _mod_body,
        grid=(DEPTH, 6 * D // tn),
        in_specs=[
            pl.BlockSpec((8, D), lambda l, j: (0, 0)),
            pl.BlockSpec((1, D, tn), lambda l, j: (l, 0, j)),
            pl.BlockSpec((1, 1, tn), lambda l, j: (l, 0, j)),
        ],
        out_specs=pl.BlockSpec((1, 8, tn), lambda l, j: (l, 0, j)),
        out_shape=jax.ShapeDtypeStruct((DEPTH, 8, 6 * D), F32),
        compiler_params=_cp(40),
        name="modulation",
    )(crow, w_mod, b_mod.reshape(DEPTH, 1, 6 * D))


def _mod_row(i):
    return jnp.where(i < NT_CTX, 0, 1 + (i - NT_CTX) // TILES_PER_LAT)


def _pos_block(i):
    return jnp.where(i < NT_CTX, 0, ID_ROWS // TM + (i - NT_CTX) % TILES_PER_LAT)


def _sum_slots(y_refs, gw_ref):
    gw = gw_ref[...]
    cols = [gw[:, kk:kk + 1] for kk in range(TOP_K)]
    parts = []
    for s in range(SUB):
        acc = cols[0] * y_refs[0][pl.ds(s, TM, stride=SUB), :]
        for kk in range(1, TOP_K):
            acc = acc + cols[kk] * y_refs[kk][pl.ds(s, TM, stride=SUB), :]
        parts.append(acc)
    return jnp.concatenate(parts, axis=-1)


def _slot_specs(tile0):
    return [pl.BlockSpec((TM * SUB, LANES), functools.partial(lambda kk, i: (kk * NT_ALL + tile0 + i, 0), kk))
            for kk in range(TOP_K)]


def _proj_body(first, *refs):
    if first:
        (x_ref, mod_ref, g1_ref, win_ref, qg_ref, wq_ref, kvg_ref, wkv_ref, cq_ref, sq_ref, cs_ref, ss_ref,
         big_ref, seg_ref, ckv_ref, q_ref, k_ref, v_ref) = refs
        x = x_ref[...]
    else:
        (x_ref, y0_ref, y1_ref, y2_ref, y3_ref, gw_ref, modp_ref, mod_ref, g1_ref, win_ref, qg_ref, wq_ref, kvg_ref, wkv_ref, cq_ref, sq_ref,
         cs_ref, ss_ref, xo_ref, big_ref, seg_ref, ckv_ref, q_ref, k_ref, v_ref) = refs
        g2 = modp_ref[0][:, 5 * D:6 * D]
        x = x_ref[...] + g2 * _sum_slots((y0_ref, y1_ref, y2_ref, y3_ref), gw_ref)
        xo_ref[...] = x
    m = mod_ref[0]
    sh1, sc1 = m[:, 0:D], m[:, D:2 * D]
    h = _rms(x) * g1_ref[...] * (1.0 + sc1) + sh1
    p = _dot(h.astype(BF16), win_ref[...])
    big_ref[...] = p[:, :BIG]
    ckv = _rms(p[:, C_PKV:C_PKV + MLA_KV_LORA]) * kvg_ref[...]
    ckv_ref[...] = ckv
    pq = p[:, C_PQ:C_PQ + 256]
    qn = pq * lax.rsqrt(jnp.sum(pq * pq, axis=-1, keepdims=True) * (1.0 / MLA_Q_LORA) + EPS) * qg_ref[...]
    a = _dot(qn.astype(BF16), wq_ref[...])
    q_ref[...] = (a[:, :512] * cq_ref[...] + a[:, 512:] * sq_ref[...]).astype(BF16)
    seg = p[:, C_SEGA:C_SEGA + LANES] * cs_ref[...] + p[:, C_SEGB:C_SEGB + LANES] * ss_ref[...]
    seg_ref[...] = seg
    kv = _dot(jnp.concatenate([ckv, seg], axis=-1).astype(BF16), wkv_ref[...])
    k_ref[...] = kv[:, :512].astype(BF16)
    v_ref[...] = kv[:, 512:].astype(BF16)


def _proj(first, x, moe, gw, mods_prev, mods_l, lw, tabs):
    tok = lambda w: pl.BlockSpec((TM, w), lambda i: (i, 0))
    modspec = pl.BlockSpec((1, 1, 6 * D), lambda i: (_mod_row(i), 0, 0))
    pos = lambda w: pl.BlockSpec((TM, w), lambda i: (_pos_block(i), 0))
    in_specs = [tok(D)]
    args = [x]
    if not first:
        in_specs += _slot_specs(0) + [tok(LANES), modspec]
        args += [moe] * TOP_K + [gw, mods_prev]
    in_specs += [modspec, _const_spec((1, D)), _const_spec((D, NC_IN)), _const_spec((1, 256)),
                 _const_spec((256, 1024)), _const_spec((1, LANES)), _const_spec((256, 768)),
                 pos(512), pos(512), pos(LANES), pos(LANES)]
    args += [mods_l, lw["norm1_g"], lw["w_in"], lw["qg"], lw["wq"], lw["kvg"], lw["wkv"],
             tabs["cq"], tabs["sq"], tabs["cs"], tabs["ss"]]
    out_specs = [tok(BIG), tok(LANES), tok(LANES), tok(512), tok(512), tok(256)]
    out_shape = [jax.ShapeDtypeStruct((T_ALL, BIG), F32), jax.ShapeDtypeStruct((T_ALL, LANES), F32),
                 jax.ShapeDtypeStruct((T_ALL, LANES), F32), jax.ShapeDtypeStruct((T_ALL, 512), BF16),
                 jax.ShapeDtypeStruct((T_ALL, 512), BF16), jax.ShapeDtypeStruct((T_ALL, 256), BF16)]
    if not first:
        out_specs = [tok(D)] + out_specs
        out_shape = [jax.ShapeDtypeStruct((T_ALL, D), F32)] + out_shape
    outs = pl.pallas_call(
        functools.partial(_proj_body, first),
        grid=(NT_ALL,),
        in_specs=in_specs,
        out_specs=out_specs,
        out_shape=out_shape,
        compiler_params=_cp(48),
        name="proj",
    )(*args)
    if first:
        return (x,) + tuple(outs)
    return tuple(outs)


def _kvc_body(ckv_ref, kr_ref, wkv_ref, k_ref, v_ref):
    kv = _dot(jnp.concatenate([ckv_ref[...], kr_ref[...]], axis=-1).astype(BF16), wkv_ref[...])
    k_ref[...] = kv[:, :512].astype(BF16)
    v_ref[...] = kv[:, 512:].astype(BF16)


def _cache_kv(ckv_c, kr_c, wkv):
    n = B_LAT * PAST
    return pl.pallas_call(
        _kvc_body,
        grid=(B_LAT,),
        in_specs=[pl.BlockSpec((PAST, LANES), lambda b: (b, 0)), pl.BlockSpec((PAST, LANES), lambda b: (b, 0)),
                  _const_spec((256, 768))],
        out_specs=[pl.BlockSpec((PAST, 512), lambda b: (b, 0)), pl.BlockSpec((PAST, 256), lambda b: (b, 0))],
        out_shape=[jax.ShapeDtypeStruct((n, 512), BF16), jax.ShapeDtypeStruct((n, 256), BF16)],
        name="cache_kv",
    )(ckv_c, kr_c, wkv)


def _softmax_pv(scores, values):
    m = scores[0].max(axis=-1, keepdims=True)
    for s in scores[1:]:
        m = jnp.maximum(m, s.max(axis=-1, keepdims=True))
    den = 0.0
    acc = 0.0
    for s, v in zip(scores, values):
        e = jnp.exp(s - m)
        den = den + e.sum(axis=-1, keepdims=True)
        acc = acc + _dot(e.astype(BF16), v)
    return acc / den


def _lane_lo(shape):
    return lax.broadcasted_iota(jnp.int32, shape, len(shape) - 1) < 64


def _ctx_attn_body(q_ref, k_ref, v_ref, na_ref, oa_ref, ob_ref):
    lo = _lane_lo((S_CTX, LANES))
    for pp in range(2):
        vp = v_ref[:, pp * LANES:(pp + 1) * LANES]
        outs = []
        for h in (2 * pp, 2 * pp + 1):
            s = _dot_nt(q_ref[:, h * LANES:(h + 1) * LANES], k_ref[:, h * LANES:(h + 1) * LANES]) * MLA_SCALE
            outs.append(_softmax_pv([s], [vp]))
        oa_ref[:, pp * LANES:(pp + 1) * LANES] = jnp.where(lo, outs[0], outs[1])
    for pp in range(2):
        qp = na_ref[:, C_NAQ + pp * LANES:C_NAQ + (pp + 1) * LANES]
        kp = na_ref[:, C_NAK + pp * LANES:C_NAK + (pp + 1) * LANES].astype(BF16)
        vp = na_ref[:, C_NAV + pp * LANES:C_NAV + (pp + 1) * LANES].astype(BF16)
        outs = []
        for sel in (lo, ~lo):
            qh = jnp.where(sel, qp, 0.0).astype(BF16)
            outs.append(_softmax_pv([_dot_nt(qh, kp) * NA_SCALE], [vp]))
        ob_ref[:, pp * LANES:(pp + 1) * LANES] = jnp.where(lo, outs[0], outs[1])


def _ctx_attn(q, k, v, big):
    blk = lambda w: pl.BlockSpec((S_CTX, w), lambda b: (b, 0))
    return pl.pallas_call(
        _ctx_attn_body,
        grid=(B_CTX,),
        in_specs=[blk(512), blk(512), blk(256), blk(768)],
        out_specs=[blk(256), blk(256)],
        out_shape=[jax.ShapeDtypeStruct((T_CTX, 256), F32), jax.ShapeDtypeStruct((T_CTX, 256), F32)],
        name="ctx_attn",
    )(q, k, v, big)


TQ = 256


def _lat_mla_body(q_ref, kl_ref, vl_ref, kc_ref, vc_ref, o_ref):
    lo = _lane_lo((TQ, LANES))
    for pp in range(2):
        vlp = vl_ref[:, pp * LANES:(pp + 1) * LANES]
        vcp = vc_ref[:, pp * LANES:(pp + 1) * LANES]
        outs = []
        for h in (2 * pp, 2 * pp + 1):
            qh = q_ref[:, h * LANES:(h + 1) * LANES]
            s_l = _dot_nt(qh, kl_ref[:, h * LANES:(h + 1) * LANES]) * MLA_SCALE
            s_c = _dot_nt(qh, kc_ref[:, h * LANES:(h + 1) * LANES]) * MLA_SCALE
            outs.append(_softmax_pv([s_l, s_c], [vlp, vcp]))
        o_ref[:, pp * LANES:(pp + 1) * LANES] = jnp.where(lo, outs[0], outs[1])


def _lat_mla(q, k, v, kc, vc):
    nq = S_LAT // TQ
    qrow = lambda b, t: (T_CTX // TQ + b * nq + t, 0)
    lat = lambda b, t: (T_CTX // S_LAT + b, 0)
    return pl.pallas_call(
        _lat_mla_body,
        grid=(B_LAT, nq),
        in_specs=[pl.BlockSpec((TQ, 512), qrow), pl.BlockSpec((S_LAT, 512), lat), pl.BlockSpec((S_LAT, 256), lat),
                  pl.BlockSpec((PAST, 512), lambda b, t: (b, 0)), pl.BlockSpec((PAST, 256), lambda b, t: (b, 0))],
        out_specs=pl.BlockSpec((TQ, 256), lambda b, t: (b * nq + t, 0)),
        out_shape=jax.ShapeDtypeStruct((T_LAT, 256), F32),
        compiler_params=_cp(48),
        name="lat_mla",
    )(q, k, v, kc, vc)


NK_LOC = NA_WIN_H * GRID_W


def _na_row_start(r):
    return jnp.clip(r - NA_WIN_H // 2, 0, ROWS - NA_WIN_H)


def _lat_na_body(q_ref, k_ref, v_ref, kc_ref, vc_ref, bias_ref, o_ref):
    r = pl.program_id(1)
    k0 = pl.multiple_of(_na_row_start(r) * GRID_W, GRID_W)
    lo = _lane_lo((GRID_W, LANES))
    for pp in range(2):
        cols = slice(pp * LANES, (pp + 1) * LANES)
        qp = q_ref[:, cols]
        kl = k_ref[pl.ds(k0, NK_LOC), cols].astype(BF16)
        vl = v_ref[pl.ds(k0, NK_LOC), cols].astype(BF16)
        kc = kc_ref[:, cols].astype(BF16)
        vc = vc_ref[:, cols].astype(BF16)
        outs = []
        for j, sel in enumerate((lo, ~lo)):
            qh = jnp.where(sel, qp, 0.0).astype(BF16)
            s_loc = _dot_nt(qh, kl) * NA_SCALE + bias_ref[0, 2 * pp + j]
            s_ctx = _dot_nt(qh, kc) * NA_SCALE
            outs.append(_softmax_pv([s_loc, s_ctx], [vl, vc]))
        o_ref[:, cols] = jnp.where(lo, outs[0], outs[1])


def _lat_na(big, kc, vc, bias):
    qrow = lambda b, r: (T_CTX // GRID_W + b * ROWS + r, 0)
    boff = lambda b, r: (_na_row_start(r) - r + NA_WIN_H - 1, 0, 0, 0)
    lat = lambda c: (lambda b, r: (T_CTX // S_LAT + b, c))
    return pl.pallas_call(
        _lat_na_body,
        grid=(B_LAT, ROWS),
        in_specs=[pl.BlockSpec((GRID_W, 256), qrow), pl.BlockSpec((S_LAT, 256), lat(1)),
                  pl.BlockSpec((S_LAT, 256), lat(2)),
                  pl.BlockSpec((PAST, 256), lambda b, r: (b, 0)), pl.BlockSpec((PAST, 256), lambda b, r: (b, 0)),
                  pl.BlockSpec((1, NA_HEADS, GRID_W, NK_LOC), boff)],
        out_specs=pl.BlockSpec((GRID_W, 256), lambda b, r: (b * ROWS + r, 0)),
        out_shape=jax.ShapeDtypeStruct((T_LAT, 256), F32),
        compiler_params=_cp(48),
        name="lat_na",
    )(big, big, big, kc, vc, bias)


def _log_sigmoid(x):
    return jnp.minimum(x, 0.0) - jnp.log1p(jnp.exp(-jnp.abs(x)))


def _gla_chunk(rev, q, k, v, la, st):
    C, SB = GLA_CHUNK, GLA_SUB
    ri = lax.broadcasted_iota(jnp.int32, (C, C), 0)
    ci = lax.broadcasted_iota(jnp.int32, (C, C), 1)
    tri = ((ci >= ri) if rev else (ci <= ri)).astype(F32)
    cb = _dot(tri, la, precision=HIGHEST)
    cl = cb[0:1] if rev else cb[C - 1:C]
    o = _dot_nt((q * jnp.exp(cb)).astype(BF16), st.astype(BF16))
    kd = (k * jnp.exp(cl - cb)).astype(BF16)
    bd = (lax.broadcasted_iota(jnp.int32, (256, LANES), 0) // GLA_DV
          == lax.broadcasted_iota(jnp.int32, (256, LANES), 1) // GLA_DK)
    st_new = st * jnp.exp(cl) + jnp.where(bd, _dot_tn(v.astype(BF16), kd), 0.0)

    vb = v.astype(BF16)
    lane_h = lax.broadcasted_iota(jnp.int32, (SB, LANES), 1) // GLA_DK
    col_h = lax.broadcasted_iota(jnp.int32, (SB, 256), 1) // GLA_DV
    expand = (lax.broadcasted_iota(jnp.int32, (LANES, 256), 0) // GLA_DK
              == lax.broadcasted_iota(jnp.int32, (LANES, 256), 1) // GLA_DV).astype(BF16)
    row = lax.broadcasted_iota(jnp.int32, (C, LANES), 0)
    pj = lax.broadcasted_iota(jnp.int32, (SB * SB, LANES), 0) // SB
    pi = lax.broadcasted_iota(jnp.int32, (SB * SB, LANES), 0) % SB
    pmask = (pi <= pj) if rev else (pi >= pj)
    nsub = C // SB
    parts = []
    for i in range(nsub):
        sl = slice(i * SB, (i + 1) * SB)
        qi, ki, cbi, vi = q[sl], k[sl], cb[sl], v[sl]
        acc = o[sl]
        has_off = (i < nsub - 1) if rev else (i > 0)
        if has_off:
            if rev:
                ref = cb[(i + 1) * SB:(i + 1) * SB + 1]
                outside = row >= (i + 1) * SB
            else:
                ref = cb[i * SB - 1:i * SB]
                outside = row < i * SB
            a = qi * jnp.exp(cbi - ref)
            bm = jnp.where(outside, k * jnp.exp(jnp.minimum(ref - cb, 0.0)), 0.0).astype(BF16)
            a4 = jnp.concatenate([jnp.where(lane_h == h, a, 0.0) for h in range(GLA_HEADS)], axis=0)
            att = _dot_nt(a4.astype(BF16), bm)
            pv = _dot(att.astype(BF16), vb)
            for h in range(GLA_HEADS):
                acc = acc + jnp.where(col_h == h, pv[h * SB:(h + 1) * SB], 0.0)
        kj = jnp.broadcast_to(ki[:, None, :], (SB, SB, LANES)).reshape(SB * SB, LANES)
        cbj = jnp.broadcast_to(cbi[:, None, :], (SB, SB, LANES)).reshape(SB * SB, LANES)
        qq = jnp.broadcast_to(qi[None, :, :], (SB, SB, LANES)).reshape(SB * SB, LANES)
        cbq = jnp.broadcast_to(cbi[None, :, :], (SB, SB, LANES)).reshape(SB * SB, LANES)
        x = qq * kj * jnp.where(pmask, jnp.exp(jnp.where(pmask, cbq - cbj, 0.0)), 0.0)
        y = _dot(x.astype(BF16), expand).reshape(SB, SB, 256)
        acc = acc + jnp.sum(y * vi[:, None, :], axis=0)
        parts.append(acc)
    return jnp.concatenate(parts, axis=0), st_new


def _gla_body(S, has_s0, *refs):
    if has_s0:
        (gq_ref, gk_ref, gv_ref, gg_ref, seg_ref, wg_ref, bg_ref, gn_ref, s0_ref,
         o_ref, sfin_ref, la_scr, of_scr, ob_scr, st_scr) = refs
    else:
        (gq_ref, gk_ref, gv_ref, gg_ref, seg_ref, wg_ref, bg_ref, gn_ref,
         o_ref, sfin_ref, la_scr, of_scr, ob_scr, st_scr) = refs
    nc = S // GLA_CHUNK
    seg = seg_ref[...].astype(BF16)
    for d in range(2):
        la_scr[d] = _log_sigmoid(_dot(seg, wg_ref[d]) + bg_ref[d]) * (1.0 / GLA_NORM)
        if has_s0:
            st_scr[d] = s0_ref[0, d]
        else:
            st_scr[d] = jnp.zeros((256, LANES), F32)

    def body(c, carry):
        for d, out_scr in ((0, of_scr), (1, ob_scr)):
            r0 = pl.multiple_of((c if d == 0 else nc - 1 - c) * GLA_CHUNK, GLA_CHUNK)
            rows = pl.ds(r0, GLA_CHUNK)
            q = gq_ref[rows, :] * (GLA_DK ** -0.5)
            o, st = _gla_chunk(d == 1, q, gk_ref[rows, :], gv_ref[rows, :], la_scr[d, rows, :], st_scr[d])
            out_scr[rows, :] = o
            st_scr[d] = st
        return carry

    lax.fori_loop(0, nc, body, 0)
    sfin_ref[0] = st_scr[...]
    o = of_scr[...] + ob_scr[...]
    ones = (lax.broadcasted_iota(jnp.int32, (256, 256), 0) // GLA_DV
            == lax.broadcasted_iota(jnp.int32, (256, 256), 1) // GLA_DV).astype(F32)
    ms = _dot(o * o, ones, precision=HIGHEST) * (1.0 / GLA_DV)
    g = gg_ref[...]
    o_ref[...] = o * lax.rsqrt(ms + EPS) * gn_ref[...] * (g * jax.nn.sigmoid(g))


def _gla(big, seg, lw, s0):
    has_s0 = s0 is not None
    S, nb, base = (S_LAT, B_LAT, T_CTX // S_LAT) if has_s0 else (S_CTX, B_CTX, 0)
    blk = lambda w, c: pl.BlockSpec((S, w), lambda b: (base + b, c))
    in_specs = [blk(LANES, C_GQ // LANES), blk(LANES, C_GK // LANES), blk(256, C_GV // 256), blk(256, C_GG // 256),
                blk(LANES, 0), _const_spec((2, LANES, LANES)), _const_spec((2, 1, LANES)), _const_spec((1, 256))]
    args = [big, big, big, big, seg, lw["wg"], lw["bg"], lw["gla_norm_g"]]
    if has_s0:
        in_specs += [pl.BlockSpec((1, 2, 256, LANES), lambda b: (b, 0, 0, 0))]
        args += [s0]
    return pl.pallas_call(
        functools.partial(_gla_body, S, has_s0),
        grid=(nb,),
        in_specs=in_specs,
        out_specs=[pl.BlockSpec((S, 256), lambda b: (b, 0)), pl.BlockSpec((1, 2, 256, LANES), lambda b: (b, 0, 0, 0))],
        out_shape=[jax.ShapeDtypeStruct((nb * S, 256), F32), jax.ShapeDtypeStruct((nb, 2, 256, LANES), F32)],
        scratch_shapes=[pltpu.VMEM((2, S, LANES), F32), pltpu.VMEM((S, 256), F32), pltpu.VMEM((S, 256), F32),
                        pltpu.VMEM((2, 256, LANES), F32)],
        compiler_params=_cp(48),
        name="gla_lat" if has_s0 else "gla_ctx",
    )(*args)


def _softplus(x):
    return jnp.maximum(x, 0.0) + jnp.log1p(jnp.exp(-jnp.abs(x)))


def _lru_body(S, has_h0, *refs):
    if has_h0:
        (lx_ref, lg_ref, cw_ref, cb_ref, wg_ref, bgate_ref, lam_ref, h0_ref,
         o_ref, hfin_ref, a_scr, b_scr, hf_scr, hb_scr) = refs
    else:
        (lx_ref, lg_ref, cw_ref, cb_ref, wg_ref, bgate_ref, lam_ref,
         o_ref, hfin_ref, a_scr, b_scr, hf_scr, hb_scr) = refs
    x = lx_ref[...]
    row = lax.broadcasted_iota(jnp.int32, (S, LRU_W), 0)
    xm1 = jnp.where(row >= 1, pltpu.roll(x, 1, 0), 0.0)
    xp1 = jnp.where(row < S - 1, pltpu.roll(x, S - 1, 0), 0.0)
    xp2 = jnp.where(row < S - 2, pltpu.roll(x, S - 2, 0), 0.0)
    xc = cb_ref[...] + xm1 * cw_ref[0:1] + x * cw_ref[1:2] + xp1 * cw_ref[2:3] + xp2 * cw_ref[3:4]
    gates = _dot(xc.astype(BF16), wg_ref[...]) + bgate_ref[...]
    for d in range(2):
        r = jax.nn.sigmoid(gates[:, d * LRU_W:(d + 1) * LRU_W])
        ig = jax.nn.sigmoid(gates[:, (2 + d) * LRU_W:(3 + d) * LRU_W])
        log_a = -LRU_C * r * _softplus(-lam_ref[d])
        a_scr[d] = jnp.exp(log_a)
        b_scr[d] = jnp.sqrt(1.0 - jnp.exp(2.0 * log_a)) * (ig * xc)
    if has_h0:
        h_init = (h0_ref[0, 0], h0_ref[0, 1])
    else:
        h_init = (jnp.zeros((1, LRU_W), F32), jnp.zeros((1, LRU_W), F32))

    def body(i, carry):
        hf, hb = carry
        r0 = pl.multiple_of(i * 8, 8)
        rb = pl.multiple_of(S - 8 - i * 8, 8)
        af, bf = a_scr[0, pl.ds(r0, 8), :], b_scr[0, pl.ds(r0, 8), :]
        ab, bb = a_scr[1, pl.ds(rb, 8), :], b_scr[1, pl.ds(rb, 8), :]
        fs, bs = [], [None] * 8
        for t in range(8):
            hf = af[t:t + 1] * hf + bf[t:t + 1]
            fs.append(hf)
            u = 7 - t
            hb = ab[u:u + 1] * hb + bb[u:u + 1]
            bs[u] = hb
        hf_scr[pl.ds(r0, 8), :] = jnp.concatenate(fs, axis=0)
        hb_scr[pl.ds(rb, 8), :] = jnp.concatenate(bs, axis=0)
        return hf, hb

    hf, hb = lax.fori_loop(0, S // 8, body, h_init)
    hfin_ref[0, 0] = hf
    hfin_ref[0, 1] = hb
    g = lg_ref[...]
    gelu = g * (0.5 * (1.0 + jnp.tanh(np.sqrt(2.0 / np.pi).astype(np.float32) * (g + 0.044715 * (g * g * g)))))
    o_ref[...] = (hf_scr[...] + hb_scr[...]) * gelu


def _lru(big, lw, h0):
    has_h0 = h0 is not None
    S, nb, base = (S_LAT, B_LAT, T_CTX // S_LAT) if has_h0 else (S_CTX, B_CTX, 0)
    blk = lambda c: pl.BlockSpec((S, LRU_W), lambda b: (base + b, c))
    in_specs = [blk(C_LX // LRU_W), blk(C_LG // LRU_W), _const_spec((4, LRU_W)), _const_spec((1, LRU_W)),
                _const_spec((LRU_W, 4 * LRU_W)), _const_spec((1, 4 * LRU_W)), _const_spec((2, 1, LRU_W))]
    args = [big, big, lw["lru_conv_w"], lw["lru_conv_b"], lw["lru_wg"], lw["lru_bg"], lw["lru_lambda"]]
    if has_h0:
        in_specs += [pl.BlockSpec((1, 2, 1, LRU_W), lambda b: (b, 0, 0, 0))]
        args += [h0]
    return pl.pallas_call(
        functools.partial(_lru_body, S, has_h0),
        grid=(nb,),
        in_specs=in_specs,
        out_specs=[pl.BlockSpec((S, LRU_W), lambda b: (b, 0)),
                   pl.BlockSpec((1, 2, 1, LRU_W), lambda b: (b, 0, 0, 0))],
        out_shape=[jax.ShapeDtypeStruct((nb * S, LRU_W), F32), jax.ShapeDtypeStruct((nb, 2, 1, LRU_W), F32)],
        scratch_shapes=[pltpu.VMEM((2, S, LRU_W), F32), pltpu.VMEM((2, S, LRU_W), F32),
                        pltpu.VMEM((S, LRU_W), F32), pltpu.VMEM((S, LRU_W), F32)],
        compiler_params=_cp(48),
        name="lru_lat" if has_h0 else "lru_ctx",
    )(*args)


def _merge_body(x_ref, *refs):
    (mod_ref, wout_ref, g2_ref, rwh_ref, rwl_ref, rb_ref, x1_ref, h2_ref, idx_ref, gw_ref) = refs[8:]
    is_ctx = pl.program_id(0) < NT_CTX
    mix = 0.0
    for j in range(4):
        o = jnp.where(is_ctx, refs[2 * j][...], refs[2 * j + 1][...])
        mix = mix + _dot(o.astype(BF16), wout_ref[j * 256:(j + 1) * 256, :])
    m = mod_ref[0]
    g1, sh2, sc2 = m[:, 2 * D:3 * D], m[:, 3 * D:4 * D], m[:, 4 * D:5 * D]
    x1 = x_ref[...] + g1 * mix
    x1_ref[...] = x1
    h2 = _rms(x1) * g2_ref[...] * (1.0 + sc2) + sh2
    for s in range(SUB):
        h2_ref[pl.ds(s, TM, stride=SUB), :] = h2[:, s * LANES:(s + 1) * LANES]
    hi = h2.astype(BF16)
    lo = (h2 - hi.astype(F32)).astype(BF16)
    logits = _dot(hi, rwh_ref[...]) + _dot(hi, rwl_ref[...]) + _dot(lo, rwh_ref[...]) + rb_ref[...]
    lane = lax.broadcasted_iota(jnp.int32, (TM, LANES), 1).astype(F32)
    idx_out = jnp.zeros((TM, LANES), F32)
    val_out = jnp.zeros((TM, LANES), F32)
    top = None
    den = 0.0
    for kk in range(TOP_K):
        mval = logits.max(axis=-1, keepdims=True)
        midx = jnp.where(logits == mval, lane, float(LANES)).min(axis=-1, keepdims=True)
        if kk == 0:
            top = mval
        e = jnp.exp(mval - top)
        den = den + e
        idx_out = jnp.where(lane == kk, midx, idx_out)
        val_out = jnp.where(lane == kk, e, val_out)
        logits = jnp.where(lane == midx, -jnp.inf, logits)
    idx_ref[...] = idx_out.astype(jnp.int32)
    gw_ref[...] = val_out / den


def _merge(x, mix_parts, mods_l, lw):
    tok = lambda w: pl.BlockSpec((TM, w), lambda i: (i, 0))
    ctx = pl.BlockSpec((TM, 256), lambda i: (jnp.minimum(i, NT_CTX - 1), 0))
    lat = pl.BlockSpec((TM, 256), lambda i: (jnp.maximum(i - NT_CTX, 0), 0))
    return pl.pallas_call(
        _merge_body,
        grid=(NT_ALL,),
        in_specs=[tok(D)] + [ctx, lat] * 4 + [
                  pl.BlockSpec((1, 1, 6 * D), lambda i: (_mod_row(i), 0, 0)),
                  _const_spec((D, D)), _const_spec((1, D)), _const_spec((D, LANES)), _const_spec((D, LANES)),
                  _const_spec((1, LANES))],
        out_specs=[tok(D), pl.BlockSpec((TM * SUB, LANES), lambda i: (i, 0)), tok(LANES), tok(LANES)],
        out_shape=[jax.ShapeDtypeStruct((T_ALL, D), F32), jax.ShapeDtypeStruct((T_ALL * SUB, LANES), F32),
                   jax.ShapeDtypeStruct((T_ALL, LANES), jnp.int32), jax.ShapeDtypeStruct((T_ALL, LANES), F32)],
        compiler_params=_cp(40),
        name="merge",
    )(x, *mix_parts, mods_l, lw["w_out"], lw["norm2_g"], lw["rw_hi"], lw["rw_lo"], lw["rb"])


def _moe_body(sn_ref, te_ref, nv_ref, tb_ref, h2_hbm, wgu_ref, bgu_ref, wd_ref, bd_ref,
              y_hbm, xbuf, obuf, wgu_bf, wd_bf, gsem, ssem):
    i = pl.program_id(0)
    slot = i % 2
    cur = jnp.minimum(i, NT_MOE - 1)
    valid = (i < NT_MOE) & (nv_ref[cur] > 0)
    prv = jnp.maximum(i - 1, 0)
    nv_prev = jnp.where(i >= 1, nv_ref[prv], 0)
    base_prev = tb_ref[prv]
    base_next = tb_ref[jnp.minimum(i + 1, NT_MOE - 1)]

    def gather_row(base, sl, r):
        tok = jnp.bitwise_and(sn_ref[base + r], T_ALL - 1)
        pltpu.make_async_copy(h2_hbm.at[pl.ds(pl.multiple_of(tok * SUB, SUB), SUB)],
                              xbuf.at[sl, pl.ds(pl.multiple_of(r * SUB, SUB), SUB)], gsem.at[sl]).start()

    def scatter_row(sl, r):
        dst = jnp.where(r < nv_prev, sn_ref[base_prev + r], N_ROWS + sl * TM_MOE + r)
        pltpu.make_async_copy(obuf.at[sl, pl.ds(pl.multiple_of(r * SUB, SUB), SUB)],
                              y_hbm.at[pl.ds(pl.multiple_of(dst * SUB, SUB), SUB)], ssem.at[sl]).start()

    def for_other_slot(fn, lo=0, hi=TM_MOE):
        for other in range(2):
            @pl.when(slot == 1 - other)
            def _():
                lax.fori_loop(lo, hi, lambda r, c: fn(other, r) or c, 0, unroll=8)

    def gather_done(sl):
        pltpu.make_async_copy(h2_hbm.at[pl.ds(0, TM_MOE * SUB)], xbuf.at[sl], gsem.at[sl]).wait()

    def scatter_done(sl):
        pltpu.make_async_copy(obuf.at[sl], y_hbm.at[pl.ds(0, TM_MOE * SUB)], ssem.at[sl]).wait()

    @pl.when(i == 0)
    def _():
        obuf[1] = jnp.zeros((TM_MOE * SUB, LANES), F32)
        cp = pltpu.make_async_copy(obuf.at[1], y_hbm.at[pl.ds(N_ROWS * SUB, TM_MOE * SUB)], ssem.at[1])
        cp.start()
        cp.wait()

        def body(r, c):
            gather_row(tb_ref[0], 0, r)
            return c
        lax.fori_loop(0, TM_MOE, body, 0, unroll=8)

    e = te_ref[cur]
    prev_e = te_ref[prv]

    @pl.when(valid & ((i == 0) | (e != prev_e)))
    def _():
        wgu_bf[...] = wgu_ref[0, 0].astype(BF16)
        wd_bf[...] = wd_ref[0, 0].astype(BF16)

    def compute(this):
        other = 1 - this
        gather_done(this)
        x = jnp.concatenate([xbuf[this, pl.ds(s, TM_MOE, stride=SUB), :] for s in range(SUB)], axis=-1).astype(BF16)
        acc = jnp.zeros((TM_MOE, D), F32)
        n_chunk = D_EXP // EXP_CHUNK
        part = TM_MOE // n_chunk
        for j in range(n_chunk):
            c0 = j * EXP_CHUNK
            for r in range(j * part, (j + 1) * part):
                gather_row(base_next, other, r)
                scatter_row(other, r)
            xg = _dot(x, wgu_bf[:, c0:c0 + EXP_CHUNK]) + bgu_ref[0, 0][:, c0:c0 + EXP_CHUNK]
            xl = _dot(x, wgu_bf[:, D_EXP + c0:D_EXP + c0 + EXP_CHUNK]) + bgu_ref[0, 0][:, D_EXP + c0:D_EXP + c0 + EXP_CHUNK]
            xg = jnp.minimum(xg, SWIGLU_LIMIT)
            xl = jnp.clip(xl, -SWIGLU_LIMIT, SWIGLU_LIMIT)
            act = xg * jax.nn.sigmoid(SWIGLU_ALPHA * xg) * (xl + 1.0)
            acc = acc + _dot(act.astype(BF16), wd_bf[c0:c0 + EXP_CHUNK, :])
        out = acc + bd_ref[0, 0]
        for s in range(SUB):
            obuf[this, pl.ds(s, TM_MOE, stride=SUB), :] = out[:, s * LANES:(s + 1) * LANES]
        scatter_done(other)

    for parity in range(2):
        pl.when(valid & (slot == parity))(functools.partial(compute, parity))

    @pl.when(jnp.logical_not(valid) & (nv_prev > 0))
    def _():
        gather_done(slot)
        for_other_slot(scatter_row)
        scatter_done(1 - slot)


def _moe(sorted_n, tile_exp, tile_nvalid, tile_base, h2, l, p):
    wmap = lambda i, sn, te, nv, tb: (l, te[jnp.minimum(i, NT_MOE - 1)], 0, 0)
    grid_spec = pltpu.PrefetchScalarGridSpec(
        num_scalar_prefetch=4,
        grid=(NT_MOE + 1,),
        in_specs=[
            pl.BlockSpec(memory_space=pl.ANY),
            pl.BlockSpec((1, 1, D, 2 * D_EXP), wmap),
            pl.BlockSpec((1, 1, 1, 2 * D_EXP), wmap),
            pl.BlockSpec((1, 1, D_EXP, D), wmap),
            pl.BlockSpec((1, 1, 1, D), wmap),
        ],
        out_specs=pl.BlockSpec(memory_space=pl.ANY),
        scratch_shapes=[pltpu.VMEM((2, TM_MOE * SUB, LANES), F32), pltpu.VMEM((2, TM_MOE * SUB, LANES), F32),
                        pltpu.VMEM((D, 2 * D_EXP), BF16), pltpu.VMEM((D_EXP, D), BF16),
                        pltpu.SemaphoreType.DMA((2,)), pltpu.SemaphoreType.DMA((2,))],
    )
    return pl.pallas_call(
        _moe_body,
        grid_spec=grid_spec,
        out_shape=jax.ShapeDtypeStruct(((N_ROWS + N_DUMP) * SUB, LANES), F32),
        compiler_params=pltpu.CompilerParams(vmem_limit_bytes=56 * 1024 * 1024, dimension_semantics=("arbitrary",),
                                             disable_bounds_checks=True),
        name="moe",
    )(sorted_n, tile_exp, tile_nvalid, tile_base, h2,
      p["exp_w_gu"], p["exp_b_gu"].reshape(DEPTH, N_EXP, 1, 2 * D_EXP),
      p["exp_w_down"], p["exp_b_down"].reshape(DEPTH, N_EXP, 1, D))


def _route(idx):
    flat = idx.T.reshape(-1)
    ids = jnp.arange(N_ROWS, dtype=jnp.int32)
    sorted_n = jnp.concatenate([jnp.sort(flat * N_ROWS + ids) % N_ROWS, jnp.zeros((TM_MOE,), jnp.int32)])
    experts = jnp.arange(N_EXP, dtype=jnp.int32)
    counts = jnp.sum((flat[None, :] == experts[:, None]).astype(jnp.int32), axis=1)
    padded = ((counts + TM_MOE - 1) // TM_MOE) * TM_MOE
    pad_ends = jnp.cumsum(padded)
    pad_starts = pad_ends - padded
    unp_starts = jnp.cumsum(counts) - counts
    tile_start = jnp.arange(NT_MOE, dtype=jnp.int32) * TM_MOE
    tile_exp = jnp.minimum(jnp.sum((pad_ends[None, :] <= tile_start[:, None]).astype(jnp.int32), axis=1), N_EXP - 1)
    onehot = (tile_exp[:, None] == experts[None, :]).astype(jnp.int32)
    pick = lambda v: jnp.sum(onehot * v[None, :], axis=1)
    offset = tile_start - pick(pad_starts)
    tile_nvalid = jnp.where(tile_start < pad_ends[-1], jnp.clip(pick(counts) - offset, 0, TM_MOE), 0)
    tile_base = jnp.clip(pick(unp_starts) + offset, 0, N_ROWS - 1)
    return sorted_n, tile_exp, tile_nvalid.astype(jnp.int32), tile_base.astype(jnp.int32)


def _final_body(x_ref, y0_ref, y1_ref, y2_ref, y3_ref, gw_ref, mod_ref, g_ref, o_ref):
    g2 = mod_ref[0][:, 5 * D:6 * D]
    o_ref[...] = _rms(x_ref[...] + g2 * _sum_slots((y0_ref, y1_ref, y2_ref, y3_ref), gw_ref)) * g_ref[...]


def _final(x1, moe, gw, mods_l, g, tile0, ntiles):
    tok = pl.BlockSpec((TM, D), lambda i: (tile0 + i, 0))
    return pl.pallas_call(
        _final_body,
        grid=(ntiles,),
        in_specs=[tok] + _slot_specs(tile0) + [
                  pl.BlockSpec((TM, LANES), lambda i: (tile0 + i, 0)),
                  pl.BlockSpec((1, 1, 6 * D), lambda i: (_mod_row(tile0 + i), 0, 0)), _const_spec((1, D))],
        out_specs=pl.BlockSpec((TM, D), lambda i: (i, 0)),
        out_shape=jax.ShapeDtypeStruct((ntiles * TM, D), F32),
        name="final_norm",
    )(x1, *([moe] * TOP_K), gw, mods_l, g)


def _rope_perm(w):
    parts = []
    for a in range(2):
        x1, x2 = w[..., a * 16:a * 16 + 8], w[..., a * 16 + 8:a * 16 + 16]
        parts += [-x2, x1]
    return jnp.concatenate(parts, axis=-1)


def _rope_tables():
    t = np.arange(S_LAT)
    freqs = ROPE_BASE ** (-np.arange(8, dtype=np.float64) / 8)
    cs, sn = [], []
    for pos in (t // GRID_W, t % GRID_W):
        ang = pos.astype(np.float64)[:, None] * freqs[None, :]
        cs += [np.cos(ang), np.cos(ang)]
        sn += [np.sin(ang), np.sin(ang)]
    cos = np.concatenate([np.ones((ID_ROWS, 32), np.float32), np.concatenate(cs, -1).astype(np.float32)], 0)
    sin = np.concatenate([np.zeros((ID_ROWS, 32), np.float32), np.concatenate(sn, -1).astype(np.float32)], 0)
    n = cos.shape[0]
    one, zero = np.ones((n, 1), np.float32), np.zeros((n, 1), np.float32)
    cq_h = np.concatenate([np.repeat(one, 64, 1), cos, np.repeat(zero, 32, 1)], 1)
    sq_h = np.concatenate([np.repeat(zero, 64, 1), sin, np.repeat(zero, 32, 1)], 1)
    cseg = np.concatenate([cos, np.repeat(one, 96, 1)], 1)
    sseg = np.concatenate([sin, np.repeat(zero, 96, 1)], 1)
    return dict(cq=jnp.asarray(np.tile(cq_h, (1, 4))), sq=jnp.asarray(np.tile(sq_h, (1, 4))),
                cs=jnp.asarray(cseg), ss=jnp.asarray(sseg))


def _na_bias_table(rpb):
    qc = np.arange(GRID_W)[:, None]
    kc = np.arange(GRID_W)[None, :]
    cs = np.clip(qc - NA_WIN_W // 2, 0, GRID_W - NA_WIN_W)
    valid = (kc >= cs) & (kc < cs + NA_WIN_W)
    pad = GRID_W - NA_WIN_W
    ext = jnp.pad(rpb, ((0, 0), (0, 0), (pad, pad)))
    toe = jnp.stack([ext[:, :, GRID_W - 1 - q:2 * GRID_W - 1 - q] for q in range(GRID_W)], axis=2)
    toe = jnp.where(valid[None, None], toe, NEG)
    tab = jnp.stack([toe[:, o:o + NA_WIN_H] for o in range(NA_WIN_H)], axis=0)
    return tab.transpose(0, 1, 3, 2, 4).reshape(NA_WIN_H, NA_HEADS, GRID_W, NK_LOC).astype(F32)


def _layer_weights(l, p):
    w_in = p["w_in"][l]
    z = lambda n: jnp.zeros((D, n), F32)
    kr = w_in[:, 320:352]
    w_perm = jnp.concatenate([
        w_in[:, 352:1120], w_in[:, 1120:1248], w_in[:, 1248:1376], w_in[:, 1376:1632], w_in[:, 1632:1888],
        w_in[:, 1920:2176], w_in[:, 2176:2432], w_in[:, 192:320], w_in[:, 0:192], z(64),
        kr, w_in[:, 1888:1920], z(64), _rope_perm(kr), z(96)], axis=1).astype(BF16)
    wuq = p["mla_w_uq"][l].reshape(MLA_Q_LORA, MLA_HEADS, MLA_NOPE + MLA_ROPE)
    zq = jnp.zeros((MLA_Q_LORA, MLA_HEADS, 32), F32)
    wq1 = jnp.concatenate([wuq, zq], axis=-1).reshape(MLA_Q_LORA, 512)
    wq2 = jnp.concatenate([jnp.zeros((MLA_Q_LORA, MLA_HEADS, MLA_NOPE), F32), _rope_perm(wuq[..., MLA_NOPE:]), zq],
                          axis=-1).reshape(MLA_Q_LORA, 512)
    wq = jnp.pad(jnp.concatenate([wq1, wq2], axis=1), ((0, 256 - MLA_Q_LORA), (0, 0))).astype(BF16)
    qg = jnp.pad(p["mla_q_norm_g"][l], (0, 256 - MLA_Q_LORA)).reshape(1, 256)
    wukv = p["mla_w_ukv"][l].reshape(MLA_KV_LORA, MLA_HEADS, MLA_NOPE + MLA_V)
    wk = jnp.concatenate([wukv[..., :MLA_NOPE], jnp.zeros((MLA_KV_LORA, MLA_HEADS, 64), F32)], -1).reshape(MLA_KV_LORA, 512)
    wv = wukv[..., MLA_NOPE:].reshape(MLA_KV_LORA, 256)
    eye = jnp.eye(MLA_ROPE, dtype=F32)[:, None, :]
    ek = jnp.concatenate([jnp.zeros((MLA_ROPE, MLA_HEADS, MLA_NOPE), F32),
                          jnp.broadcast_to(eye, (MLA_ROPE, MLA_HEADS, MLA_ROPE)),
                          jnp.zeros((MLA_ROPE, MLA_HEADS, 32), F32)], -1).reshape(MLA_ROPE, 512)
    wkv = jnp.concatenate([
        jnp.concatenate([wk, wv], axis=1),
        jnp.concatenate([ek, jnp.zeros((MLA_ROPE, 256), F32)], axis=1),
        jnp.zeros((LANES - MLA_ROPE, 768), F32)], axis=0).astype(BF16)
    wg = jnp.stack([jnp.pad(p["gla_w_gate"][l, d], ((SEG_GA + d * GLA_RANK, LANES - SEG_GA - (d + 1) * GLA_RANK), (0, 0)))
                    for d in range(2)])
    def bdiag(w):
        eye = jnp.eye(LRU_BLOCKS, dtype=F32)
        return (w[:, :, None, :] * eye[:, None, :, None]).reshape(LRU_W, LRU_W)
    lru_wg = jnp.concatenate([bdiag(p["lru_w_a"][l, 0]), bdiag(p["lru_w_a"][l, 1]),
                              bdiag(p["lru_w_x"][l, 0]), bdiag(p["lru_w_x"][l, 1])], axis=1).astype(BF16)
    lru_bg = jnp.concatenate([p["lru_b_a"][l, 0], p["lru_b_a"][l, 1], p["lru_b_x"][l, 0], p["lru_b_x"][l, 1]]).reshape(1, 4 * LRU_W)
    rw = jnp.pad(p["router_w"][l], ((0, 0), (0, LANES - N_EXP)))
    rw_hi = rw.astype(BF16)
    rw_lo = (rw - rw_hi.astype(F32)).astype(BF16)
    rb = jnp.concatenate([p["router_b"][l], jnp.full((LANES - N_EXP,), -jnp.inf, F32)]).reshape(1, LANES)
    return dict(
        w_in=w_perm, norm1_g=p["norm1_g"][l].reshape(1, D), norm2_g=p["norm2_g"][l].reshape(1, D),
        qg=qg, wq=wq, kvg=p["mla_kv_norm_g"][l].reshape(1, LANES), wkv=wkv,
        wg=wg.astype(BF16), bg=p["gla_b_gate"][l].reshape(2, 1, LANES), gla_norm_g=p["gla_norm_g"][l].reshape(1, 256),
        lru_conv_w=p["lru_conv_w"][l], lru_conv_b=p["lru_conv_b"][l].reshape(1, LRU_W), lru_wg=lru_wg, lru_bg=lru_bg,
        lru_lambda=p["lru_lambda"][l].reshape(2, 1, LRU_W),
        w_out=p["w_out"][l].astype(BF16), rw_hi=rw_hi, rw_lo=rw_lo, rb=rb,
        na_bias=_na_bias_table(p["na_rpb"][l]),
    )


def _state_to_blockdiag(s):
    eye = jnp.eye(GLA_HEADS, dtype=s.dtype)
    return jnp.einsum("bdhkv,hg->bdhvgk", s, eye).reshape(s.shape[0], 2, 256, LANES)


def _blockdiag_to_state(st):
    b = st.shape[0]
    s6 = st.reshape(b, 2, GLA_HEADS, GLA_DV, GLA_HEADS, GLA_DK)
    diag = jnp.stack([s6[:, :, h, :, h, :] for h in range(GLA_HEADS)], axis=2)
    return diag.transpose(0, 1, 2, 4, 3)


def kernel(x_prompt, x_sample, cache_mla_ckv, cache_mla_krope, cache_na_k, cache_na_v, state_gla, state_lru, c, c_ctx, norm1_g, norm2_g, w_mod, b_mod, w_in, w_out, mla_q_norm_g, mla_w_uq, mla_kv_norm_g, mla_w_ukv, na_rpb, gla_w_gate, gla_b_gate, gla_norm_g, lru_conv_w, lru_conv_b, lru_w_a, lru_b_a, lru_w_x, lru_b_x, lru_lambda, router_w, router_b, exp_w_gu, exp_b_gu, exp_w_down, exp_b_down, final_norm_g):
    p = dict(norm1_g=norm1_g, norm2_g=norm2_g, w_in=w_in, w_out=w_out, mla_q_norm_g=mla_q_norm_g, mla_w_uq=mla_w_uq,
             mla_kv_norm_g=mla_kv_norm_g, mla_w_ukv=mla_w_ukv, na_rpb=na_rpb, gla_w_gate=gla_w_gate,
             gla_b_gate=gla_b_gate, gla_norm_g=gla_norm_g, lru_conv_w=lru_conv_w, lru_conv_b=lru_conv_b,
             lru_w_a=lru_w_a, lru_b_a=lru_b_a, lru_w_x=lru_w_x, lru_b_x=lru_b_x, lru_lambda=lru_lambda,
             router_w=router_w, router_b=router_b, exp_w_gu=exp_w_gu, exp_b_gu=exp_b_gu, exp_w_down=exp_w_down,
             exp_b_down=exp_b_down)
    x = jnp.concatenate([x_prompt.reshape(T_CTX, D), x_sample.reshape(T_LAT, D)], axis=0)
    crow = jnp.concatenate([c_ctx[None, :], c, jnp.zeros((8 - 1 - B_LAT, D), F32)], axis=0)
    mods = _modulation(crow, w_mod, b_mod).reshape(DEPTH, 8, 1, 6 * D)
    tabs = _rope_tables()

    col = [[] for _ in range(6)]
    moe = None
    x1 = None
    for l in range(DEPTH):
        lw = _layer_weights(l, p)
        if l == 0:
            x, big, seg, ckv, q, k, v = _proj(True, x, None, None, None, mods[l], lw, tabs)
        else:
            x, big, seg, ckv, q, k, v = _proj(False, x1, moe, gw, mods[l - 1], mods[l], lw, tabs)
        kr_c = jnp.pad(cache_mla_krope[:, l].reshape(B_LAT * PAST, MLA_ROPE), ((0, 0), (0, LANES - MLA_ROPE)))
        kc, vc = _cache_kv(cache_mla_ckv[:, l].reshape(B_LAT * PAST, MLA_KV_LORA), kr_c, lw["wkv"])
        oa_c, ob_c = _ctx_attn(q, k, v, big)
        oa_l = _lat_mla(q, k, v, kc, vc)
        ob_l = _lat_na(big, cache_na_k[:, l].reshape(B_LAT * PAST, 256), cache_na_v[:, l].reshape(B_LAT * PAST, 256),
                       lw["na_bias"])
        oc_c, sg = _gla(big, seg, lw, None)
        oc_l, _ = _gla(big, seg, lw, _state_to_blockdiag(state_gla[:, l]))
        od_c, sl = _lru(big, lw, None)
        od_l, _ = _lru(big, lw, state_lru[:, l].reshape(B_LAT, 2, 1, LRU_W))
        x1, h2, idx, gw = _merge(x, (oa_c, oa_l, ob_c, ob_l, oc_c, oc_l, od_c, od_l), mods[l], lw)
        moe = _moe(*_route(idx[:, :TOP_K]), h2, l, p)
        col[0].append(ckv[:T_CTX].reshape(B_CTX, S_CTX, MLA_KV_LORA))
        col[1].append(seg[:T_CTX, :MLA_ROPE].reshape(B_CTX, S_CTX, MLA_ROPE))
        col[2].append(big[:T_CTX, C_NAK:C_NAK + 256].reshape(B_CTX, S_CTX, NA_HEADS, NA_DH))
        col[3].append(big[:T_CTX, C_NAV:C_NAV + 256].reshape(B_CTX, S_CTX, NA_HEADS, NA_DH))
        col[4].append(_blockdiag_to_state(sg))
        col[5].append(sl.reshape(B_CTX, 2, LRU_W))
    fg = final_norm_g.reshape(1, D)
    y_prompt = _final(x1, moe, gw, mods[DEPTH - 1], fg, 0, NT_CTX).reshape(B_CTX, S_CTX, D)
    y_sample = _final(x1, moe, gw, mods[DEPTH - 1], fg, NT_CTX, NT_ALL - NT_CTX).reshape(B_LAT, S_LAT, D)
    return (y_prompt, y_sample) + tuple(jnp.stack(cl, axis=1) for cl in col)
```
